```python
import math
import jax, jax.numpy as jnp
from jax import lax
import numpy as np

D_MODEL = 1024
BATCH = 4
SEQ = 4096
DEPTH = 2
DEC_BATCH = 16
DEC_SEQ = 2048
PAST_LEN = 128

HEAD_DIM = 64
ATTN_WIDTH = 3 * D_MODEL // 8
ATTN_Q_HEADS = ATTN_WIDTH // HEAD_DIM
ATTN_KV_HEADS = 2
ATTN_GROUP = ATTN_Q_HEADS // ATTN_KV_HEADS
KV_WIDTH = ATTN_KV_HEADS * HEAD_DIM
WINDOW = 128
BLOCK = 128
ROPE_THETA = 10000.0
REC_WIDTH = 3 * D_MODEL // 8
REC_BLOCKS = REC_WIDTH // HEAD_DIM
REC_BLOCK_W = REC_WIDTH // REC_BLOCKS
CONV_WIDTH = 4
LRU_C = 8.0
SG_WIDTH = D_MODEL // 4
SG_HEADS = SG_WIDTH // HEAD_DIM
CHUNK = 128
MIX_WIDTH = ATTN_WIDTH + REC_WIDTH + SG_WIDTH
IN_COLS = ATTN_WIDTH + 2 * KV_WIDTH + 2 * REC_WIDTH + 2 * SG_WIDTH
IN_SPLITS = [ATTN_WIDTH,
             ATTN_WIDTH + KV_WIDTH,
             ATTN_WIDTH + 2 * KV_WIDTH,
             ATTN_WIDTH + 2 * KV_WIDTH + REC_WIDTH,
             ATTN_WIDTH + 2 * KV_WIDTH + 2 * REC_WIDTH]
N_EXPERTS = 16
N_GROUPS = 4
EXPERTS_PER_GROUP = N_EXPERTS // N_GROUPS
TOP_K = 2
EXPERT_FF = D_MODEL // 2
ALPHA = (2 * DEPTH) ** 0.25
BETA = (8 * DEPTH) ** -0.25
LN_EPS = 1e-5
RMS_EPS = 1e-6

kernel_name = "hybrid_bidir_headgroup_moe_encoder"


def layer_norm(x):
    xf = x.astype(jnp.float32)
    mu = jnp.mean(xf, axis=-1, keepdims=True)
    var = jnp.mean(jnp.square(xf - mu), axis=-1, keepdims=True)
    return ((xf - mu) * lax.rsqrt(var + LN_EPS)).astype(x.dtype)


def rms_norm(x):
    xf = x.astype(jnp.float32)
    return (xf * lax.rsqrt(jnp.mean(jnp.square(xf), axis=-1, keepdims=True) + RMS_EPS)).astype(x.dtype)


def rope_tables(S, dtype):
    inv = jnp.power(ROPE_THETA, -jnp.arange(0, HEAD_DIM, 2, dtype=jnp.float32) / HEAD_DIM)
    ang = jnp.arange(S, dtype=jnp.float32)[:, None] * inv[None, :]
    return jnp.cos(ang).astype(dtype)[None, :, None, :], jnp.sin(ang).astype(dtype)[None, :, None, :]


def rope(x, cos, sin):
    x1, x2 = jnp.split(x, 2, axis=-1)
    return jnp.concatenate([x1 * cos - x2 * sin, x2 * cos + x1 * sin], axis=-1)


def banded_attention(q, k, v, sink):
    B, S = q.shape[0], q.shape[1]
    nb = S // BLOCK
    pad = ((0, 0), (BLOCK, BLOCK), (0, 0), (0, 0))
    kb = jnp.pad(k, pad).reshape(B, nb + 2, BLOCK, ATTN_KV_HEADS, HEAD_DIM)
    vb = jnp.pad(v, pad).reshape(B, nb + 2, BLOCK, ATTN_KV_HEADS, HEAD_DIM)
    kw = jnp.concatenate([kb[:, :-2], kb[:, 1:-1], kb[:, 2:]], axis=2)
    vw = jnp.concatenate([vb[:, :-2], vb[:, 1:-1], vb[:, 2:]], axis=2)
    qb = q.reshape(B, nb, BLOCK, ATTN_KV_HEADS, ATTN_GROUP, HEAD_DIM)
    s = jnp.einsum('bnqkgd,bnskd->bnkgqs', qb, kw).astype(jnp.float32) * (HEAD_DIM ** -0.5)
    qpos = jnp.arange(nb)[:, None] * BLOCK + jnp.arange(BLOCK)[None, :]
    kpos = (jnp.arange(nb)[:, None] - 1) * BLOCK + jnp.arange(3 * BLOCK)[None, :]
    mask = (jnp.abs(qpos[:, :, None] - kpos[:, None, :]) <= WINDOW) & (kpos[:, None, :] >= 0) & (kpos[:, None, :] < S)
    s = jnp.where(mask[None, :, None, None], s, -jnp.inf)
    sink_l = jnp.broadcast_to(sink.astype(jnp.float32).reshape(ATTN_KV_HEADS, ATTN_GROUP)[None, None, :, :, None, None],
                              s.shape[:-1] + (1,))
    p = jax.nn.softmax(jnp.concatenate([s, sink_l], axis=-1), axis=-1)[..., :-1]
    o = jnp.einsum('bnkgqs,bnskd->bnqkgd', p.astype(v.dtype), vw)
    return o.reshape(B, S, ATTN_Q_HEADS * HEAD_DIM)


def centred_conv(x, w, b):
    S = x.shape[1]
    left = CONV_WIDTH // 2
    xp = jnp.pad(x, ((0, 0), (left, CONV_WIDTH - 1 - left), (0, 0)))
    out = xp[:, 0:S] * w[0]
    for t in range(1, CONV_WIDTH):
        out = out + xp[:, t:t + S] * w[t]
    return out + b


def _lin_combine(c1, c2):
    a1, b1 = c1
    a2, b2 = c2
    return a1 * a2, a2 * b1 + b2


def rglru_bidirectional(xr, w_r, b_r, w_i, b_i, lam):
    B, S, R = xr.shape
    xg = xr.reshape(B, S, REC_BLOCKS, REC_BLOCK_W)

    def block_gate(w, b):
        z = jnp.einsum('bsnc,dncf->dbsnf', xg, w).reshape(2, B, S, R)
        return jax.nn.sigmoid((z + b[:, None, None, :]).astype(jnp.float32))

    r = block_gate(w_r, b_r)
    i = block_gate(w_i, b_i)
    log_a = -LRU_C * jax.nn.softplus(-lam.astype(jnp.float32))[:, None, None, :] * r
    a = jnp.exp(log_a)
    b_in = jnp.sqrt(-jnp.expm1(2.0 * log_a)) * i * xr.astype(jnp.float32)[None]
    h_fwd = lax.associative_scan(_lin_combine, (a[0], b_in[0]), axis=1)[1]
    h_bwd = lax.associative_scan(_lin_combine, (a[1], b_in[1]), reverse=True, axis=1)[1]
    return (h_fwd + h_bwd).astype(xr.dtype)


def spatial_gating(z, norm_g, w_s, b_s):
    B, S, _ = z.shape
    u, v = jnp.split(z, 2, axis=-1)
    v = layer_norm(v) * norm_g
    vb = v.reshape(B, S // CHUNK, CHUNK, SG_HEADS, HEAD_DIM)
    mixed = jnp.einsum('hpq,bnqhd->bnphd', w_s, vb) + b_s.T[None, None, :, :, None]
    return u * mixed.reshape(B, S, SG_WIDTH)


def mixer(h, cos, sin, w_in, attn_sink, conv_w, conv_b, lru_w_r, lru_b_r, lru_w_i, lru_b_i, lru_lambda,
          sg_norm_g, sg_w, sg_b, mix_norm_g, w_out):
    B, S, _ = h.shape
    proj = h @ w_in
    q, k, v, y_gate, x_rec, z_sg = jnp.split(proj, IN_SPLITS, axis=-1)
    q = rope(q.reshape(B, S, ATTN_Q_HEADS, HEAD_DIM), cos, sin)
    k = rope(k.reshape(B, S, ATTN_KV_HEADS, HEAD_DIM), cos, sin)
    v = v.reshape(B, S, ATTN_KV_HEADS, HEAD_DIM)
    o_attn = banded_attention(q, k, v, attn_sink)
    x_rec = centred_conv(x_rec, conv_w, conv_b)
    o_rec = jax.nn.gelu(y_gate) * rglru_bidirectional(x_rec, lru_w_r, lru_b_r, lru_w_i, lru_b_i, lru_lambda)
    o_sg = spatial_gating(jax.nn.gelu(z_sg), sg_norm_g, sg_w, sg_b)
    g_a, g_r, g_s = jnp.split(mix_norm_g, [ATTN_WIDTH, ATTN_WIDTH + REC_WIDTH])
    merged = jnp.concatenate([rms_norm(o_attn) * g_a, rms_norm(o_rec) * g_r, rms_norm(o_sg) * g_s], axis=-1)
    return merged @ w_out


def grouped_moe(h, router_w, router_bias, w1, w3, w2):
    B, S, D = h.shape
    t = h.reshape(B * S, D)
    scores = jax.nn.sigmoid((t @ router_w).astype(jnp.float32))
    sel = scores + router_bias.astype(jnp.float32)
    grp_score = lax.top_k(sel.reshape(-1, N_GROUPS, EXPERTS_PER_GROUP), TOP_K)[0].sum(-1)
    best = jnp.argmax(grp_score, axis=-1)
    in_grp = (jnp.arange(N_EXPERTS) // EXPERTS_PER_GROUP)[None, :] == best[:, None]
    _, idx = lax.top_k(jnp.where(in_grp, sel, -jnp.inf), TOP_K)
    w = jnp.take_along_axis(scores, idx, axis=-1)
    w = w / jnp.sum(w, axis=-1, keepdims=True)
    gates = jnp.sum(jax.nn.one_hot(idx, N_EXPERTS, dtype=jnp.float32) * w[..., None], axis=1)
    out = jnp.zeros(t.shape, jnp.float32)
    for e in range(N_EXPERTS):
        a = jax.nn.silu(t @ w1[e]) * (t @ w3[e])
        out = out + gates[:, e:e + 1] * (a @ w2[e]).astype(jnp.float32)
    return out.astype(h.dtype).reshape(B, S, D)


def adaln(c_act, w, b):
    mod = c_act @ w + b
    shift, scale, gate = jnp.split(mod, 3, axis=-1)
    return shift[:, None, :], scale[:, None, :], gate[:, None, :]


def trunk(x, c, w_mod, b_mod, w_in, attn_sink, conv_w, conv_b, lru_w_r, lru_b_r, lru_w_i, lru_b_i,
          lru_lambda, sg_norm_g, sg_w, sg_b, mix_norm_g, w_out, ln_g, ln_b, router_w, router_bias,
          exp_w1, exp_w3, exp_w2):
    S = x.shape[1]
    cos, sin = rope_tables(S, x.dtype)
    c_act = jax.nn.silu(c)
    for l in range(DEPTH):
        shift, scale, gate = adaln(c_act, w_mod[l, 0], b_mod[l, 0])
        h = layer_norm(x) * (1.0 + scale) + shift
        o = mixer(h, cos, sin, w_in[l], attn_sink[l], conv_w[l], conv_b[l], lru_w_r[l], lru_b_r[l],
                  lru_w_i[l], lru_b_i[l], lru_lambda[l], sg_norm_g[l], sg_w[l], sg_b[l], mix_norm_g[l], w_out[l])
        x = layer_norm(ALPHA * x + gate * o) * ln_g[l, 0] + ln_b[l, 0]
        shift, scale, gate = adaln(c_act, w_mod[l, 1], b_mod[l, 1])
        h = layer_norm(x) * (1.0 + scale) + shift
        o = grouped_moe(h, router_w, router_bias, exp_w1[l], exp_w3[l], exp_w2[l])
        x = layer_norm(ALPHA * x + gate * o) * ln_g[l, 1] + ln_b[l, 1]
    return x


def setup_inputs(seed: int = 0) -> dict:
    key = jax.random.key(seed)
    ks = jax.random.split(key, 32)
    f32 = jnp.float32

    def nrm(k, shape, s):
        return jax.random.normal(k, shape, f32) * s

    u = jax.random.uniform(ks[14], (DEPTH, 2, REC_WIDTH), f32, 0.9, 0.999)
    p = u ** (1.0 / LRU_C)
    return {
        "x_prompt": nrm(ks[0], (BATCH, SEQ, D_MODEL), 1.0),
        "x_sample": nrm(ks[1], (DEC_BATCH, DEC_SEQ, D_MODEL), 1.0),
        "c_prompt": nrm(ks[2], (BATCH, D_MODEL), 1.0),
        "c_sample": nrm(ks[3], (DEC_BATCH, D_MODEL), 1.0),
        "w_mod": nrm(ks[4], (DEPTH, 2, D_MODEL, 3 * D_MODEL), 0.5 * D_MODEL ** -0.5),
        "b_mod": nrm(ks[5], (DEPTH, 2, 3 * D_MODEL), 0.01),
        "w_in": nrm(ks[6], (DEPTH, D_MODEL, IN_COLS), D_MODEL ** -0.5),
        "attn_sink": nrm(ks[7], (DEPTH, ATTN_Q_HEADS), 0.5),
        "conv_w": nrm(ks[8], (DEPTH, CONV_WIDTH, REC_WIDTH), CONV_WIDTH ** -0.5),
        "conv_b": nrm(ks[9], (DEPTH, REC_WIDTH), 0.01),
        "lru_w_r": nrm(ks[10], (DEPTH, 2, REC_BLOCKS, REC_BLOCK_W, REC_BLOCK_W), REC_BLOCK_W ** -0.5),
        "lru_b_r": nrm(ks[11], (DEPTH, 2, REC_WIDTH), 0.01),
        "lru_w_i": nrm(ks[12], (DEPTH, 2, REC_BLOCKS, REC_BLOCK_W, REC_BLOCK_W), REC_BLOCK_W ** -0.5),
        "lru_b_i": nrm(ks[13], (DEPTH, 2, REC_WIDTH), 0.01),
        "lru_lambda": jnp.log(p) - jnp.log1p(-p),
        "sg_norm_g": 1.0 + nrm(ks[15], (DEPTH, SG_WIDTH), 0.02),
        "sg_w": nrm(ks[16], (DEPTH, SG_HEADS, CHUNK, CHUNK), 0.5 * CHUNK ** -0.5),
        "sg_b": 1.0 + nrm(ks[17], (DEPTH, SG_HEADS, CHUNK), 0.02),
        "mix_norm_g": 1.0 + nrm(ks[18], (DEPTH, MIX_WIDTH), 0.02),
        "w_out": nrm(ks[19], (DEPTH, MIX_WIDTH, D_MODEL), BETA * MIX_WIDTH ** -0.5),
        "ln_g": 1.0 + nrm(ks[20], (DEPTH, 2, D_MODEL), 0.02),
        "ln_b": nrm(ks[21], (DEPTH, 2, D_MODEL), 0.01),
        "router_w": nrm(ks[22], (D_MODEL, N_EXPERTS), D_MODEL ** -0.5),
        "router_bias": nrm(ks[23], (N_EXPERTS,), 0.01),
        "exp_w1": nrm(ks[24], (DEPTH, N_EXPERTS, D_MODEL, EXPERT_FF), D_MODEL ** -0.5),
        "exp_w3": nrm(ks[25], (DEPTH, N_EXPERTS, D_MODEL, EXPERT_FF), D_MODEL ** -0.5),
        "exp_w2": nrm(ks[26], (DEPTH, N_EXPERTS, EXPERT_FF, D_MODEL), BETA * EXPERT_FF ** -0.5),
    }


def reference(x_prompt, x_sample, c_prompt, c_sample, w_mod, b_mod, w_in, attn_sink, conv_w, conv_b,
              lru_w_r, lru_b_r, lru_w_i, lru_b_i, lru_lambda, sg_norm_g, sg_w, sg_b, mix_norm_g, w_out,
              ln_g, ln_b, router_w, router_bias, exp_w1, exp_w3, exp_w2):
    y_prompt = trunk(x_prompt, c_prompt, w_mod, b_mod, w_in, attn_sink, conv_w, conv_b, lru_w_r, lru_b_r,
                     lru_w_i, lru_b_i, lru_lambda, sg_norm_g, sg_w, sg_b, mix_norm_g, w_out, ln_g, ln_b,
                     router_w, router_bias, exp_w1, exp_w3, exp_w2)
    y_sample = trunk(x_sample, c_sample, w_mod, b_mod, w_in, attn_sink, conv_w, conv_b, lru_w_r, lru_b_r,
                     lru_w_i, lru_b_i, lru_lambda, sg_norm_g, sg_w, sg_b, mix_norm_g, w_out, ln_g, ln_b,
                     router_w, router_bias, exp_w1, exp_w3, exp_w2)
    return (y_prompt, y_sample)
```

```python
import functools

import numpy as np
import jax
import jax.numpy as jnp
from jax import lax
from jax.experimental import pallas as pl
from jax.experimental.pallas import tpu as pltpu

F32 = jnp.float32
BF16 = jnp.bfloat16

D = 1024
DEPTH = 2
HD = 64
AW = 384
KVW = 128
RW = 384
SGW = 256
WINDOW = 128
CONV_WIDTH = 4
LRU_C = 8.0
CHUNK = 128
NE = 16
NG = 4
EPG = NE // NG
FF = 512
ALPHA = (2 * DEPTH) ** 0.25
LN_EPS = 1e-5
RMS_EPS = 1e-6
ROPE_THETA = 10000.0

TM = 512
QB = 128
SEG = TM // 8
NLG = RW // 128
TME = 1024
NEG = -1e30
VMEM_LIMIT = 48 * 1024 * 1024

C_Q, C_K, C_V, C_Y, C_R, C_U, C_SV, C_END = 0, 384, 640, 896, 1280, 1664, 1920, 2176


def _ln(x):
    mu = jnp.mean(x, axis=-1, keepdims=True)
    xc = x - mu
    var = jnp.mean(xc * xc, axis=-1, keepdims=True)
    return xc * lax.rsqrt(var + LN_EPS)


def _rms(x):
    return x * lax.rsqrt(jnp.mean(x * x, axis=-1, keepdims=True) + RMS_EPS)


def _sigmoid(z):
    return 0.5 * (jnp.tanh(0.5 * z) + 1.0)


def _gelu(x):
    return 0.5 * x * (1.0 + jnp.tanh(0.7978845608028654 * (x + 0.044715 * (x * x * x))))


def _split_bf16(a):
    hi = a.astype(BF16)
    lo = (a - hi.astype(F32)).astype(BF16)
    return hi, lo


def _mod_kernel(c_ref, w_ref, b_ref, o_ref):
    c = c_ref[...]
    ca = c * _sigmoid(c)
    ch, cl = _split_bf16(ca)
    wh, wl = _split_bf16(w_ref[...])
    acc = jnp.dot(ch, wh, preferred_element_type=F32)
    acc += jnp.dot(ch, wl, preferred_element_type=F32)
    acc += jnp.dot(cl, wh, preferred_element_type=F32)
    o_ref[...] = acc + b_ref[...]


def _modulation(c_all, w_mod, b_mod):
    bt = c_all.shape[0]
    cb = 768
    w = w_mod.reshape(2 * DEPTH, D, 3 * D)
    b = b_mod.reshape(2 * DEPTH, 1, 3 * D)
    return pl.pallas_call(
        _mod_kernel,
        grid=(2 * DEPTH, 3 * D // cb),
        in_specs=[pl.BlockSpec((bt, D), lambda s, j: (0, 0)),
                  pl.BlockSpec((None, D, cb), lambda s, j: (s, 0, j)),
                  pl.BlockSpec((None, 1, cb), lambda s, j: (s, 0, j))],
        out_specs=pl.BlockSpec((None, bt, cb), lambda s, j: (s, 0, j)),
        out_shape=jax.ShapeDtypeStruct((2 * DEPTH, bt, 3 * D), F32),
        name="modulation",
    )(c_all, w, b)


def _pre_kernel(bidx_ref, pblk_ref, x_ref, mod_ref, w_ref, cos_ref, sin_ref, sgg_ref,
                q_ref, kk_ref, vv_ref, gy_ref, xr_ref, u_ref, vn_ref):
    del bidx_ref, pblk_ref
    mod = mod_ref[...]
    shift, scale = mod[:, :D], mod[:, D:2 * D]
    h = (_ln(x_ref[...]) * (1.0 + scale) + shift).astype(BF16)
    cos = cos_ref[...]
    sin = sin_ref[...]
    lane = lax.broadcasted_iota(jnp.int32, (1, 128), 1)
    first_half = (lane % HD) < (HD // 2)

    def rope128(xg):
        rot = jnp.where(first_half, pltpu.roll(xg, 128 - HD // 2, 1), pltpu.roll(xg, HD // 2, 1))
        return xg * cos + rot * sin

    q = jnp.dot(h, w_ref[:, C_Q:C_K], preferred_element_type=F32)
    for g in range(AW // 128):
        q_ref[:, g * 128:(g + 1) * 128] = (rope128(q[:, g * 128:(g + 1) * 128]) * (HD ** -0.5)).astype(BF16)
    k = jnp.dot(h, w_ref[:, C_K:C_V], preferred_element_type=F32)
    for g in range(2):
        kk_ref[:, g * 128:(g + 1) * 128] = rope128(k[:, g * 128:(g + 1) * 128]).astype(BF16)
    vv_ref[...] = jnp.dot(h, w_ref[:, C_V:C_Y], preferred_element_type=F32).astype(BF16)
    gy_ref[...] = _gelu(jnp.dot(h, w_ref[:, C_Y:C_R], preferred_element_type=F32)).astype(gy_ref.dtype)
    xr_ref[...] = jnp.dot(h, w_ref[:, C_R:C_U], preferred_element_type=F32)
    u_ref[...] = _gelu(jnp.dot(h, w_ref[:, C_U:C_SV], preferred_element_type=F32)).astype(u_ref.dtype)
    sv = _gelu(jnp.dot(h, w_ref[:, C_SV:C_END], preferred_element_type=F32))
    vn_ref[...] = (_ln(sv) * sgg_ref[...]).astype(BF16)


def _pre_mixer(meta, x, mod_l, w_ext, cos_t, sin_t, sgg):
    n = x.shape[0]
    nt = n // TM
    tok = lambda w: pl.BlockSpec((TM, w), lambda i, b, p: (i, 0))
    full = lambda a: pl.BlockSpec(a.shape, lambda i, b, p: (0,) * a.ndim)
    grid_spec = pltpu.PrefetchScalarGridSpec(
        num_scalar_prefetch=2, grid=(nt,),
        in_specs=[tok(D),
                  pl.BlockSpec((None, 1, 3 * D), lambda i, b, p: (b[i], 0, 0)),
                  full(w_ext),
                  pl.BlockSpec((TM, 128), lambda i, b, p: (p[i], 0)),
                  pl.BlockSpec((TM, 128), lambda i, b, p: (p[i], 0)),
                  full(sgg)],
        out_specs=[tok(AW), tok(256), tok(256), tok(RW), tok(RW), tok(SGW), tok(SGW)])
    sds = lambda w, dt: jax.ShapeDtypeStruct((n, w), dt)
    return pl.pallas_call(
        _pre_kernel, grid_spec=grid_spec,
        out_shape=[sds(AW, BF16), sds(256, BF16), sds(256, BF16), sds(RW, BF16), sds(RW, F32),
                   sds(SGW, BF16), sds(SGW, BF16)],
        compiler_params=pltpu.CompilerParams(dimension_semantics=("parallel",), vmem_limit_bytes=VMEM_LIMIT),
        name="pre_mixer",
    )(meta["bidx"], meta["pblk"], x, mod_l, w_ext, cos_t, sin_t, sgg)


def _attn_kernel(first_ref, last_ref, sink_ref, q_ref, k_ref, kp_ref, kn_ref, v_ref, vp_ref, vn_ref,
                 o_ref, kw_s, vw_s):
    i = pl.program_id(0)
    is_first = first_ref[i] == 1
    is_last = last_ref[i] == 1
    kw_s[0:QB] = kp_ref[...]
    kw_s[QB:QB + TM] = k_ref[...]
    kw_s[QB + TM:] = kn_ref[...]
    vw_s[0:QB] = vp_ref[...]
    vw_s[QB:QB + TM] = v_ref[...]
    vw_s[QB + TM:] = vn_ref[...]
    iq = lax.broadcasted_iota(jnp.int32, (QB, 3 * QB), 0)
    ik = lax.broadcasted_iota(jnp.int32, (QB, 3 * QB), 1)
    rel = ik - iq
    band = (rel >= 0) & (rel <= 2 * WINDOW)
    lane = lax.broadcasted_iota(jnp.int32, (1, 128), 1)
    lo_half = lane < HD
    zero = jnp.zeros((), BF16)
    for j in range(TM // QB):
        ok = band
        if j == 0:
            ok = ok & ((ik >= QB) | jnp.logical_not(is_first))
        if j == TM // QB - 1:
            ok = ok & ((ik < 2 * QB) | jnp.logical_not(is_last))
        bias = jnp.where(ok, 0.0, NEG)
        rows = slice(j * QB, (j + 1) * QB)
        keys = slice(j * QB, j * QB + 3 * QB)
        kfull = [kw_s[keys, g * 128:(g + 1) * 128] for g in range(2)]
        vfull = [vw_s[keys, g * 128:(g + 1) * 128] for g in range(2)]
        for jg in range(AW // 128):
            qg = q_ref[rows, jg * 128:(jg + 1) * 128]
            acc = None
            for half in range(2):
                hh = 2 * jg + half
                g = hh // 3
                sel = lo_half if half == 0 else jnp.logical_not(lo_half)
                kh = jnp.where(sel, kfull[g], zero)
                vh = jnp.where(sel, vfull[g], zero)
                s = lax.dot_general(qg, kh, (((1,), (1,)), ((), ())), preferred_element_type=F32) + bias
                sink = sink_ref[hh]
                m = jnp.maximum(jnp.max(s, axis=-1, keepdims=True), sink)
                p = jnp.exp(s - m)
                denom = jnp.sum(p, axis=-1, keepdims=True) + jnp.exp(sink - m)
                pv = jnp.dot(p.astype(BF16), vh, preferred_element_type=F32)
                contrib = pv * (1.0 / denom)
                acc = contrib if acc is None else acc + contrib
            o_ref[rows, jg * 128:(jg + 1) * 128] = acc.astype(o_ref.dtype)


def _attention(meta, sink, q, kk, vv):
    n = q.shape[0]
    nt = n // TM
    nqb = n // QB
    r = TM // QB
    main = lambda w: pl.BlockSpec((TM, w), lambda i, f, l, s: (i, 0))
    prev = pl.BlockSpec((QB, 256), lambda i, f, l, s: (jnp.maximum(i * r - 1, 0), 0))
    nxt = pl.BlockSpec((QB, 256), lambda i, f, l, s: (jnp.minimum(i * r + r, nqb - 1), 0))
    grid_spec = pltpu.PrefetchScalarGridSpec(
        num_scalar_prefetch=3, grid=(nt,),
        in_specs=[main(AW), main(256), prev, nxt, main(256), prev, nxt],
        out_specs=main(AW),
        scratch_shapes=[pltpu.VMEM((TM + 2 * QB, 256), BF16), pltpu.VMEM((TM + 2 * QB, 256), BF16)])
    return pl.pallas_call(
        _attn_kernel, grid_spec=grid_spec,
        out_shape=jax.ShapeDtypeStruct((n, AW), BF16),
        compiler_params=pltpu.CompilerParams(dimension_semantics=("parallel",), vmem_limit_bytes=VMEM_LIMIT),
        name="attention",
    )(meta["first"], meta["last"], sink, q, kk, kk, kk, vv, vv, vv)


def _rec_kernel(cf_ref, cb_ref, first_ref, last_ref,
                xf_ref, xfp_ref, xfn_ref, xb_ref, xbp_ref, xbn_ref,
                cw_ref, cbias_ref, wr_ref, wi_ref, br_ref, bi_ref, lam_ref,
                hf_ref, hb_ref,
                ext_s, a_s, b_s, hs_s, ps_s, carry_s):
    i = pl.program_id(0)
    cw = cw_ref[...]
    cbias = cbias_ref[...]

    def run(d, c, x_ref, xp_ref, xn_ref, out_ref):
        first = first_ref[c] == 1
        last = last_ref[c] == 1
        ext_s[0:8] = jnp.where(first, 0.0, xp_ref[...])
        ext_s[8:8 + TM] = x_ref[...]
        ext_s[8 + TM:] = jnp.where(last, 0.0, xn_ref[...])
        xc = ext_s[6:6 + TM] * cw[0:1]
        for t in range(1, CONV_WIDTH):
            xc = xc + ext_s[6 + t:6 + t + TM] * cw[t:t + 1]
        xc = xc + cbias
        xb16 = xc.astype(BF16)
        zr = jnp.dot(xb16, wr_ref[d], preferred_element_type=F32) + br_ref[d:d + 1]
        zi = jnp.dot(xb16, wi_ref[d], preferred_element_type=F32) + bi_ref[d:d + 1]
        nlam = -lam_ref[d:d + 1]
        softplus = jnp.maximum(nlam, 0.0) + jnp.log1p(jnp.exp(-jnp.abs(nlam)))
        log_a = (-LRU_C * softplus) * _sigmoid(zr)
        a = jnp.exp(log_a)
        b = jnp.sqrt((1.0 + a * a) * jnp.tanh(-log_a)) * _sigmoid(zi) * xc
        for g in range(NLG):
            a_s[g] = a[:, g * 128:(g + 1) * 128]
            b_s[g] = b[:, g * 128:(g + 1) * 128]

        reverse = d == 1
        reset = last if reverse else first

        @pl.when(reset)
        def _():
            carry_s[d] = jnp.zeros((NLG, 128), F32)

        def step(jj, hp):
            j = SEG - 1 - jj if reverse else jj
            out = []
            for g in range(NLG):
                h, p = hp[g]
                ag = a_s[g, pl.ds(j, 8, stride=SEG), :]
                bg = b_s[g, pl.ds(j, 8, stride=SEG), :]
                h = ag * h + bg
                p = p * ag
                hs_s[g, j] = h
                ps_s[g, j] = p
                out.append((h, p))
            return tuple(out)

        init = tuple((jnp.zeros((8, 128), F32), jnp.ones((8, 128), F32)) for _ in range(NLG))
        ends = lax.fori_loop(0, SEG, step, init)
        cmats = []
        for g in range(NLG):
            e, pe = ends[g]
            c_in = carry_s[d, g:g + 1]
            rows = [None] * 8
            for s in (range(7, -1, -1) if reverse else range(8)):
                rows[s] = c_in
                c_in = e[s:s + 1] + pe[s:s + 1] * c_in
            carry_s[d, g:g + 1] = c_in
            cmats.append(jnp.concatenate(rows, axis=0))

        def fix(j, _):
            for g in range(NLG):
                out_ref[g, pl.ds(j, 8, stride=SEG), :] = hs_s[g, j] + ps_s[g, j] * cmats[g]
            return 0

        lax.fori_loop(0, SEG, fix, 0)

    run(0, cf_ref[i], xf_ref, xfp_ref, xfn_ref, hf_ref)
    run(1, cb_ref[i], xb_ref, xbp_ref, xbn_ref, hb_ref)


def _recurrent(meta, xr, cw, cbias, wr, wi, br, bi, lam):
    n = xr.shape[0]
    nt = n // TM
    n8 = n // 8
    r8 = TM // 8

    def specs(which):
        sel = (lambda cf, cb: cf) if which == 0 else (lambda cf, cb: cb)
        main = pl.BlockSpec((TM, RW), lambda i, cf, cb, f, l: (sel(cf, cb)[i], 0))
        prev = pl.BlockSpec((8, RW), lambda i, cf, cb, f, l: (jnp.maximum(sel(cf, cb)[i] * r8 - 1, 0), 0))
        nxt = pl.BlockSpec((8, RW), lambda i, cf, cb, f, l: (jnp.minimum(sel(cf, cb)[i] * r8 + r8, n8 - 1), 0))
        return main, prev, nxt

    full = lambda a: pl.BlockSpec(a.shape, lambda i, cf, cb, f, l: (0,) * a.ndim)
    mf, pf, nf = specs(0)
    mb, pb, nb = specs(1)
    grid_spec = pltpu.PrefetchScalarGridSpec(
        num_scalar_prefetch=4, grid=(nt,),
        in_specs=[mf, pf, nf, mb, pb, nb, full(cw), full(cbias), full(wr), full(wi), full(br), full(bi), full(lam)],
        out_specs=[pl.BlockSpec((NLG, TM, 128), lambda i, cf, cb, f, l: (0, cf[i], 0)),
                   pl.BlockSpec((NLG, TM, 128), lambda i, cf, cb, f, l: (0, cb[i], 0))],
        scratch_shapes=[pltpu.VMEM((TM + 16, RW), F32),
                        pltpu.VMEM((NLG, TM, 128), F32), pltpu.VMEM((NLG, TM, 128), F32),
                        pltpu.VMEM((NLG, SEG, 8, 128), F32), pltpu.VMEM((NLG, SEG, 8, 128), F32),
                        pltpu.VMEM((2, NLG, 128), F32)])
    return pl.pallas_call(
        _rec_kernel, grid_spec=grid_spec,
        out_shape=[jax.ShapeDtypeStruct((NLG, n, 128), F32), jax.ShapeDtypeStruct((NLG, n, 128), F32)],
        compiler_params=pltpu.CompilerParams(dimension_semantics=("arbitrary",), vmem_limit_bytes=VMEM_LIMIT),
        name="recurrent",
    )(meta["cf"], meta["cb"], meta["first"], meta["last"], xr, xr, xr, xr, xr, xr, cw, cbias, wr, wi, br, bi, lam)


def _route(sel, score):
    one = jnp.ones_like(sel[0])
    zero = jnp.zeros_like(sel[0])

    def before(vk, vj, k, j):
        return (vk > vj) | ((vk == vj) & (k < j)) if k < j else (vk > vj)

    in_top = []
    gscore = []
    for g in range(NG):
        ids = range(g * EPG, (g + 1) * EPG)
        gs = zero
        for j in ids:
            rank = zero
            for k in ids:
                if k != j:
                    rank = rank + jnp.where(before(sel[k], sel[j], k, j), one, zero)
            m = jnp.where(rank < 2.0, one, zero)
            in_top.append(m)
            gs = gs + m * sel[j]
        gscore.append(gs)
    mask = []
    for g in range(NG):
        worse = zero
        for k in range(NG):
            if k != g:
                worse = worse + jnp.where(before(gscore[k], gscore[g], k, g), one, zero)
        best = jnp.where(worse < 1.0, one, zero)
        for j in range(g * EPG, (g + 1) * EPG):
            mask.append(in_top[j] * best)
    total = zero
    for e in range(NE):
        total = total + mask[e] * score[e]
    inv = 1.0 / total
    gate = [mask[e] * score[e] * inv for e in range(NE)]
    return mask, gate


def _merge_kernel(bidx_ref, x_ref, oa_ref, gy_ref, hf_ref, hb_ref, u_ref, vn_ref,
                  mod1_ref, mod2_ref, gmix_ref, sgw_ref, sgb_ref, wout_ref, lng_ref, lnb_ref, rwt_ref, rb_ref,
                  x1_ref, h2_ref, gate_ref, mask_ref, mrg_s):
    del bidx_ref
    gmix = gmix_ref[...]
    mrg_s[:, 0:AW] = (_rms(oa_ref[...].astype(F32)) * gmix[:, 0:AW]).astype(BF16)
    hsum = jnp.concatenate([hf_ref[g] + hb_ref[g] for g in range(NLG)], axis=1)
    o_rec = gy_ref[...].astype(F32) * hsum
    mrg_s[:, AW:AW + RW] = (_rms(o_rec) * gmix[:, AW:AW + RW]).astype(BF16)
    lane = lax.broadcasted_iota(jnp.int32, (1, 128), 1)
    lo_half = lane < HD
    zero = jnp.zeros((), BF16)
    pieces = []
    for c in range(TM // CHUNK):
        rows = slice(c * CHUNK, (c + 1) * CHUNK)
        grp = []
        for g in range(SGW // 128):
            vg = vn_ref[rows, g * 128:(g + 1) * 128]
            mixed = jnp.dot(sgw_ref[2 * g], jnp.where(lo_half, vg, zero), preferred_element_type=F32)
            mixed += jnp.dot(sgw_ref[2 * g + 1], jnp.where(lo_half, zero, vg), preferred_element_type=F32)
            grp.append(mixed)
        mixed = jnp.concatenate(grp, axis=1) + sgb_ref[...]
        pieces.append(u_ref[rows, :].astype(F32) * mixed)
    o_sg = jnp.concatenate(pieces, axis=0)
    mrg_s[:, AW + RW:] = (_rms(o_sg) * gmix[:, AW + RW:]).astype(BF16)
    o = jnp.dot(mrg_s[...], wout_ref[...], preferred_element_type=F32)
    gate1 = mod1_ref[...][:, 2 * D:]
    x1 = _ln(ALPHA * x_ref[...] + gate1 * o) * lng_ref[...] + lnb_ref[...]
    x1_ref[...] = x1
    mod2 = mod2_ref[...]
    h2 = (_ln(x1) * (1.0 + mod2[:, D:2 * D]) + mod2[:, :D]).astype(BF16)
    h2_ref[...] = h2
    logits = lax.dot_general(rwt_ref[...], h2, (((1,), (1,)), ((), ())), preferred_element_type=F32)
    score = _sigmoid(logits)
    sel = score + rb_ref[...]
    mask, gate = _route([sel[e:e + 1] for e in range(NE)], [score[e:e + 1] for e in range(NE)])
    gate_ref[...] = jnp.concatenate(gate, axis=0)
    mask_ref[...] = jnp.concatenate(mask, axis=0)


def _merge(meta, x, oa, gy, hf, hb, u, vn, mod1, mod2, gmix, sgw, sgb, wout, lng, lnb, rwt, rb):
    n = x.shape[0]
    nt = n // TM
    tok = lambda w: pl.BlockSpec((TM, w), lambda i, b: (i, 0))
    full = lambda a: pl.BlockSpec(a.shape, lambda i, b: (0,) * a.ndim)
    modspec = pl.BlockSpec((None, 1, 3 * D), lambda i, b: (b[i], 0, 0))
    tspec = pl.BlockSpec((NE, TM), lambda i, b: (0, i))
    rec = pl.BlockSpec((NLG, TM, 128), lambda i, b: (0, i, 0))
    grid_spec = pltpu.PrefetchScalarGridSpec(
        num_scalar_prefetch=1, grid=(nt,),
        in_specs=[tok(D), tok(AW), tok(RW), rec, rec, tok(SGW), tok(SGW), modspec, modspec,
                  full(gmix), full(sgw), full(sgb), full(wout), full(lng), full(lnb), full(rwt), full(rb)],
        out_specs=[tok(D), tok(D), tspec, tspec],
        scratch_shapes=[pltpu.VMEM((TM, D), BF16)])
    return pl.pallas_call(
        _merge_kernel, grid_spec=grid_spec,
        out_shape=[jax.ShapeDtypeStruct((n, D), F32), jax.ShapeDtypeStruct((n, D), BF16),
                   jax.ShapeDtypeStruct((NE, n), F32), jax.ShapeDtypeStruct((NE, n), F32)],
        compiler_params=pltpu.CompilerParams(dimension_semantics=("parallel",), vmem_limit_bytes=VMEM_LIMIT),
        name="merge_route",
    )(meta["bidx"], x, oa, gy, hf, hb, u, vn, mod1, mod2, gmix, sgw, sgb, wout, lng, lnb, rwt, rb)


def _moe_kernel(bidx_ref, h_ref, g_ref, w1_ref, w3_ref, w2_ref, x1_ref, mod_ref, lng_ref, lnb_ref, o_ref, acc_s):
    del bidx_ref
    e = pl.program_id(1)

    @pl.when(e == 0)
    def _():
        acc_s[...] = jnp.zeros_like(acc_s)

    h = h_ref[...]
    a = jnp.dot(h, w1_ref[...], preferred_element_type=F32)
    a = a * _sigmoid(a) * jnp.dot(h, w3_ref[...], preferred_element_type=F32)
    y = jnp.dot(a.astype(BF16), w2_ref[...], preferred_element_type=F32)
    lane = lax.broadcasted_iota(jnp.int32, (1, NE), 1)
    ge = jnp.sum(jnp.where(lane == e, g_ref[...], 0.0), axis=1, keepdims=True)
    acc_s[...] += ge * y

    @pl.when(e == NE - 1)
    def _():
        gate = mod_ref[...][:, 2 * D:]
        o_ref[...] = _ln(ALPHA * x1_ref[...] + gate * acc_s[...]) * lng_ref[...] + lnb_ref[...]


def _moe(bidx_e, h2, gates, w1, w3, w2, x1, mod2, lng, lnb):
    n = h2.shape[0]
    nt = n // TME
    tok = lambda w: pl.BlockSpec((TME, w), lambda i, e, b: (i, 0))
    full = lambda a: pl.BlockSpec(a.shape, lambda i, e, b: (0,) * a.ndim)
    grid_spec = pltpu.PrefetchScalarGridSpec(
        num_scalar_prefetch=1, grid=(nt, NE),
        in_specs=[tok(D), tok(NE),
                  pl.BlockSpec((None, D, FF), lambda i, e, b: (e, 0, 0)),
                  pl.BlockSpec((None, D, FF), lambda i, e, b: (e, 0, 0)),
                  pl.BlockSpec((None, FF, D), lambda i, e, b: (e, 0, 0)),
                  tok(D),
                  pl.BlockSpec((None, 1, 3 * D), lambda i, e, b: (b[i], 0, 0)),
                  full(lng), full(lnb)],
        out_specs=tok(D),
        scratch_shapes=[pltpu.VMEM((TME, D), F32)])
    return pl.pallas_call(
        _moe_kernel, grid_spec=grid_spec,
        out_shape=jax.ShapeDtypeStruct((n, D), F32),
        compiler_params=pltpu.CompilerParams(dimension_semantics=("parallel", "arbitrary"),
                                             vmem_limit_bytes=VMEM_LIMIT),
        name="experts",
    )(bidx_e, h2, gates, w1, w3, w2, x1, mod2, lng, lnb)


def _tile_meta(groups):
    bidx, pblk, first, last, cf, cb, bidx_e = [], [], [], [], [], [], []
    row = 0
    tile = 0
    for (b, s) in groups:
        per = s // TM
        for bi in range(b):
            for j in range(per):
                bidx.append(row + bi)
                pblk.append(j)
                first.append(1 if j == 0 else 0)
                last.append(1 if j == per - 1 else 0)
                cf.append(tile + j)
                cb.append(tile + per - 1 - j)
            tile += per
            bidx_e += [row + bi] * (s // TME)
        row += b
    as_i32 = lambda v: jnp.asarray(np.asarray(v, np.int32))
    return dict(bidx=as_i32(bidx), pblk=as_i32(pblk), first=as_i32(first), last=as_i32(last),
                cf=as_i32(cf), cb=as_i32(cb), bidx_e=as_i32(bidx_e))


def _rope_tables(s_max):
    inv = jnp.power(ROPE_THETA, -jnp.arange(0, HD, 2, dtype=F32) / HD)
    ang = jnp.arange(s_max, dtype=F32)[:, None] * inv[None, :]
    cos, sin = jnp.cos(ang), jnp.sin(ang)
    cos128 = jnp.tile(cos, (1, 4))
    sin128 = jnp.tile(jnp.concatenate([-sin, sin], axis=1), (1, 2))
    return cos128, sin128


def _block_diag(w):
    d, nb, c, f = w.shape
    eye = jnp.eye(nb, dtype=w.dtype)
    return jnp.einsum('dncf,nm->dncmf', w, eye).reshape(d, nb * c, nb * f)


def _forward(xs, cs, w_mod, b_mod, w_in, attn_sink, conv_w, conv_b, lru_w_r, lru_b_r, lru_w_i, lru_b_i,
             lru_lambda, sg_norm_g, sg_w, sg_b, mix_norm_g, w_out, ln_g, ln_b, router_w, router_bias,
             exp_w1, exp_w3, exp_w2):
    groups = [(x.shape[0], x.shape[1]) for x in xs]
    for (_, s) in groups:
        assert s % TME == 0 and s % TM == 0
    meta = _tile_meta(groups)
    x = jnp.concatenate([xx.reshape(-1, D) for xx in xs], axis=0)
    c_all = jnp.concatenate(cs, axis=0)
    bt = c_all.shape[0]
    mods = _modulation(c_all, w_mod, b_mod).reshape(2 * DEPTH, bt, 1, 3 * D)
    cos_t, sin_t = _rope_tables(max(s for _, s in groups))
    rwt = router_w.T.astype(BF16)
    rb = router_bias.reshape(NE, 1)
    for l in range(DEPTH):
        wi = w_in[l]
        kcols = wi[:, AW:AW + KVW]
        vcols = wi[:, AW + KVW:AW + 2 * KVW]
        dup = lambda m: jnp.concatenate([m[:, :HD], m[:, :HD], m[:, HD:], m[:, HD:]], axis=1)
        w_ext = jnp.concatenate([wi[:, :AW], dup(kcols), dup(vcols), wi[:, AW + 2 * KVW:]], axis=1).astype(BF16)
        q, kk, vv, gy, xr, u, vn = _pre_mixer(meta, x, mods[2 * l], w_ext, cos_t, sin_t,
                                              sg_norm_g[l].reshape(1, SGW))
        oa = _attention(meta, attn_sink[l], q, kk, vv)
        hf, hb = _recurrent(meta, xr, conv_w[l], conv_b[l].reshape(1, RW),
                            _block_diag(lru_w_r[l]).astype(BF16), _block_diag(lru_w_i[l]).astype(BF16),
                            lru_b_r[l], lru_b_i[l], lru_lambda[l])
        sgb = jnp.repeat(sg_b[l].T, HD, axis=1)
        x1, h2, gate_t, _ = _merge(meta, x, oa, gy, hf, hb, u, vn, mods[2 * l], mods[2 * l + 1],
                                   mix_norm_g[l].reshape(1, D), sg_w[l].astype(BF16), sgb,
                                   w_out[l].astype(BF16), ln_g[l, 0].reshape(1, D), ln_b[l, 0].reshape(1, D),
                                   rwt, rb)
        x = _moe(meta["bidx_e"], h2, gate_t.T, exp_w1[l].astype(BF16), exp_w3[l].astype(BF16),
                 exp_w2[l].astype(BF16), x1, mods[2 * l + 1], ln_g[l, 1].reshape(1, D), ln_b[l, 1].reshape(1, D))
    outs = []
    off = 0
    for (b, s) in groups:
        outs.append(x[off:off + b * s].reshape(b, s, D))
        off += b * s
    return tuple(outs)


def kernel(x_prompt, x_sample, c_prompt, c_sample, w_mod, b_mod, w_in, attn_sink, conv_w, conv_b, lru_w_r, lru_b_r,
           lru_w_i, lru_b_i, lru_lambda, sg_norm_g, sg_w, sg_b, mix_norm_g, w_out, ln_g, ln_b, router_w,
           router_bias, exp_w1, exp_w3, exp_w2):
    return _forward([x_prompt, x_sample], [c_prompt, c_sample], w_mod, b_mod, w_in, attn_sink, conv_w, conv_b,
                    lru_w_r, lru_b_r, lru_w_i, lru_b_i, lru_lambda, sg_norm_g, sg_w, sg_b, mix_norm_g, w_out,
                    ln_g, ln_b, router_w, router_bias, exp_w1, exp_w3, exp_w2)
```

```python
import functools

import numpy as np
import jax
import jax.numpy as jnp
from jax import lax
from jax.experimental import pallas as pl
from jax.experimental.pallas import tpu as pltpu

F32 = jnp.float32
BF16 = jnp.bfloat16

D = 1024
DEPTH = 2
HD = 64
AW = 384
KVW = 128
RW = 384
SGW = 256
WINDOW = 128
CONV_WIDTH = 4
LRU_C = 8.0
CHUNK = 128
NE = 16
NG = 4
EPG = NE // NG
FF = 512
ALPHA = (2 * DEPTH) ** 0.25
LN_EPS = 1e-5
RMS_EPS = 1e-6
ROPE_THETA = 10000.0

TM = 512
QB = 128
SEG = TM // 8
NLG = RW // 128
TD = 256
ALIGN = 16
RL = 2 * TD + NE * ALIGN
TMX = 512
ZROWS = 256
NEG = -1e30
VMEM_LIMIT = 48 * 1024 * 1024

C_Q, C_K, C_V, C_Y, C_R, C_U, C_SV, C_END = 0, 384, 640, 896, 1280, 1664, 1920, 2176


def _ln(x):
    mu = jnp.mean(x, axis=-1, keepdims=True)
    xc = x - mu
    var = jnp.mean(xc * xc, axis=-1, keepdims=True)
    return xc * lax.rsqrt(var + LN_EPS)


def _rms(x):
    return x * lax.rsqrt(jnp.mean(x * x, axis=-1, keepdims=True) + RMS_EPS)


def _sigmoid(z):
    return 0.5 * (jnp.tanh(0.5 * z) + 1.0)


def _gelu(x):
    return 0.5 * x * (1.0 + jnp.tanh(0.7978845608028654 * (x + 0.044715 * (x * x * x))))


def _split_bf16(a):
    hi = a.astype(BF16)
    lo = (a - hi.astype(F32)).astype(BF16)
    return hi, lo


def _mod_kernel(c_ref, w_ref, b_ref, o_ref):
    c = c_ref[...]
    ca = c * _sigmoid(c)
    ch, cl = _split_bf16(ca)
    wh, wl = _split_bf16(w_ref[...])
    acc = jnp.dot(ch, wh, preferred_element_type=F32)
    acc += jnp.dot(ch, wl, preferred_element_type=F32)
    acc += jnp.dot(cl, wh, preferred_element_type=F32)
    o_ref[...] = acc + b_ref[...]


def _modulation(c_all, w_mod, b_mod):
    bt = c_all.shape[0]
    cb = 768
    w = w_mod.reshape(2 * DEPTH, D, 3 * D)
    b = b_mod.reshape(2 * DEPTH, 1, 3 * D)
    return pl.pallas_call(
        _mod_kernel,
        grid=(2 * DEPTH, 3 * D // cb),
        in_specs=[pl.BlockSpec((bt, D), lambda s, j: (0, 0)),
                  pl.BlockSpec((None, D, cb), lambda s, j: (s, 0, j)),
                  pl.BlockSpec((None, 1, cb), lambda s, j: (s, 0, j))],
        out_specs=pl.BlockSpec((None, bt, cb), lambda s, j: (s, 0, j)),
        out_shape=jax.ShapeDtypeStruct((2 * DEPTH, bt, 3 * D), F32),
        name="modulation",
    )(c_all, w, b)


def _pre_kernel(bidx_ref, pblk_ref, x_ref, mod_ref, w_ref, cos_ref, sin_ref, sgg_ref,
                q_ref, kk_ref, vv_ref, gy_ref, xr_ref, u_ref, vn_ref):
    del bidx_ref, pblk_ref
    mod = mod_ref[...]
    shift, scale = mod[:, :D], mod[:, D:2 * D]
    h = (_ln(x_ref[...]) * (1.0 + scale) + shift).astype(BF16)
    cos = cos_ref[...]
    sin = sin_ref[...]
    lane = lax.broadcasted_iota(jnp.int32, (1, 128), 1)
    first_half = (lane % HD) < (HD // 2)

    def rope128(xg):
        rot = jnp.where(first_half, pltpu.roll(xg, 128 - HD // 2, 1), pltpu.roll(xg, HD // 2, 1))
        return xg * cos + rot * sin

    q = jnp.dot(h, w_ref[:, C_Q:C_K], preferred_element_type=F32)
    for g in range(AW // 128):
        q_ref[:, g * 128:(g + 1) * 128] = (rope128(q[:, g * 128:(g + 1) * 128]) * (HD ** -0.5)).astype(BF16)
    k = jnp.dot(h, w_ref[:, C_K:C_V], preferred_element_type=F32)
    for g in range(2):
        kk_ref[:, g * 128:(g + 1) * 128] = rope128(k[:, g * 128:(g + 1) * 128]).astype(BF16)
    vv_ref[...] = jnp.dot(h, w_ref[:, C_V:C_Y], preferred_element_type=F32).astype(BF16)
    gy_ref[...] = _gelu(jnp.dot(h, w_ref[:, C_Y:C_R], preferred_element_type=F32)).astype(gy_ref.dtype)
    xr_ref[...] = jnp.dot(h, w_ref[:, C_R:C_U], preferred_element_type=F32)
    u_ref[...] = _gelu(jnp.dot(h, w_ref[:, C_U:C_SV], preferred_element_type=F32)).astype(u_ref.dtype)
    sv = _gelu(jnp.dot(h, w_ref[:, C_SV:C_END], preferred_element_type=F32))
    vn_ref[...] = (_ln(sv) * sgg_ref[...]).astype(BF16)


def _pre_mixer(meta, x, mod_l, w_ext, cos_t, sin_t, sgg):
    n = x.shape[0]
    nt = n // TM
    tok = lambda w: pl.BlockSpec((TM, w), lambda i, b, p: (i, 0))
    full = lambda a: pl.BlockSpec(a.shape, lambda i, b, p: (0,) * a.ndim)
    grid_spec = pltpu.PrefetchScalarGridSpec(
        num_scalar_prefetch=2, grid=(nt,),
        in_specs=[tok(D),
                  pl.BlockSpec((None, 1, 3 * D), lambda i, b, p: (b[i], 0, 0)),
                  full(w_ext),
                  pl.BlockSpec((TM, 128), lambda i, b, p: (p[i], 0)),
                  pl.BlockSpec((TM, 128), lambda i, b, p: (p[i], 0)),
                  full(sgg)],
        out_specs=[tok(AW), tok(256), tok(256), tok(RW), tok(RW), tok(SGW), tok(SGW)])
    sds = lambda w, dt: jax.ShapeDtypeStruct((n, w), dt)
    return pl.pallas_call(
        _pre_kernel, grid_spec=grid_spec,
        out_shape=[sds(AW, BF16), sds(256, BF16), sds(256, BF16), sds(RW, BF16), sds(RW, F32),
                   sds(SGW, BF16), sds(SGW, BF16)],
        compiler_params=pltpu.CompilerParams(dimension_semantics=("parallel",), vmem_limit_bytes=VMEM_LIMIT),
        name="pre_mixer",
    )(meta["bidx"], meta["pblk"], x, mod_l, w_ext, cos_t, sin_t, sgg)


def _attn_kernel(first_ref, last_ref, sink_ref, q_ref, k_ref, kp_ref, kn_ref, v_ref, vp_ref, vn_ref,
                 o_ref, kw_s, vw_s):
    i = pl.program_id(0)
    is_first = first_ref[i] == 1
    is_last = last_ref[i] == 1
    kw_s[0:QB] = kp_ref[...]
    kw_s[QB:QB + TM] = k_ref[...]
    kw_s[QB + TM:] = kn_ref[...]
    vw_s[0:QB] = vp_ref[...]
    vw_s[QB:QB + TM] = v_ref[...]
    vw_s[QB + TM:] = vn_ref[...]
    iq = lax.broadcasted_iota(jnp.int32, (QB, 3 * QB), 0)
    ik = lax.broadcasted_iota(jnp.int32, (QB, 3 * QB), 1)
    rel = ik - iq
    band = (rel >= 0) & (rel <= 2 * WINDOW)
    lane = lax.broadcasted_iota(jnp.int32, (1, 128), 1)
    lo_half = lane < HD
    zero = jnp.zeros((), BF16)
    for j in range(TM // QB):
        ok = band
        if j == 0:
            ok = ok & ((ik >= QB) | jnp.logical_not(is_first))
        if j == TM // QB - 1:
            ok = ok & ((ik < 2 * QB) | jnp.logical_not(is_last))
        bias = jnp.where(ok, 0.0, NEG)
        rows = slice(j * QB, (j + 1) * QB)
        keys = slice(j * QB, j * QB + 3 * QB)
        kfull = [kw_s[keys, g * 128:(g + 1) * 128] for g in range(2)]
        vfull = [vw_s[keys, g * 128:(g + 1) * 128] for g in range(2)]
        for jg in range(AW // 128):
            qg = q_ref[rows, jg * 128:(jg + 1) * 128]
            acc = None
            for half in range(2):
                hh = 2 * jg + half
                g = hh // 3
                sel = lo_half if half == 0 else jnp.logical_not(lo_half)
                kh = jnp.where(sel, kfull[g], zero)
                vh = jnp.where(sel, vfull[g], zero)
                s = lax.dot_general(qg, kh, (((1,), (1,)), ((), ())), preferred_element_type=F32) + bias
                sink = sink_ref[hh]
                m = jnp.maximum(jnp.max(s, axis=-1, keepdims=True), sink)
                p = jnp.exp(s - m)
                denom = jnp.sum(p, axis=-1, keepdims=True) + jnp.exp(sink - m)
                pv = jnp.dot(p.astype(BF16), vh, preferred_element_type=F32)
                contrib = pv * (1.0 / denom)
                acc = contrib if acc is None else acc + contrib
            o_ref[rows, jg * 128:(jg + 1) * 128] = acc.astype(o_ref.dtype)


def _attention(meta, sink, q, kk, vv):
    n = q.shape[0]
    nt = n // TM
    nqb = n // QB
    r = TM // QB
    main = lambda w: pl.BlockSpec((TM, w), lambda i, f, l, s: (i, 0))
    prev = pl.BlockSpec((QB, 256), lambda i, f, l, s: (jnp.maximum(i * r - 1, 0), 0))
    nxt = pl.BlockSpec((QB, 256), lambda i, f, l, s: (jnp.minimum(i * r + r, nqb - 1), 0))
    grid_spec = pltpu.PrefetchScalarGridSpec(
        num_scalar_prefetch=3, grid=(nt,),
        in_specs=[main(AW), main(256), prev, nxt, main(256), prev, nxt],
        out_specs=main(AW),
        scratch_shapes=[pltpu.VMEM((TM + 2 * QB, 256), BF16), pltpu.VMEM((TM + 2 * QB, 256), BF16)])
    return pl.pallas_call(
        _attn_kernel, grid_spec=grid_spec,
        out_shape=jax.ShapeDtypeStruct((n, AW), BF16),
        compiler_params=pltpu.CompilerParams(dimension_semantics=("parallel",), vmem_limit_bytes=VMEM_LIMIT),
        name="attention",
    )(meta["first"], meta["last"], sink, q, kk, kk, kk, vv, vv, vv)


def _rec_kernel(cf_ref, cb_ref, first_ref, last_ref,
                xf_ref, xfp_ref, xfn_ref, xb_ref, xbp_ref, xbn_ref,
                cw_ref, cbias_ref, wr_ref, wi_ref, br_ref, bi_ref, lam_ref,
                hf_ref, hb_ref,
                ext_s, a_s, b_s, hs_s, ps_s, carry_s):
    i = pl.program_id(0)
    cw = cw_ref[...]
    cbias = cbias_ref[...]

    def run(d, c, x_ref, xp_ref, xn_ref, out_ref):
        first = first_ref[c] == 1
        last = last_ref[c] == 1
        ext_s[0:8] = jnp.where(first, 0.0, xp_ref[...])
        ext_s[8:8 + TM] = x_ref[...]
        ext_s[8 + TM:] = jnp.where(last, 0.0, xn_ref[...])
        xc = ext_s[6:6 + TM] * cw[0:1]
        for t in range(1, CONV_WIDTH):
            xc = xc + ext_s[6 + t:6 + t + TM] * cw[t:t + 1]
        xc = xc + cbias
        xb16 = xc.astype(BF16)
        zr = jnp.dot(xb16, wr_ref[d], preferred_element_type=F32) + br_ref[d:d + 1]
        zi = jnp.dot(xb16, wi_ref[d], preferred_element_type=F32) + bi_ref[d:d + 1]
        nlam = -lam_ref[d:d + 1]
        softplus = jnp.maximum(nlam, 0.0) + jnp.log1p(jnp.exp(-jnp.abs(nlam)))
        log_a = (-LRU_C * softplus) * _sigmoid(zr)
        a = jnp.exp(log_a)
        b = jnp.sqrt((1.0 + a * a) * jnp.tanh(-log_a)) * _sigmoid(zi) * xc
        for g in range(NLG):
            a_s[g] = a[:, g * 128:(g + 1) * 128]
            b_s[g] = b[:, g * 128:(g + 1) * 128]

        reverse = d == 1
        reset = last if reverse else first

        @pl.when(reset)
        def _():
            carry_s[d] = jnp.zeros((NLG, 128), F32)

        def step(jj, hp):
            j = SEG - 1 - jj if reverse else jj
            out = []
            for g in range(NLG):
                h, p = hp[g]
                ag = a_s[g, pl.ds(j, 8, stride=SEG), :]
                bg = b_s[g, pl.ds(j, 8, stride=SEG), :]
                h = ag * h + bg
                p = p * ag
                hs_s[g, j] = h
                ps_s[g, j] = p
                out.append((h, p))
            return tuple(out)

        init = tuple((jnp.zeros((8, 128), F32), jnp.ones((8, 128), F32)) for _ in range(NLG))
        ends = lax.fori_loop(0, SEG, step, init)
        cmats = []
        for g in range(NLG):
            e, pe = ends[g]
            c_in = carry_s[d, g:g + 1]
            rows = [None] * 8
            for s in (range(7, -1, -1) if reverse else range(8)):
                rows[s] = c_in
                c_in = e[s:s + 1] + pe[s:s + 1] * c_in
            carry_s[d, g:g + 1] = c_in
            cmats.append(jnp.concatenate(rows, axis=0))

        def fix(j, _):
            for g in range(NLG):
                out_ref[g, pl.ds(j, 8, stride=SEG), :] = hs_s[g, j] + ps_s[g, j] * cmats[g]
            return 0

        lax.fori_loop(0, SEG, fix, 0)

    run(0, cf_ref[i], xf_ref, xfp_ref, xfn_ref, hf_ref)
    run(1, cb_ref[i], xb_ref, xbp_ref, xbn_ref, hb_ref)


def _recurrent(meta, xr, cw, cbias, wr, wi, br, bi, lam):
    n = xr.shape[0]
    nt = n // TM
    n8 = n // 8
    r8 = TM // 8

    def specs(which):
        sel = (lambda cf, cb: cf) if which == 0 else (lambda cf, cb: cb)
        main = pl.BlockSpec((TM, RW), lambda i, cf, cb, f, l: (sel(cf, cb)[i], 0))
        prev = pl.BlockSpec((8, RW), lambda i, cf, cb, f, l: (jnp.maximum(sel(cf, cb)[i] * r8 - 1, 0), 0))
        nxt = pl.BlockSpec((8, RW), lambda i, cf, cb, f, l: (jnp.minimum(sel(cf, cb)[i] * r8 + r8, n8 - 1), 0))
        return main, prev, nxt

    full = lambda a: pl.BlockSpec(a.shape, lambda i, cf, cb, f, l: (0,) * a.ndim)
    mf, pf, nf = specs(0)
    mb, pb, nb = specs(1)
    grid_spec = pltpu.PrefetchScalarGridSpec(
        num_scalar_prefetch=4, grid=(nt,),
        in_specs=[mf, pf, nf, mb, pb, nb, full(cw), full(cbias), full(wr), full(wi), full(br), full(bi), full(lam)],
        out_specs=[pl.BlockSpec((NLG, TM, 128), lambda i, cf, cb, f, l: (0, cf[i], 0)),
                   pl.BlockSpec((NLG, TM, 128), lambda i, cf, cb, f, l: (0, cb[i], 0))],
        scratch_shapes=[pltpu.VMEM((TM + 16, RW), F32),
                        pltpu.VMEM((NLG, TM, 128), F32), pltpu.VMEM((NLG, TM, 128), F32),
                        pltpu.VMEM((NLG, SEG, 8, 128), F32), pltpu.VMEM((NLG, SEG, 8, 128), F32),
                        pltpu.VMEM((2, NLG, 128), F32)])
    return pl.pallas_call(
        _rec_kernel, grid_spec=grid_spec,
        out_shape=[jax.ShapeDtypeStruct((NLG, n, 128), F32), jax.ShapeDtypeStruct((NLG, n, 128), F32)],
        compiler_params=pltpu.CompilerParams(dimension_semantics=("arbitrary",), vmem_limit_bytes=VMEM_LIMIT),
        name="recurrent",
    )(meta["cf"], meta["cb"], meta["first"], meta["last"], xr, xr, xr, xr, xr, xr, cw, cbias, wr, wi, br, bi, lam)


def _route(sel, score):
    one = jnp.ones_like(sel[0])
    zero = jnp.zeros_like(sel[0])

    def before(vk, vj, k, j):
        return (vk > vj) | ((vk == vj) & (k < j)) if k < j else (vk > vj)

    in_top = []
    gscore = []
    for g in range(NG):
        ids = range(g * EPG, (g + 1) * EPG)
        gs = zero
        for j in ids:
            rank = zero
            for k in ids:
                if k != j:
                    rank = rank + jnp.where(before(sel[k], sel[j], k, j), one, zero)
            m = jnp.where(rank < 2.0, one, zero)
            in_top.append(m)
            gs = gs + m * sel[j]
        gscore.append(gs)
    mask = []
    for g in range(NG):
        worse = zero
        for k in range(NG):
            if k != g:
                worse = worse + jnp.where(before(gscore[k], gscore[g], k, g), one, zero)
        best = jnp.where(worse < 1.0, one, zero)
        for j in range(g * EPG, (g + 1) * EPG):
            mask.append(in_top[j] * best)
    total = zero
    for e in range(NE):
        total = total + mask[e] * score[e]
    inv = 1.0 / total
    gate = [mask[e] * score[e] * inv for e in range(NE)]
    return mask, gate


def _merge_kernel(bidx_ref, x_ref, oa_ref, gy_ref, hf_ref, hb_ref, u_ref, vn_ref,
                  mod1_ref, mod2_ref, gmix_ref, sgw_ref, sgb_ref, wout_ref, lng_ref, lnb_ref, rwt_ref, rb_ref,
                  x1_ref, h2_ref, gate_ref, mask_ref, cnt_ref, mrg_s):
    del bidx_ref
    gmix = gmix_ref[...]
    mrg_s[:, 0:AW] = (_rms(oa_ref[...].astype(F32)) * gmix[:, 0:AW]).astype(BF16)
    hsum = jnp.concatenate([hf_ref[g] + hb_ref[g] for g in range(NLG)], axis=1)
    o_rec = gy_ref[...].astype(F32) * hsum
    mrg_s[:, AW:AW + RW] = (_rms(o_rec) * gmix[:, AW:AW + RW]).astype(BF16)
    lane = lax.broadcasted_iota(jnp.int32, (1, 128), 1)
    lo_half = lane < HD
    zero = jnp.zeros((), BF16)
    pieces = []
    for c in range(TM // CHUNK):
        rows = slice(c * CHUNK, (c + 1) * CHUNK)
        grp = []
        for g in range(SGW // 128):
            vg = vn_ref[rows, g * 128:(g + 1) * 128]
            mixed = jnp.dot(sgw_ref[2 * g], jnp.where(lo_half, vg, zero), preferred_element_type=F32)
            mixed += jnp.dot(sgw_ref[2 * g + 1], jnp.where(lo_half, zero, vg), preferred_element_type=F32)
            grp.append(mixed)
        mixed = jnp.concatenate(grp, axis=1) + sgb_ref[...]
        pieces.append(u_ref[rows, :].astype(F32) * mixed)
    o_sg = jnp.concatenate(pieces, axis=0)
    mrg_s[:, AW + RW:] = (_rms(o_sg) * gmix[:, AW + RW:]).astype(BF16)
    o = jnp.dot(mrg_s[...], wout_ref[...], preferred_element_type=F32)
    gate1 = mod1_ref[...][:, 2 * D:]
    x1 = _ln(ALPHA * x_ref[...] + gate1 * o) * lng_ref[...] + lnb_ref[...]
    x1_ref[...] = x1
    mod2 = mod2_ref[...]
    h2 = (_ln(x1) * (1.0 + mod2[:, D:2 * D]) + mod2[:, :D]).astype(BF16)
    h2_ref[...] = h2
    logits = lax.dot_general(rwt_ref[...], h2, (((1,), (1,)), ((), ())), preferred_element_type=F32)
    score = _sigmoid(logits)
    sel = score + rb_ref[...]
    mask, gate = _route([sel[e:e + 1] for e in range(NE)], [score[e:e + 1] for e in range(NE)])
    gate_ref[...] = jnp.concatenate(gate, axis=0)
    mask_t = jnp.concatenate(mask, axis=0)
    mask_ref[...] = mask_t
    t_i = lax.broadcasted_iota(jnp.int32, (TM, 128), 0)
    j_i = lax.broadcasted_iota(jnp.int32, (TM, 128), 1)
    in_tile = jnp.where((t_i >= j_i * TD) & (t_i < (j_i + 1) * TD), 1.0, 0.0).astype(BF16)
    cnt_ref[...] = jnp.dot(mask_t.astype(BF16), in_tile, preferred_element_type=F32)


def _merge(meta, x, oa, gy, hf, hb, u, vn, mod1, mod2, gmix, sgw, sgb, wout, lng, lnb, rwt, rb):
    n = x.shape[0]
    nt = n // TM
    tok = lambda w: pl.BlockSpec((TM, w), lambda i, b: (i, 0))
    full = lambda a: pl.BlockSpec(a.shape, lambda i, b: (0,) * a.ndim)
    modspec = pl.BlockSpec((None, 1, 3 * D), lambda i, b: (b[i], 0, 0))
    tspec = pl.BlockSpec((NE, TM), lambda i, b: (0, i))
    rec = pl.BlockSpec((NLG, TM, 128), lambda i, b: (0, i, 0))
    grid_spec = pltpu.PrefetchScalarGridSpec(
        num_scalar_prefetch=1, grid=(nt,),
        in_specs=[tok(D), tok(AW), tok(RW), rec, rec, tok(SGW), tok(SGW), modspec, modspec,
                  full(gmix), full(sgw), full(sgb), full(wout), full(lng), full(lnb), full(rwt), full(rb)],
        out_specs=[tok(D), tok(D), tspec, tspec, pl.BlockSpec((None, NE, 128), lambda i, b: (i, 0, 0))],
        scratch_shapes=[pltpu.VMEM((TM, D), BF16)])
    return pl.pallas_call(
        _merge_kernel, grid_spec=grid_spec,
        out_shape=[jax.ShapeDtypeStruct((n, D), F32), jax.ShapeDtypeStruct((n, D), BF16),
                   jax.ShapeDtypeStruct((NE, n), F32), jax.ShapeDtypeStruct((NE, n), F32),
                   jax.ShapeDtypeStruct((nt, NE, 128), F32)],
        compiler_params=pltpu.CompilerParams(dimension_semantics=("parallel",), vmem_limit_bytes=VMEM_LIMIT),
        name="merge_route",
    )(meta["bidx"], x, oa, gy, hf, hb, u, vn, mod1, mod2, gmix, sgw, sgb, wout, lng, lnb, rwt, rb)


def _plan(cnt, n_mt):
    pc = (cnt + (ALIGN - 1)) // ALIGN * ALIGN
    lo = jnp.cumsum(pc, axis=1) - pc
    tot = jnp.sum(pc, axis=0)
    seg = (tot + (TMX - 1)) // TMX * TMX
    gend = jnp.cumsum(seg)
    gstart = gend - seg
    dst = gstart[None, :] + jnp.cumsum(pc, axis=0) - pc
    texp = jnp.minimum(jnp.searchsorted(gend, jnp.arange(n_mt, dtype=jnp.int32) * TMX, side="right"), NE - 1)
    nact = (gend[-1] // TMX).reshape(1)
    i32 = lambda a: a.astype(jnp.int32)
    return dict(pc=i32(pc).reshape(-1), lo=i32(lo).reshape(-1), dst=i32(dst).reshape(-1),
                zdst=i32(jnp.concatenate([gstart + tot, gend[-1:]])), zlen=i32(seg - tot),
                texp=i32(texp), nact=i32(nact))


def _run_copies(src, dst, src_off, dst_off, rows, sem, start, max_bits):
    m = rows // ALIGN
    for b in range(max_bits):
        size = ALIGN << b

        @pl.when(((m >> b) & 1) == 1)
        def _():
            off = ((m >> (b + 1)) << (b + 1)) * ALIGN
            cp = pltpu.make_async_copy(
                src.at[pl.ds(pl.multiple_of(src_off + off, ALIGN), size)],
                dst.at[pl.ds(pl.multiple_of(dst_off + off, ALIGN), size)], sem)
            if start:
                cp.start()
            else:
                cp.wait()


RUN_BITS = (TD // ALIGN).bit_length()


def _dispatch_kernel(pc_ref, lo_ref, dst_ref, zd_ref, zl_ref, mask_ref, gate_ref, h_ref,
                     xs_ref, info_ref, loc_s, zero_s, sem):
    i = pl.program_id(0)
    mask = mask_ref[...]
    mb = mask.astype(BF16)
    s_i = lax.broadcasted_iota(jnp.int32, (TD, TD), 0)
    t_i = lax.broadcasted_iota(jnp.int32, (TD, TD), 1)
    earlier = jnp.where(s_i < t_i, 1.0, 0.0).astype(BF16)
    rank = jnp.dot(mb, earlier, preferred_element_type=F32)
    e_i = lax.broadcasted_iota(jnp.int32, (NE, NE), 0)
    f_i = lax.broadcasted_iota(jnp.int32, (NE, NE), 1)
    below = jnp.where(f_i < e_i, 1.0, 0.0).astype(BF16)
    slot = jnp.dot(below, mb, preferred_element_type=F32)
    row_e = lax.broadcasted_iota(jnp.int32, (NE, 1), 0)
    lo_vec = jnp.zeros((NE, 1), F32)
    for e in range(NE):
        lo_vec = jnp.where(row_e == e, lo_ref[i * NE + e].astype(F32), lo_vec)
    row = lo_vec + rank
    is0 = mask * jnp.where(slot == 0.0, 1.0, 0.0)
    is1 = mask - is0
    d0 = jnp.sum(is0 * row, axis=0, keepdims=True)
    d1 = jnp.sum(is1 * row, axis=0, keepdims=True)
    gate = gate_ref[...]
    w0 = jnp.sum(is0 * gate, axis=0, keepdims=True)
    w1 = jnp.sum(is1 * gate, axis=0, keepdims=True)
    info_ref[...] = jnp.concatenate([d0, d1, w0, w1, jnp.zeros((4, TD), F32)], axis=0)
    r_i = lax.broadcasted_iota(jnp.int32, (RL, TD), 0)
    perm = jnp.where((r_i == d0.astype(jnp.int32)) | (r_i == d1.astype(jnp.int32)), 1.0, 0.0).astype(BF16)
    loc_s[...] = jnp.dot(perm, h_ref[...], preferred_element_type=F32).astype(BF16)
    for start in (True, False):
        for e in range(NE):
            _run_copies(loc_s, xs_ref, lo_ref[i * NE + e], dst_ref[i * NE + e], pc_ref[i * NE + e], sem,
                        start, RUN_BITS)

    @pl.when(i == pl.num_programs(0) - 1)
    def _():
        zero_s[...] = jnp.zeros_like(zero_s)
        for start in (True, False):
            for e in range(NE):
                for part in range(TMX // ZROWS):
                    rows = jnp.clip(zl_ref[e] - part * ZROWS, 0, ZROWS)
                    _run_copies(zero_s, xs_ref, 0, zd_ref[e] + part * ZROWS, rows, sem, start,
                                (ZROWS // ALIGN).bit_length())
        tail = zd_ref[NE]
        for start in (True, False):
            def fill(c, carry, start=start):
                cp = pltpu.make_async_copy(
                    zero_s, xs_ref.at[pl.ds(pl.multiple_of(tail + c * ZROWS, ZROWS), ZROWS)], sem)
                if start:
                    cp.start()
                else:
                    cp.wait()
                return carry
            lax.fori_loop(0, (xs_ref.shape[0] - tail) // ZROWS, fill, 0)


def _dispatch(plan, mask_t, gate_t, h2, rtot):
    n = h2.shape[0]
    ntd = n // TD
    tspec = pl.BlockSpec((NE, TD), lambda i, *_: (0, i))
    grid_spec = pltpu.PrefetchScalarGridSpec(
        num_scalar_prefetch=5, grid=(ntd,),
        in_specs=[tspec, tspec, pl.BlockSpec((TD, D), lambda i, *_: (i, 0))],
        out_specs=[pl.BlockSpec(memory_space=pl.ANY), pl.BlockSpec((8, TD), lambda i, *_: (0, i))],
        scratch_shapes=[pltpu.VMEM((RL, D), BF16), pltpu.VMEM((ZROWS, D), BF16), pltpu.SemaphoreType.DMA])
    return pl.pallas_call(
        _dispatch_kernel, grid_spec=grid_spec,
        out_shape=[jax.ShapeDtypeStruct((rtot, D), BF16), jax.ShapeDtypeStruct((8, n), F32)],
        compiler_params=pltpu.CompilerParams(dimension_semantics=("arbitrary",), vmem_limit_bytes=VMEM_LIMIT),
        name="dispatch",
    )(plan["pc"], plan["lo"], plan["dst"], plan["zdst"], plan["zlen"], mask_t, gate_t, h2)


def _expert_kernel(texp_ref, nact_ref, x_ref, w1_ref, w3_ref, w2_ref, y_ref):
    del texp_ref
    m = pl.program_id(0)

    @pl.when(m < nact_ref[0])
    def _():
        x = x_ref[...]
        a = jnp.dot(x, w1_ref[...], preferred_element_type=F32)
        a = a * _sigmoid(a) * jnp.dot(x, w3_ref[...], preferred_element_type=F32)
        y_ref[...] = jnp.dot(a.astype(BF16), w2_ref[...], preferred_element_type=F32).astype(BF16)

    @pl.when(m >= nact_ref[0])
    def _():
        y_ref[...] = jnp.zeros_like(y_ref)


def _experts(plan, xs, w1, w3, w2):
    rtot = xs.shape[0]
    n_mt = rtot // TMX
    grid_spec = pltpu.PrefetchScalarGridSpec(
        num_scalar_prefetch=2, grid=(n_mt,),
        in_specs=[pl.BlockSpec((TMX, D), lambda m, te, na: (jnp.minimum(m, na[0] - 1), 0)),
                  pl.BlockSpec((None, D, FF), lambda m, te, na: (te[m], 0, 0)),
                  pl.BlockSpec((None, D, FF), lambda m, te, na: (te[m], 0, 0)),
                  pl.BlockSpec((None, FF, D), lambda m, te, na: (te[m], 0, 0))],
        out_specs=pl.BlockSpec((TMX, D), lambda m, te, na: (m, 0)))
    return pl.pallas_call(
        _expert_kernel, grid_spec=grid_spec,
        out_shape=jax.ShapeDtypeStruct((rtot, D), BF16),
        compiler_params=pltpu.CompilerParams(dimension_semantics=("arbitrary",), vmem_limit_bytes=VMEM_LIMIT),
        name="experts",
    )(plan["texp"], plan["nact"], xs, w1, w3, w2)


def _combine_kernel(bidx_ref, pc_ref, lo_ref, dst_ref, info_ref, ys_ref, x1_ref, mod_ref, lng_ref, lnb_ref,
                    o_ref, loc_s, sem):
    del bidx_ref
    i = pl.program_id(0)

    @pl.when(i == 0)
    def _():
        loc_s[...] = jnp.zeros_like(loc_s)

    for start in (True, False):
        for e in range(NE):
            _run_copies(ys_ref, loc_s, dst_ref[i * NE + e], lo_ref[i * NE + e], pc_ref[i * NE + e], sem,
                        start, RUN_BITS)
    info = info_ref[...]
    d0 = info[0:1].astype(jnp.int32)
    d1 = info[1:2].astype(jnp.int32)
    r_i = lax.broadcasted_iota(jnp.int32, (RL, TD), 0)
    wperm = (jnp.where(r_i == d0, info[2:3], 0.0) + jnp.where(r_i == d1, info[3:4], 0.0)).astype(BF16)
    moe = lax.dot_general(wperm, loc_s[...], (((0,), (0,)), ((), ())), preferred_element_type=F32)
    gate = mod_ref[...][:, 2 * D:]
    o_ref[...] = _ln(ALPHA * x1_ref[...] + gate * moe) * lng_ref[...] + lnb_ref[...]


def _combine(bidx_d, plan, info, ys, x1, mod2, lng, lnb):
    n = x1.shape[0]
    ntd = n // TD
    tok = pl.BlockSpec((TD, D), lambda i, *_: (i, 0))
    full = lambda a: pl.BlockSpec(a.shape, lambda i, *_: (0,) * a.ndim)
    grid_spec = pltpu.PrefetchScalarGridSpec(
        num_scalar_prefetch=4, grid=(ntd,),
        in_specs=[pl.BlockSpec((8, TD), lambda i, *_: (0, i)),
                  pl.BlockSpec(memory_space=pl.ANY),
                  tok,
                  pl.BlockSpec((None, 1, 3 * D), lambda i, b, *_: (b[i], 0, 0)),
                  full(lng), full(lnb)],
        out_specs=tok,
        scratch_shapes=[pltpu.VMEM((RL, D), BF16), pltpu.SemaphoreType.DMA])
    return pl.pallas_call(
        _combine_kernel, grid_spec=grid_spec,
        out_shape=jax.ShapeDtypeStruct((n, D), F32),
        compiler_params=pltpu.CompilerParams(dimension_semantics=("arbitrary",), vmem_limit_bytes=VMEM_LIMIT),
        name="combine",
    )(bidx_d, plan["pc"], plan["lo"], plan["dst"], info, ys, x1, mod2, lng, lnb)


def _moe(meta, h2, mask_t, gate_t, cnt, w1, w3, w2, x1, mod2, lng, lnb):
    n = h2.shape[0]
    ntd = n // TD
    rtot = ntd * RL + NE * TMX
    cnt = cnt[:, :, :TM // TD].transpose(0, 2, 1).reshape(ntd, NE).astype(jnp.int32)
    plan = _plan(cnt, rtot // TMX)
    xs, info = _dispatch(plan, mask_t, gate_t, h2, rtot)
    ys = _experts(plan, xs, w1, w3, w2)
    return _combine(meta["bidx_d"], plan, info, ys, x1, mod2, lng, lnb)


def _tile_meta(groups):
    bidx, pblk, first, last, cf, cb, bidx_d = [], [], [], [], [], [], []
    row = 0
    tile = 0
    for (b, s) in groups:
        per = s // TM
        for bi in range(b):
            for j in range(per):
                bidx.append(row + bi)
                pblk.append(j)
                first.append(1 if j == 0 else 0)
                last.append(1 if j == per - 1 else 0)
                cf.append(tile + j)
                cb.append(tile + per - 1 - j)
            tile += per
            bidx_d += [row + bi] * (s // TD)
        row += b
    as_i32 = lambda v: jnp.asarray(np.asarray(v, np.int32))
    return dict(bidx=as_i32(bidx), pblk=as_i32(pblk), first=as_i32(first), last=as_i32(last),
                cf=as_i32(cf), cb=as_i32(cb), bidx_d=as_i32(bidx_d))


def _rope_tables(s_max):
    inv = jnp.power(ROPE_THETA, -jnp.arange(0, HD, 2, dtype=F32) / HD)
    ang = jnp.arange(s_max, dtype=F32)[:, None] * inv[None, :]
    cos, sin = jnp.cos(ang), jnp.sin(ang)
    cos128 = jnp.tile(cos, (1, 4))
    sin128 = jnp.tile(jnp.concatenate([-sin, sin], axis=1), (1, 2))
    return cos128, sin128


def _block_diag(w):
    d, nb, c, f = w.shape
    eye = jnp.eye(nb, dtype=w.dtype)
    return jnp.einsum('dncf,nm->dncmf', w, eye).reshape(d, nb * c, nb * f)


def _forward(xs, cs, w_mod, b_mod, w_in, attn_sink, conv_w, conv_b, lru_w_r, lru_b_r, lru_w_i, lru_b_i,
             lru_lambda, sg_norm_g, sg_w, sg_b, mix_norm_g, w_out, ln_g, ln_b, router_w, router_bias,
             exp_w1, exp_w3, exp_w2):
    groups = [(x.shape[0], x.shape[1]) for x in xs]
    for (_, s) in groups:
        assert s % TM == 0
    meta = _tile_meta(groups)
    x = jnp.concatenate([xx.reshape(-1, D) for xx in xs], axis=0)
    c_all = jnp.concatenate(cs, axis=0)
    bt = c_all.shape[0]
    mods = _modulation(c_all, w_mod, b_mod).reshape(2 * DEPTH, bt, 1, 3 * D)
    cos_t, sin_t = _rope_tables(max(s for _, s in groups))
    rwt = router_w.T.astype(BF16)
    rb = router_bias.reshape(NE, 1)
    for l in range(DEPTH):
        wi = w_in[l]
        kcols = wi[:, AW:AW + KVW]
        vcols = wi[:, AW + KVW:AW + 2 * KVW]
        dup = lambda m: jnp.concatenate([m[:, :HD], m[:, :HD], m[:, HD:], m[:, HD:]], axis=1)
        w_ext = jnp.concatenate([wi[:, :AW], dup(kcols), dup(vcols), wi[:, AW + 2 * KVW:]], axis=1).astype(BF16)
        q, kk, vv, gy, xr, u, vn = _pre_mixer(meta, x, mods[2 * l], w_ext, cos_t, sin_t,
                                              sg_norm_g[l].reshape(1, SGW))
        oa = _attention(meta, attn_sink[l], q, kk, vv)
        hf, hb = _recurrent(meta, xr, conv_w[l], conv_b[l].reshape(1, RW),
                            _block_diag(lru_w_r[l]).astype(BF16), _block_diag(lru_w_i[l]).astype(BF16),
                            lru_b_r[l], lru_b_i[l], lru_lambda[l])
        sgb = jnp.repeat(sg_b[l].T, HD, axis=1)
        x1, h2, gate_t, mask_t, cnt = _merge(meta, x, oa, gy, hf, hb, u, vn, mods[2 * l], mods[2 * l + 1],
                                             mix_norm_g[l].reshape(1, D), sg_w[l].astype(BF16), sgb,
                                             w_out[l].astype(BF16), ln_g[l, 0].reshape(1, D),
                                             ln_b[l, 0].reshape(1, D), rwt, rb)
        x = _moe(meta, h2, mask_t, gate_t, cnt, exp_w1[l].astype(BF16), exp_w3[l].astype(BF16),
                 exp_w2[l].astype(BF16), x1, mods[2 * l + 1], ln_g[l, 1].reshape(1, D), ln_b[l, 1].reshape(1, D))
    outs = []
    off = 0
    for (b, s) in groups:
        outs.append(x[off:off + b * s].reshape(b, s, D))
        off += b * s
    return tuple(outs)


def kernel(x_prompt, x_sample, c_prompt, c_sample, w_mod, b_mod, w_in, attn_sink, conv_w, conv_b, lru_w_r, lru_b_r,
           lru_w_i, lru_b_i, lru_lambda, sg_norm_g, sg_w, sg_b, mix_norm_g, w_out, ln_g, ln_b, router_w,
           router_bias, exp_w1, exp_w3, exp_w2):
    return _forward([x_prompt, x_sample], [c_prompt, c_sample], w_mod, b_mod, w_in, attn_sink, conv_w, conv_b,
                    lru_w_r, lru_b_r, lru_w_i, lru_b_i, lru_lambda, sg_norm_g, sg_w, sg_b, mix_norm_g, w_out,
                    ln_g, ln_b, router_w, router_bias, exp_w1, exp_w3, exp_w2)
```

```python
import functools

import numpy as np
import jax
import jax.numpy as jnp
from jax import lax
from jax.experimental import pallas as pl
from jax.experimental.pallas import tpu as pltpu

F32 = jnp.float32
BF16 = jnp.bfloat16

D = 1024
DEPTH = 2
HD = 64
AW = 384
KVW = 128
RW = 384
SGW = 256
WINDOW = 128
CONV_WIDTH = 4
LRU_C = 8.0
CHUNK = 128
NE = 16
NG = 4
EPG = NE // NG
FF = 512
ALPHA = (2 * DEPTH) ** 0.25
LN_EPS = 1e-5
RMS_EPS = 1e-6
ROPE_THETA = 10000.0

TM = 512
QB = 128
SEG = TM // 8
NLG = RW // 128
TD = 256
ALIGN = 16
RL = 2 * TD + NE * ALIGN
TMX = 512
ZROWS = 256
NEG = -1e30
VMEM_LIMIT = 48 * 1024 * 1024

C_Q, C_K, C_V, C_Y, C_R, C_U, C_SV, C_END = 0, 384, 640, 896, 1280, 1664, 1920, 2176


def _ln(x):
    mu = jnp.mean(x, axis=-1, keepdims=True)
    xc = x - mu
    var = jnp.mean(xc * xc, axis=-1, keepdims=True)
    return xc * lax.rsqrt(var + LN_EPS)


def _rms(x):
    return x * lax.rsqrt(jnp.mean(x * x, axis=-1, keepdims=True) + RMS_EPS)


def _sigmoid(z):
    return 0.5 * (jnp.tanh(0.5 * z) + 1.0)


def _gelu(x):
    return 0.5 * x * (1.0 + jnp.tanh(0.7978845608028654 * (x + 0.044715 * (x * x * x))))


def _split_bf16(a):
    hi = a.astype(BF16)
    lo = (a - hi.astype(F32)).astype(BF16)
    return hi, lo


def _mod_kernel(c_ref, w_ref, b_ref, o_ref):
    c = c_ref[...]
    ca = c * _sigmoid(c)
    ch, cl = _split_bf16(ca)
    wh, wl = _split_bf16(w_ref[...])
    acc = jnp.dot(ch, wh, preferred_element_type=F32)
    acc += jnp.dot(ch, wl, preferred_element_type=F32)
    acc += jnp.dot(cl, wh, preferred_element_type=F32)
    o_ref[...] = acc + b_ref[...]


def _modulation(c_all, w_mod, b_mod):
    bt = c_all.shape[0]
    cb = 768
    w = w_mod.reshape(2 * DEPTH, D, 3 * D)
    b = b_mod.reshape(2 * DEPTH, 1, 3 * D)
    return pl.pallas_call(
        _mod_kernel,
        grid=(2 * DEPTH, 3 * D // cb),
        in_specs=[pl.BlockSpec((bt, D), lambda s, j: (0, 0)),
                  pl.BlockSpec((None, D, cb), lambda s, j: (s, 0, j)),
                  pl.BlockSpec((None, 1, cb), lambda s, j: (s, 0, j))],
        out_specs=pl.BlockSpec((None, bt, cb), lambda s, j: (s, 0, j)),
        out_shape=jax.ShapeDtypeStruct((2 * DEPTH, bt, 3 * D), F32),
        name="modulation",
    )(c_all, w, b)


def _load_tokens(i, x_refs, split):
    if len(x_refs) == 1:
        return x_refs[0][...]
    return jnp.where(i < split, x_refs[0][...], x_refs[1][...])


def _token_specs(x_parts, rows):
    if len(x_parts) == 1:
        return [pl.BlockSpec((rows, D), lambda i, *_: (i, 0))], 0
    split = x_parts[0].shape[0] // rows
    return [pl.BlockSpec((rows, D), lambda i, *_: (jnp.minimum(i, split - 1), 0)),
            pl.BlockSpec((rows, D), lambda i, *_: (jnp.maximum(i - split, 0), 0))], split


def _pre_kernel(split, bidx_ref, pblk_ref, *refs):
    del bidx_ref, pblk_ref
    nx = len(refs) - 12
    x_refs = refs[:nx]
    mod_ref, w_ref, cos_ref, sin_ref, sgg_ref, q_ref, kk_ref, vv_ref, gy_ref, xr_ref, u_ref, vn_ref = refs[nx:]
    mod = mod_ref[...]
    shift, scale = mod[:, :D], mod[:, D:2 * D]
    h = (_ln(_load_tokens(pl.program_id(0), x_refs, split)) * (1.0 + scale) + shift).astype(BF16)
    cos = cos_ref[...]
    sin = sin_ref[...]
    lane = lax.broadcasted_iota(jnp.int32, (1, 128), 1)
    first_half = (lane % HD) < (HD // 2)

    def rope128(xg):
        rot = jnp.where(first_half, pltpu.roll(xg, 128 - HD // 2, 1), pltpu.roll(xg, HD // 2, 1))
        return xg * cos + rot * sin

    q = jnp.dot(h, w_ref[:, C_Q:C_K], preferred_element_type=F32)
    for g in range(AW // 128):
        q_ref[:, g * 128:(g + 1) * 128] = (rope128(q[:, g * 128:(g + 1) * 128]) * (HD ** -0.5)).astype(BF16)
    k = jnp.dot(h, w_ref[:, C_K:C_V], preferred_element_type=F32)
    for g in range(2):
        kk_ref[:, g * 128:(g + 1) * 128] = rope128(k[:, g * 128:(g + 1) * 128]).astype(BF16)
    vv_ref[...] = jnp.dot(h, w_ref[:, C_V:C_Y], preferred_element_type=F32).astype(BF16)
    gy_ref[...] = _gelu(jnp.dot(h, w_ref[:, C_Y:C_R], preferred_element_type=F32)).astype(gy_ref.dtype)
    xr = jnp.dot(h, w_ref[:, C_R:C_U], preferred_element_type=F32)
    for g in range(NLG):
        xr_ref[g] = xr[:, g * 128:(g + 1) * 128]
    u_ref[...] = _gelu(jnp.dot(h, w_ref[:, C_U:C_SV], preferred_element_type=F32)).astype(u_ref.dtype)
    sv = _gelu(jnp.dot(h, w_ref[:, C_SV:C_END], preferred_element_type=F32))
    vn_ref[...] = (_ln(sv) * sgg_ref[...]).astype(BF16)


def _pre_mixer(meta, x_parts, mod_l, w_ext, cos_t, sin_t, sgg):
    n = sum(p.shape[0] for p in x_parts)
    nt = n // TM
    tok = lambda w: pl.BlockSpec((TM, w), lambda i, b, p: (i, 0))
    full = lambda a: pl.BlockSpec(a.shape, lambda i, b, p: (0,) * a.ndim)
    x_specs, split = _token_specs(x_parts, TM)
    grid_spec = pltpu.PrefetchScalarGridSpec(
        num_scalar_prefetch=2, grid=(nt,),
        in_specs=x_specs + [
            pl.BlockSpec((None, 1, 3 * D), lambda i, b, p: (b[i], 0, 0)),
            full(w_ext),
            pl.BlockSpec((TM, 128), lambda i, b, p: (p[i], 0)),
            pl.BlockSpec((TM, 128), lambda i, b, p: (p[i], 0)),
            full(sgg)],
        out_specs=[tok(AW), tok(256), tok(256), tok(RW),
                   pl.BlockSpec((NLG, TM, 128), lambda i, b, p: (0, i, 0)), tok(SGW), tok(SGW)])
    sds = lambda w, dt: jax.ShapeDtypeStruct((n, w), dt)
    return pl.pallas_call(
        functools.partial(_pre_kernel, split), grid_spec=grid_spec,
        out_shape=[sds(AW, BF16), sds(256, BF16), sds(256, BF16), sds(RW, BF16),
                   jax.ShapeDtypeStruct((NLG, n, 128), F32), sds(SGW, BF16), sds(SGW, BF16)],
        compiler_params=pltpu.CompilerParams(dimension_semantics=("parallel",), vmem_limit_bytes=VMEM_LIMIT),
        name="pre_mixer",
    )(meta["bidx"], meta["pblk"], *x_parts, mod_l, w_ext, cos_t, sin_t, sgg)


def _attn_kernel(first_ref, last_ref, sink_ref, q_ref, k_ref, kp_ref, kn_ref, v_ref, vp_ref, vn_ref,
                 o_ref, kw_s, vw_s):
    i = pl.program_id(0)
    is_first = first_ref[i] == 1
    is_last = last_ref[i] == 1
    kw_s[0:QB] = kp_ref[...]
    kw_s[QB:QB + TM] = k_ref[...]
    kw_s[QB + TM:] = kn_ref[...]
    vw_s[0:QB] = vp_ref[...]
    vw_s[QB:QB + TM] = v_ref[...]
    vw_s[QB + TM:] = vn_ref[...]
    iq = lax.broadcasted_iota(jnp.int32, (QB, 3 * QB), 0)
    ik = lax.broadcasted_iota(jnp.int32, (QB, 3 * QB), 1)
    rel = ik - iq
    band = (rel >= 0) & (rel <= 2 * WINDOW)
    lane = lax.broadcasted_iota(jnp.int32, (1, 128), 1)
    lo_half = lane < HD
    zero = jnp.zeros((), BF16)
    gsz = AW // HD // 2
    row_head = lax.broadcasted_iota(jnp.int32, (gsz * QB, 1), 0) // QB
    for j in range(TM // QB):
        ok = band
        if j == 0:
            ok = ok & ((ik >= QB) | jnp.logical_not(is_first))
        if j == TM // QB - 1:
            ok = ok & ((ik < 2 * QB) | jnp.logical_not(is_last))
        bias = jnp.where(ok, 0.0, NEG)
        bias = jnp.concatenate([bias] * gsz, axis=0)
        rows = slice(j * QB, (j + 1) * QB)
        keys = slice(j * QB, j * QB + 3 * QB)
        outs = [None] * (AW // HD)
        for g in range(2):
            heads = range(g * gsz, (g + 1) * gsz)
            qs = []
            for hh in heads:
                qg = q_ref[rows, (hh // 2) * 128:(hh // 2 + 1) * 128]
                qs.append(jnp.where(lo_half if hh % 2 == 0 else jnp.logical_not(lo_half), qg, zero))
            s = lax.dot_general(jnp.concatenate(qs, axis=0), kw_s[keys, g * 128:(g + 1) * 128],
                                (((1,), (1,)), ((), ())), preferred_element_type=F32) + bias
            sink = jnp.full((gsz * QB, 1), sink_ref[heads[-1]], F32)
            for t in range(gsz - 1):
                sink = jnp.where(row_head == t, sink_ref[heads[t]], sink)
            m = jnp.maximum(jnp.max(s, axis=-1, keepdims=True), sink)
            p = jnp.exp(s - m)
            denom = jnp.sum(p, axis=-1, keepdims=True) + jnp.exp(sink - m)
            pv = jnp.dot(p.astype(BF16), vw_s[keys, g * 128:(g + 1) * 128], preferred_element_type=F32)
            o = pv * (1.0 / denom)
            for t, hh in enumerate(heads):
                outs[hh] = o[t * QB:(t + 1) * QB]
        for jg in range(AW // 128):
            o_ref[rows, jg * 128:(jg + 1) * 128] = jnp.where(lo_half, outs[2 * jg], outs[2 * jg + 1]).astype(o_ref.dtype)


def _attention(meta, sink, q, kk, vv):
    n = q.shape[0]
    nt = n // TM
    nqb = n // QB
    r = TM // QB
    main = lambda w: pl.BlockSpec((TM, w), lambda i, f, l, s: (i, 0))
    prev = pl.BlockSpec((QB, 256), lambda i, f, l, s: (jnp.maximum(i * r - 1, 0), 0))
    nxt = pl.BlockSpec((QB, 256), lambda i, f, l, s: (jnp.minimum(i * r + r, nqb - 1), 0))
    grid_spec = pltpu.PrefetchScalarGridSpec(
        num_scalar_prefetch=3, grid=(nt,),
        in_specs=[main(AW), main(256), prev, nxt, main(256), prev, nxt],
        out_specs=main(AW),
        scratch_shapes=[pltpu.VMEM((TM + 2 * QB, 256), BF16), pltpu.VMEM((TM + 2 * QB, 256), BF16)])
    return pl.pallas_call(
        _attn_kernel, grid_spec=grid_spec,
        out_shape=jax.ShapeDtypeStruct((n, AW), BF16),
        compiler_params=pltpu.CompilerParams(dimension_semantics=("parallel",), vmem_limit_bytes=VMEM_LIMIT),
        name="attention",
    )(meta["first"], meta["last"], sink, q, kk, kk, kk, vv, vv, vv)


def _rec_kernel(cf_ref, cb_ref, first_ref, last_ref,
                xf_ref, xfp_ref, xfn_ref, xb_ref, xbp_ref, xbn_ref,
                cw_ref, cbias_ref, wr_ref, wi_ref, br_ref, bi_ref, lam_ref,
                hf_ref, hb_ref,
                ext_s, a_s, b_s, hs_s, ps_s, carry_s):
    i = pl.program_id(0)
    cw = cw_ref[...]
    cbias = cbias_ref[...]
    sub = lax.broadcasted_iota(jnp.int32, (8, 128), 0)
    unroll = 8

    def run(d, c, x_ref, xp_ref, xn_ref, out_ref):
        first = first_ref[c] == 1
        last = last_ref[c] == 1

        def gather(jo, carry):
            for ji in range(unroll):
                j = jo * unroll + ji
                for g in range(NLG):
                    ext_s[g, pl.ds(pl.multiple_of(16 + j * 8, 8), 8), :] = x_ref[g, pl.ds(j, 8, stride=SEG), :]
            return carry

        lax.fori_loop(0, SEG // unroll, gather, 0)
        planes = []
        for g in range(NLG):
            prev = jnp.where(first, 0.0, xp_ref[g])
            nxt = jnp.where(last, 0.0, xn_ref[g])
            tail1 = ext_s[g, 16 + (SEG - 1) * 8:16 + SEG * 8, :]
            tail2 = ext_s[g, 16 + (SEG - 2) * 8:16 + (SEG - 1) * 8, :]
            head = ext_s[g, 16:24, :]
            ext_s[g, 0:8, :] = jnp.where(sub == 0, prev[6:7], pltpu.roll(tail2, 1, 0))
            ext_s[g, 8:16, :] = jnp.where(sub == 0, prev[7:8], pltpu.roll(tail1, 1, 0))
            ext_s[g, 16 + TM:24 + TM, :] = jnp.where(sub == 7, nxt[0:1], pltpu.roll(head, 7, 0))
            lanes = slice(g * 128, (g + 1) * 128)
            acc = ext_s[g, 0:TM, :] * cw[0:1, lanes]
            for t in range(1, CONV_WIDTH):
                acc = acc + ext_s[g, 8 * t:8 * t + TM, :] * cw[t:t + 1, lanes]
            planes.append(acc)
        xc = jnp.concatenate(planes, axis=1) + cbias
        xb16 = xc.astype(BF16)
        zr = jnp.dot(xb16, wr_ref[d], preferred_element_type=F32) + br_ref[d:d + 1]
        zi = jnp.dot(xb16, wi_ref[d], preferred_element_type=F32) + bi_ref[d:d + 1]
        nlam = -lam_ref[d:d + 1]
        softplus = jnp.maximum(nlam, 0.0) + jnp.log1p(jnp.exp(-jnp.abs(nlam)))
        log_a = (-LRU_C * softplus) * _sigmoid(zr)
        a = jnp.exp(log_a)
        b = jnp.sqrt((1.0 + a * a) * jnp.tanh(-log_a)) * _sigmoid(zi) * xc
        for g in range(NLG):
            a_s[g] = a[:, g * 128:(g + 1) * 128]
            b_s[g] = b[:, g * 128:(g + 1) * 128]

        reverse = d == 1
        reset = last if reverse else first

        @pl.when(reset)
        def _():
            carry_s[d] = jnp.zeros((NLG, 128), F32)

        def step(jo, hp):
            hp = list(hp)
            for ji in range(unroll):
                jj = jo * unroll + ji
                j = SEG - 1 - jj if reverse else jj
                r = pl.ds(pl.multiple_of(j * 8, 8), 8)
                for g in range(NLG):
                    h, p = hp[g]
                    ag = a_s[g, r, :]
                    h = ag * h + b_s[g, r, :]
                    p = p * ag
                    hs_s[g, r, :] = h
                    ps_s[g, r, :] = p
                    hp[g] = (h, p)
            return tuple(hp)

        init = tuple((jnp.zeros((8, 128), F32), jnp.ones((8, 128), F32)) for _ in range(NLG))
        ends = lax.fori_loop(0, SEG // unroll, step, init)
        cmats = []
        for g in range(NLG):
            e, pe = ends[g]
            c_in = carry_s[d, g:g + 1]
            rows = [None] * 8
            for s in (range(7, -1, -1) if reverse else range(8)):
                rows[s] = c_in
                c_in = e[s:s + 1] + pe[s:s + 1] * c_in
            carry_s[d, g:g + 1] = c_in
            cmats.append(jnp.concatenate(rows, axis=0))

        def fix(jo, carry):
            for ji in range(unroll):
                j = jo * unroll + ji
                r = pl.ds(pl.multiple_of(j * 8, 8), 8)
                for g in range(NLG):
                    out_ref[g, pl.ds(j, 8, stride=SEG), :] = hs_s[g, r, :] + ps_s[g, r, :] * cmats[g]
            return carry

        lax.fori_loop(0, SEG // unroll, fix, 0)

    run(0, cf_ref[i], xf_ref, xfp_ref, xfn_ref, hf_ref)
    run(1, cb_ref[i], xb_ref, xbp_ref, xbn_ref, hb_ref)


def _recurrent(meta, xr, cw, cbias, wr, wi, br, bi, lam):
    n = xr.shape[1]
    nt = n // TM
    n8 = n // 8
    r8 = TM // 8

    def specs(which):
        sel = (lambda cf, cb: cf) if which == 0 else (lambda cf, cb: cb)
        main = pl.BlockSpec((NLG, TM, 128), lambda i, cf, cb, f, l: (0, sel(cf, cb)[i], 0))
        prev = pl.BlockSpec((NLG, 8, 128),
                            lambda i, cf, cb, f, l: (0, jnp.maximum(sel(cf, cb)[i] * r8 - 1, 0), 0))
        nxt = pl.BlockSpec((NLG, 8, 128),
                           lambda i, cf, cb, f, l: (0, jnp.minimum(sel(cf, cb)[i] * r8 + r8, n8 - 1), 0))
        return main, prev, nxt

    full = lambda a: pl.BlockSpec(a.shape, lambda i, cf, cb, f, l: (0,) * a.ndim)
    mf, pf, nf = specs(0)
    mb, pb, nb = specs(1)
    grid_spec = pltpu.PrefetchScalarGridSpec(
        num_scalar_prefetch=4, grid=(nt,),
        in_specs=[mf, pf, nf, mb, pb, nb, full(cw), full(cbias), full(wr), full(wi), full(br), full(bi), full(lam)],
        out_specs=[pl.BlockSpec((NLG, TM, 128), lambda i, cf, cb, f, l: (0, cf[i], 0)),
                   pl.BlockSpec((NLG, TM, 128), lambda i, cf, cb, f, l: (0, cb[i], 0))],
        scratch_shapes=[pltpu.VMEM((NLG, TM + 24, 128), F32),
                        pltpu.VMEM((NLG, TM, 128), F32), pltpu.VMEM((NLG, TM, 128), F32),
                        pltpu.VMEM((NLG, TM, 128), F32), pltpu.VMEM((NLG, TM, 128), F32),
                        pltpu.VMEM((2, NLG, 128), F32)])
    return pl.pallas_call(
        _rec_kernel, grid_spec=grid_spec,
        out_shape=[jax.ShapeDtypeStruct((NLG, n, 128), F32), jax.ShapeDtypeStruct((NLG, n, 128), F32)],
        compiler_params=pltpu.CompilerParams(dimension_semantics=("arbitrary",), vmem_limit_bytes=VMEM_LIMIT),
        name="recurrent",
    )(meta["cf"], meta["cb"], meta["first"], meta["last"], xr, xr, xr, xr, xr, xr, cw, cbias, wr, wi, br, bi, lam)


def _route(sel, score):
    one = jnp.ones_like(sel[0])
    zero = jnp.zeros_like(sel[0])

    def before(vk, vj, k, j):
        return (vk > vj) | ((vk == vj) & (k < j)) if k < j else (vk > vj)

    in_top = []
    gscore = []
    for g in range(NG):
        ids = range(g * EPG, (g + 1) * EPG)
        gs = zero
        for j in ids:
            rank = zero
            for k in ids:
                if k != j:
                    rank = rank + jnp.where(before(sel[k], sel[j], k, j), one, zero)
            m = jnp.where(rank < 2.0, one, zero)
            in_top.append(m)
            gs = gs + m * sel[j]
        gscore.append(gs)
    mask = []
    for g in range(NG):
        worse = zero
        for k in range(NG):
            if k != g:
                worse = worse + jnp.where(before(gscore[k], gscore[g], k, g), one, zero)
        best = jnp.where(worse < 1.0, one, zero)
        for j in range(g * EPG, (g + 1) * EPG):
            mask.append(in_top[j] * best)
    total = zero
    for e in range(NE):
        total = total + mask[e] * score[e]
    inv = 1.0 / total
    gate = [mask[e] * score[e] * inv for e in range(NE)]
    return mask, gate


def _merge_kernel(split, bidx_ref, *refs):
    del bidx_ref
    nx = len(refs) - 22
    x_refs = refs[:nx]
    (oa_ref, gy_ref, hf_ref, hb_ref, u_ref, vn_ref, mod1_ref, mod2_ref, gmix_ref, sgw_ref, sgb_ref, wout_ref,
     lng_ref, lnb_ref, rwt_ref, rb_ref, x1_ref, h2_ref, gate_ref, mask_ref, cnt_ref, mrg_s) = refs[nx:]
    gmix = gmix_ref[...]
    mrg_s[:, 0:AW] = (_rms(oa_ref[...].astype(F32)) * gmix[:, 0:AW]).astype(BF16)
    hsum = jnp.concatenate([hf_ref[g] + hb_ref[g] for g in range(NLG)], axis=1)
    o_rec = gy_ref[...].astype(F32) * hsum
    mrg_s[:, AW:AW + RW] = (_rms(o_rec) * gmix[:, AW:AW + RW]).astype(BF16)
    lane = lax.broadcasted_iota(jnp.int32, (1, 128), 1)
    lo_half = lane < HD
    zero = jnp.zeros((), BF16)
    pieces = []
    for c in range(TM // CHUNK):
        rows = slice(c * CHUNK, (c + 1) * CHUNK)
        grp = []
        for g in range(SGW // 128):
            vg = vn_ref[rows, g * 128:(g + 1) * 128]
            mixed = jnp.dot(sgw_ref[2 * g], jnp.where(lo_half, vg, zero), preferred_element_type=F32)
            mixed += jnp.dot(sgw_ref[2 * g + 1], jnp.where(lo_half, zero, vg), preferred_element_type=F32)
            grp.append(mixed)
        mixed = jnp.concatenate(grp, axis=1) + sgb_ref[...]
        pieces.append(u_ref[rows, :].astype(F32) * mixed)
    o_sg = jnp.concatenate(pieces, axis=0)
    mrg_s[:, AW + RW:] = (_rms(o_sg) * gmix[:, AW + RW:]).astype(BF16)
    o = jnp.dot(mrg_s[...], wout_ref[...], preferred_element_type=F32)
    gate1 = mod1_ref[...][:, 2 * D:]
    x_in = _load_tokens(pl.program_id(0), x_refs, split)
    x1 = _ln(ALPHA * x_in + gate1 * o) * lng_ref[...] + lnb_ref[...]
    x1_ref[...] = x1
    mod2 = mod2_ref[...]
    h2 = (_ln(x1) * (1.0 + mod2[:, D:2 * D]) + mod2[:, :D]).astype(BF16)
    h2_ref[...] = h2
    logits = lax.dot_general(rwt_ref[...], h2, (((1,), (1,)), ((), ())), preferred_element_type=F32)
    score = _sigmoid(logits)
    sel = score + rb_ref[...]
    mask, gate = _route([sel[e:e + 1] for e in range(NE)], [score[e:e + 1] for e in range(NE)])
    gate_ref[...] = jnp.concatenate(gate, axis=0)
    mask_t = jnp.concatenate(mask, axis=0)
    mask_ref[...] = mask_t
    t_i = lax.broadcasted_iota(jnp.int32, (TM, 128), 0)
    j_i = lax.broadcasted_iota(jnp.int32, (TM, 128), 1)
    in_tile = jnp.where((t_i >= j_i * TD) & (t_i < (j_i + 1) * TD), 1.0, 0.0).astype(BF16)
    cnt_ref[...] = jnp.dot(mask_t.astype(BF16), in_tile, preferred_element_type=F32)


def _merge(meta, x_parts, oa, gy, hf, hb, u, vn, mod1, mod2, gmix, sgw, sgb, wout, lng, lnb, rwt, rb):
    n = oa.shape[0]
    nt = n // TM
    x_specs, split = _token_specs(x_parts, TM)
    tok = lambda w: pl.BlockSpec((TM, w), lambda i, b: (i, 0))
    full = lambda a: pl.BlockSpec(a.shape, lambda i, b: (0,) * a.ndim)
    modspec = pl.BlockSpec((None, 1, 3 * D), lambda i, b: (b[i], 0, 0))
    tspec = pl.BlockSpec((NE, TM), lambda i, b: (0, i))
    rec = pl.BlockSpec((NLG, TM, 128), lambda i, b: (0, i, 0))
    grid_spec = pltpu.PrefetchScalarGridSpec(
        num_scalar_prefetch=1, grid=(nt,),
        in_specs=x_specs + [
            tok(AW), tok(RW), rec, rec, tok(SGW), tok(SGW), modspec, modspec,
            full(gmix), full(sgw), full(sgb), full(wout), full(lng), full(lnb), full(rwt), full(rb)],
        out_specs=[tok(D), tok(D), tspec, tspec, pl.BlockSpec((None, NE, 128), lambda i, b: (i, 0, 0))],
        scratch_shapes=[pltpu.VMEM((TM, D), BF16)])
    return pl.pallas_call(
        functools.partial(_merge_kernel, split), grid_spec=grid_spec,
        out_shape=[jax.ShapeDtypeStruct((n, D), F32), jax.ShapeDtypeStruct((n, D), BF16),
                   jax.ShapeDtypeStruct((NE, n), F32), jax.ShapeDtypeStruct((NE, n), F32),
                   jax.ShapeDtypeStruct((nt, NE, 128), F32)],
        compiler_params=pltpu.CompilerParams(dimension_semantics=("parallel",), vmem_limit_bytes=VMEM_LIMIT),
        name="merge_route",
    )(meta["bidx"], *x_parts, oa, gy, hf, hb, u, vn, mod1, mod2, gmix, sgw, sgb, wout, lng, lnb, rwt, rb)


def _plan(cnt, n_mt):
    pc = (cnt + (ALIGN - 1)) // ALIGN * ALIGN
    lo = jnp.cumsum(pc, axis=1) - pc
    tot = jnp.sum(pc, axis=0)
    seg = (tot + (TMX - 1)) // TMX * TMX
    gend = jnp.cumsum(seg)
    gstart = gend - seg
    dst = gstart[None, :] + jnp.cumsum(pc, axis=0) - pc
    tile_row = jnp.arange(n_mt, dtype=jnp.int32) * TMX
    texp = jnp.minimum(jnp.sum((gend[None, :] <= tile_row[:, None]).astype(jnp.int32), axis=1), NE - 1)
    nact = (gend[-1] // TMX).reshape(1)
    grp = lambda a: (a // ALIGN).astype(jnp.int32)
    return dict(pc=grp(pc).reshape(-1), lo=grp(lo).reshape(-1), dst=grp(dst).reshape(-1),
                tsum=grp(jnp.sum(pc, axis=1)),
                zdst=grp(jnp.concatenate([gstart + tot, gend[-1:]])), zlen=grp(seg - tot),
                texp=texp.astype(jnp.int32), nact=nact.astype(jnp.int32))


def _run_copies(src, dst, src_off, dst_off, groups, sem, max_bits):
    for b in range(max_bits):
        @pl.when(((groups >> b) & 1) == 1)
        def _():
            off = (groups >> (b + 1)) << (b + 1)
            pltpu.make_async_copy(src.at[pl.ds(src_off + off, 1 << b)], dst.at[pl.ds(dst_off + off, 1 << b)],
                                  sem).start()


def _wait_groups(buf, groups, sem, max_bits):
    for b in range(max_bits):
        @pl.when(((groups >> b) & 1) == 1)
        def _():
            pltpu.make_async_copy(buf.at[pl.ds(0, 1 << b)], buf.at[pl.ds(0, 1 << b)], sem).wait()


RUN_BITS = (TD // ALIGN).bit_length()
TILE_BITS = (RL // ALIGN).bit_length()


def _dispatch_kernel(pc_ref, lo_ref, dst_ref, ts_ref, zd_ref, zl_ref, mask_ref, gate_ref, h_ref,
                     xs_ref, info_ref, loc_s, zero_s, sems):
    i = pl.program_id(0)
    last_step = pl.num_programs(0) - 1
    slot = i % 2
    mask = mask_ref[...]
    mb = mask.astype(BF16)
    s_i = lax.broadcasted_iota(jnp.int32, (TD, TD), 0)
    t_i = lax.broadcasted_iota(jnp.int32, (TD, TD), 1)
    earlier = jnp.where(s_i < t_i, 1.0, 0.0).astype(BF16)
    rank = jnp.dot(mb, earlier, preferred_element_type=F32)
    e_i = lax.broadcasted_iota(jnp.int32, (NE, NE), 0)
    f_i = lax.broadcasted_iota(jnp.int32, (NE, NE), 1)
    below = jnp.where(f_i < e_i, 1.0, 0.0).astype(BF16)
    lower = jnp.dot(below, mb, preferred_element_type=F32)
    row_e = lax.broadcasted_iota(jnp.int32, (NE, 1), 0)
    lo_vec = jnp.zeros((NE, 1), F32)
    for e in range(NE):
        lo_vec = jnp.where(row_e == e, (lo_ref[i * NE + e] * ALIGN).astype(F32), lo_vec)
    row = lo_vec + rank
    is0 = mask * jnp.where(lower == 0.0, 1.0, 0.0)
    is1 = mask - is0
    d0 = jnp.sum(is0 * row, axis=0, keepdims=True)
    d1 = jnp.sum(is1 * row, axis=0, keepdims=True)
    gate = gate_ref[...]
    w0 = jnp.sum(is0 * gate, axis=0, keepdims=True)
    w1 = jnp.sum(is1 * gate, axis=0, keepdims=True)
    info_ref[...] = jnp.concatenate([d0, d1, w0, w1, jnp.zeros((4, TD), F32)], axis=0)
    r_i = lax.broadcasted_iota(jnp.int32, (RL, TD), 0)
    perm = jnp.where((r_i == d0.astype(jnp.int32)) | (r_i == d1.astype(jnp.int32)), 1.0, 0.0).astype(BF16)
    loc = loc_s.at[slot]
    loc[...] = jnp.dot(perm, h_ref[...], preferred_element_type=F32).astype(BF16).reshape(RL // ALIGN, ALIGN, D)
    for e in range(NE):
        _run_copies(loc, xs_ref, lo_ref[i * NE + e], dst_ref[i * NE + e], pc_ref[i * NE + e], sems.at[slot],
                    RUN_BITS)

    @pl.when(i > 0)
    def _():
        _wait_groups(loc_s.at[1 - slot], ts_ref[jnp.maximum(i - 1, 0)], sems.at[1 - slot], TILE_BITS)

    @pl.when(i == last_step)
    def _():
        _wait_groups(loc, ts_ref[i], sems.at[slot], TILE_BITS)
        sem = sems.at[0]
        zero_s[...] = jnp.zeros_like(zero_s)
        zg = ZROWS // ALIGN
        for wait in (False, True):
            for e in range(NE):
                for part in range(TMX // ZROWS):
                    groups = jnp.clip(zl_ref[e] - part * zg, 0, zg)
                    if wait:
                        _wait_groups(zero_s, groups, sem, zg.bit_length())
                    else:
                        _run_copies(zero_s, xs_ref, 0, zd_ref[e] + part * zg, groups, sem, zg.bit_length())
        tail = zd_ref[NE]
        chunks = (xs_ref.shape[0] - tail) // zg

        def fill(c, carry):
            pltpu.make_async_copy(zero_s, xs_ref.at[pl.ds(tail + c * zg, zg)], sem).start()
            return carry

        def drain(c, carry):
            pltpu.make_async_copy(zero_s, zero_s, sem).wait()
            return carry

        lax.fori_loop(0, chunks, fill, 0)
        lax.fori_loop(0, chunks, drain, 0)


def _dispatch(plan, mask_t, gate_t, h2, rtot):
    n = h2.shape[0]
    ntd = n // TD
    tspec = pl.BlockSpec((NE, TD), lambda i, *_: (0, i))
    grid_spec = pltpu.PrefetchScalarGridSpec(
        num_scalar_prefetch=6, grid=(ntd,),
        in_specs=[tspec, tspec, pl.BlockSpec((TD, D), lambda i, *_: (i, 0))],
        out_specs=[pl.BlockSpec(memory_space=pl.ANY), pl.BlockSpec((8, TD), lambda i, *_: (0, i))],
        scratch_shapes=[pltpu.VMEM((2, RL // ALIGN, ALIGN, D), BF16), pltpu.VMEM((ZROWS // ALIGN, ALIGN, D), BF16),
                        pltpu.SemaphoreType.DMA((2,))])
    return pl.pallas_call(
        _dispatch_kernel, grid_spec=grid_spec,
        out_shape=[jax.ShapeDtypeStruct((rtot // ALIGN, ALIGN, D), BF16), jax.ShapeDtypeStruct((8, n), F32)],
        compiler_params=pltpu.CompilerParams(dimension_semantics=("arbitrary",), vmem_limit_bytes=VMEM_LIMIT),
        name="dispatch",
    )(plan["pc"], plan["lo"], plan["dst"], plan["tsum"], plan["zdst"], plan["zlen"], mask_t, gate_t, h2)


def _expert_kernel(texp_ref, nact_ref, x_ref, w1_ref, w3_ref, w2_ref, y_ref, w1_s, w3_s, w2_s):
    m = pl.program_id(0)
    new_expert = (m == 0) | (texp_ref[m] != texp_ref[jnp.maximum(m - 1, 0)])

    @pl.when(new_expert)
    def _():
        w1_s[...] = w1_ref[...].astype(BF16)
        w3_s[...] = w3_ref[...].astype(BF16)
        w2_s[...] = w2_ref[...].astype(BF16)

    @pl.when(m < nact_ref[0])
    def _():
        x = x_ref[...]
        a = jnp.dot(x, w1_s[...], preferred_element_type=F32)
        a = a * _sigmoid(a) * jnp.dot(x, w3_s[...], preferred_element_type=F32)
        y_ref[...] = jnp.dot(a.astype(BF16), w2_s[...], preferred_element_type=F32).astype(BF16)

    @pl.when(m >= nact_ref[0])
    def _():
        y_ref[...] = jnp.zeros_like(y_ref)


def _experts(plan, xs, w1, w3, w2):
    rtot = xs.shape[0]
    n_mt = rtot // TMX
    grid_spec = pltpu.PrefetchScalarGridSpec(
        num_scalar_prefetch=2, grid=(n_mt,),
        in_specs=[pl.BlockSpec((TMX, D), lambda m, te, na: (jnp.minimum(m, na[0] - 1), 0)),
                  pl.BlockSpec((None, D, FF), lambda m, te, na: (te[m], 0, 0)),
                  pl.BlockSpec((None, D, FF), lambda m, te, na: (te[m], 0, 0)),
                  pl.BlockSpec((None, FF, D), lambda m, te, na: (te[m], 0, 0))],
        out_specs=pl.BlockSpec((TMX, D), lambda m, te, na: (m, 0)),
        scratch_shapes=[pltpu.VMEM((D, FF), BF16), pltpu.VMEM((D, FF), BF16), pltpu.VMEM((FF, D), BF16)])
    return pl.pallas_call(
        _expert_kernel, grid_spec=grid_spec,
        out_shape=jax.ShapeDtypeStruct((rtot, D), BF16),
        compiler_params=pltpu.CompilerParams(dimension_semantics=("arbitrary",), vmem_limit_bytes=VMEM_LIMIT),
        name="experts",
    )(plan["texp"], plan["nact"], xs, w1, w3, w2)


def _combine_kernel(split, bidx_ref, pc_ref, lo_ref, dst_ref, ts_ref, info_ref, ys_ref, x1_ref, mod_ref,
                    lng_ref, lnb_ref, *rest):
    del bidx_ref
    o_refs, (loc_s, sems) = rest[:-2], rest[-2:]
    i = pl.program_id(0)
    slot = i % 2

    def fetch(tile, buf):
        for e in range(NE):
            _run_copies(ys_ref, loc_s.at[buf], dst_ref[tile * NE + e], lo_ref[tile * NE + e],
                        pc_ref[tile * NE + e], sems.at[buf], RUN_BITS)

    @pl.when(i == 0)
    def _():
        loc_s[...] = jnp.zeros_like(loc_s)
        fetch(0, 0)

    @pl.when(i + 1 < pl.num_programs(0))
    def _():
        fetch(i + 1, 1 - slot)

    _wait_groups(loc_s.at[slot], ts_ref[i], sems.at[slot], TILE_BITS)
    info = info_ref[...]
    d0 = info[0:1].astype(jnp.int32)
    d1 = info[1:2].astype(jnp.int32)
    r_i = lax.broadcasted_iota(jnp.int32, (RL, TD), 0)
    wperm = (jnp.where(r_i == d0, info[2:3], 0.0) + jnp.where(r_i == d1, info[3:4], 0.0)).astype(BF16)
    moe = lax.dot_general(wperm, loc_s[slot].reshape(RL, D), (((0,), (0,)), ((), ())),
                          preferred_element_type=F32)
    gate = mod_ref[...][:, 2 * D:]
    out = _ln(ALPHA * x1_ref[...] + gate * moe) * lng_ref[...] + lnb_ref[...]
    if len(o_refs) == 1:
        o_refs[0][...] = out
    else:
        @pl.when(i < split)
        def _():
            o_refs[0][...] = out

        @pl.when(i >= split)
        def _():
            o_refs[1][...] = out


def _combine(bidx_d, plan, info, ys, x1, mod2, lng, lnb, out_rows):
    n = x1.shape[0]
    ntd = n // TD
    tok = pl.BlockSpec((TD, D), lambda i, *_: (i, 0))
    full = lambda a: pl.BlockSpec(a.shape, lambda i, *_: (0,) * a.ndim)
    if len(out_rows) == 1:
        split, out_specs = 0, [tok]
    else:
        split = out_rows[0] // TD
        out_specs = [pl.BlockSpec((TD, D), lambda i, *_: (jnp.minimum(i, split - 1), 0)),
                     pl.BlockSpec((TD, D), lambda i, *_: (jnp.maximum(i - split, 0), 0))]
    grid_spec = pltpu.PrefetchScalarGridSpec(
        num_scalar_prefetch=5, grid=(ntd,),
        in_specs=[pl.BlockSpec((8, TD), lambda i, *_: (0, i)),
                  pl.BlockSpec(memory_space=pl.ANY),
                  tok,
                  pl.BlockSpec((None, 1, 3 * D), lambda i, b, *_: (b[i], 0, 0)),
                  full(lng), full(lnb)],
        out_specs=out_specs,
        scratch_shapes=[pltpu.VMEM((2, RL // ALIGN, ALIGN, D), BF16), pltpu.SemaphoreType.DMA((2,))])
    return pl.pallas_call(
        functools.partial(_combine_kernel, split), grid_spec=grid_spec,
        out_shape=[jax.ShapeDtypeStruct((r, D), F32) for r in out_rows],
        compiler_params=pltpu.CompilerParams(dimension_semantics=("arbitrary",), vmem_limit_bytes=VMEM_LIMIT),
        name="combine",
    )(bidx_d, plan["pc"], plan["lo"], plan["dst"], plan["tsum"], info, ys, x1, mod2, lng, lnb)


def _moe(meta, h2, mask_t, gate_t, cnt, w1, w3, w2, x1, mod2, lng, lnb, out_rows):
    n = h2.shape[0]
    ntd = n // TD
    rtot = ntd * RL + NE * TMX
    cnt = cnt[:, :, :TM // TD].transpose(0, 2, 1).reshape(ntd, NE).astype(jnp.int32)
    plan = _plan(cnt, rtot // TMX)
    xs, info = _dispatch(plan, mask_t, gate_t, h2, rtot)
    ys = _experts(plan, xs.reshape(rtot, D), w1, w3, w2).reshape(rtot // ALIGN, ALIGN, D)
    return _combine(meta["bidx_d"], plan, info, ys, x1, mod2, lng, lnb, out_rows)


def _tile_meta(groups):
    bidx, pblk, first, last, cf, cb, bidx_d = [], [], [], [], [], [], []
    row = 0
    tile = 0
    for (b, s) in groups:
        per = s // TM
        for bi in range(b):
            for j in range(per):
                bidx.append(row + bi)
                pblk.append(j)
                first.append(1 if j == 0 else 0)
                last.append(1 if j == per - 1 else 0)
                cf.append(tile + j)
                cb.append(tile + per - 1 - j)
            tile += per
            bidx_d += [row + bi] * (s // TD)
        row += b
    as_i32 = lambda v: jnp.asarray(np.asarray(v, np.int32))
    return dict(bidx=as_i32(bidx), pblk=as_i32(pblk), first=as_i32(first), last=as_i32(last),
                cf=as_i32(cf), cb=as_i32(cb), bidx_d=as_i32(bidx_d))


def _rope_tables(s_max):
    inv = jnp.power(ROPE_THETA, -jnp.arange(0, HD, 2, dtype=F32) / HD)
    ang = jnp.arange(s_max, dtype=F32)[:, None] * inv[None, :]
    cos, sin = jnp.cos(ang), jnp.sin(ang)
    cos128 = jnp.tile(cos, (1, 4))
    sin128 = jnp.tile(jnp.concatenate([-sin, sin], axis=1), (1, 2))
    return cos128, sin128


def _block_diag(w):
    d, nb, c, f = w.shape
    eye = jnp.eye(nb, dtype=w.dtype)
    return jnp.einsum('dncf,nm->dncmf', w, eye).reshape(d, nb * c, nb * f)


def _forward(xs, cs, w_mod, b_mod, w_in, attn_sink, conv_w, conv_b, lru_w_r, lru_b_r, lru_w_i, lru_b_i,
             lru_lambda, sg_norm_g, sg_w, sg_b, mix_norm_g, w_out, ln_g, ln_b, router_w, router_bias,
             exp_w1, exp_w3, exp_w2):
    groups = [(x.shape[0], x.shape[1]) for x in xs]
    assert len(groups) == 2
    for (_, s) in groups:
        assert s % TM == 0
    meta = _tile_meta(groups)
    x_parts = [xx.reshape(-1, D) for xx in xs]
    group_rows = [p.shape[0] for p in x_parts]
    c_all = jnp.concatenate(cs, axis=0)
    bt = c_all.shape[0]
    mods = _modulation(c_all, w_mod, b_mod).reshape(2 * DEPTH, bt, 1, 3 * D)
    cos_t, sin_t = _rope_tables(max(s for _, s in groups))
    rwt = router_w.T.astype(BF16)
    rb = router_bias.reshape(NE, 1)
    for l in range(DEPTH):
        wi = w_in[l]
        kcols = wi[:, AW:AW + KVW]
        vcols = wi[:, AW + KVW:AW + 2 * KVW]
        dup = lambda m: jnp.concatenate([m[:, :HD], m[:, :HD], m[:, HD:], m[:, HD:]], axis=1)
        w_ext = jnp.concatenate([wi[:, :AW], dup(kcols), dup(vcols), wi[:, AW + 2 * KVW:]], axis=1).astype(BF16)
        q, kk, vv, gy, xr, u, vn = _pre_mixer(meta, x_parts, mods[2 * l], w_ext, cos_t, sin_t,
                                              sg_norm_g[l].reshape(1, SGW))
        oa = _attention(meta, attn_sink[l], q, kk, vv)
        hf, hb = _recurrent(meta, xr, conv_w[l], conv_b[l].reshape(1, RW),
                            _block_diag(lru_w_r[l]).astype(BF16), _block_diag(lru_w_i[l]).astype(BF16),
                            lru_b_r[l], lru_b_i[l], lru_lambda[l])
        sgb = jnp.repeat(sg_b[l].T, HD, axis=1)
        x1, h2, gate_t, mask_t, cnt = _merge(meta, x_parts, oa, gy, hf, hb, u, vn, mods[2 * l], mods[2 * l + 1],
                                             mix_norm_g[l].reshape(1, D), sg_w[l].astype(BF16), sgb,
                                             w_out[l].astype(BF16), ln_g[l, 0].reshape(1, D),
                                             ln_b[l, 0].reshape(1, D), rwt, rb)
        out_rows = group_rows if l == DEPTH - 1 else [sum(group_rows)]
        x_parts = _moe(meta, h2, mask_t, gate_t, cnt, exp_w1[l], exp_w3[l], exp_w2[l], x1, mods[2 * l + 1],
                       ln_g[l, 1].reshape(1, D), ln_b[l, 1].reshape(1, D), out_rows)
    return tuple(p.reshape(b, s, D) for p, (b, s) in zip(x_parts, groups))


def kernel(x_prompt, x_sample, c_prompt, c_sample, w_mod, b_mod, w_in, attn_sink, conv_w, conv_b, lru_w_r, lru_b_r,
           lru_w_i, lru_b_i, lru_lambda, sg_norm_g, sg_w, sg_b, mix_norm_g, w_out, ln_g, ln_b, router_w,
           router_bias, exp_w1, exp_w3, exp_w2):
    return _forward([x_prompt, x_sample], [c_prompt, c_sample], w_mod, b_mod, w_in, attn_sink, conv_w, conv_b,
                    lru_w_r, lru_b_r, lru_w_i, lru_b_i, lru_lambda, sg_norm_g, sg_w, sg_b, mix_norm_g, w_out,
                    ln_g, ln_b, router_w, router_bias, exp_w1, exp_w3, exp_w2)
```

```python
import functools

import numpy as np
import jax
import jax.numpy as jnp
from jax import lax
from jax.experimental import pallas as pl
from jax.experimental.pallas import tpu as pltpu

F32 = jnp.float32
BF16 = jnp.bfloat16

D = 1024
DEPTH = 2
HD = 64
AW = 384
KVW = 128
RW = 384
SGW = 256
WINDOW = 128
CONV_WIDTH = 4
LRU_C = 8.0
CHUNK = 128
NE = 16
NG = 4
EPG = NE // NG
FF = 512
ALPHA = (2 * DEPTH) ** 0.25
LN_EPS = 1e-5
RMS_EPS = 1e-6
ROPE_THETA = 10000.0

TM = 512
QB = 128
SEG = TM // 8
NLG = RW // 128
TD = 256
ALIGN = 16
RL = 2 * TD + NE * ALIGN
TMX = 1024
ZROWS = 256
NEG = -1e30
VMEM_LIMIT = 48 * 1024 * 1024

C_Q, C_K, C_V, C_Y, C_R, C_U, C_SV, C_END = 0, 384, 512, 640, 1024, 1408, 1664, 1920


def _ln(x):
    mu = jnp.mean(x, axis=-1, keepdims=True)
    xc = x - mu
    var = jnp.mean(xc * xc, axis=-1, keepdims=True)
    return xc * lax.rsqrt(var + LN_EPS)


def _rms(x):
    return x * lax.rsqrt(jnp.mean(x * x, axis=-1, keepdims=True) + RMS_EPS)


def _sigmoid(z):
    return 0.5 * (jnp.tanh(0.5 * z) + 1.0)


def _gelu(x):
    return 0.5 * x * (1.0 + jnp.tanh(0.7978845608028654 * (x + 0.044715 * (x * x * x))))


def _split_bf16(a):
    hi = a.astype(BF16)
    lo = (a - hi.astype(F32)).astype(BF16)
    return hi, lo


def _mod_kernel(c_ref, w_ref, b_ref, o_ref):
    c = c_ref[...]
    ca = c * _sigmoid(c)
    ch, cl = _split_bf16(ca)
    wh, wl = _split_bf16(w_ref[...])
    acc = jnp.dot(ch, wh, preferred_element_type=F32)
    acc += jnp.dot(ch, wl, preferred_element_type=F32)
    acc += jnp.dot(cl, wh, preferred_element_type=F32)
    o_ref[...] = acc + b_ref[...]


def _modulation(c_all, w_mod, b_mod):
    bt = c_all.shape[0]
    cb = 768
    w = w_mod.reshape(2 * DEPTH, D, 3 * D)
    b = b_mod.reshape(2 * DEPTH, 1, 3 * D)
    return pl.pallas_call(
        _mod_kernel,
        grid=(2 * DEPTH, 3 * D // cb),
        in_specs=[pl.BlockSpec((bt, D), lambda s, j: (0, 0)),
                  pl.BlockSpec((None, D, cb), lambda s, j: (s, 0, j)),
                  pl.BlockSpec((None, 1, cb), lambda s, j: (s, 0, j))],
        out_specs=pl.BlockSpec((None, bt, cb), lambda s, j: (s, 0, j)),
        out_shape=jax.ShapeDtypeStruct((2 * DEPTH, bt, 3 * D), F32),
        name="modulation",
    )(c_all, w, b)


def _load_tokens(i, x_refs, split):
    if len(x_refs) == 1:
        return x_refs[0][...]
    return jnp.where(i < split, x_refs[0][...], x_refs[1][...])


def _token_specs(x_parts, rows):
    if len(x_parts) == 1:
        return [pl.BlockSpec((rows, D), lambda i, *_: (i, 0))], 0
    split = x_parts[0].shape[0] // rows
    return [pl.BlockSpec((rows, D), lambda i, *_: (jnp.minimum(i, split - 1), 0)),
            pl.BlockSpec((rows, D), lambda i, *_: (jnp.maximum(i - split, 0), 0))], split


def _pre_kernel(split, bidx_ref, pblk_ref, *refs):
    del bidx_ref, pblk_ref
    nx = len(refs) - 12
    x_refs = refs[:nx]
    mod_ref, w_ref, cos_ref, sin_ref, sgg_ref, q_ref, kk_ref, vv_ref, gy_ref, xr_ref, u_ref, vn_ref = refs[nx:]
    mod = mod_ref[...]
    shift, scale = mod[:, :D], mod[:, D:2 * D]
    h = (_ln(_load_tokens(pl.program_id(0), x_refs, split)) * (1.0 + scale) + shift).astype(BF16)
    cos = cos_ref[...]
    sin = sin_ref[...]
    lane = lax.broadcasted_iota(jnp.int32, (1, 128), 1)
    first_half = (lane % HD) < (HD // 2)

    def rope128(xg):
        rot = jnp.where(first_half, pltpu.roll(xg, 128 - HD // 2, 1), pltpu.roll(xg, HD // 2, 1))
        return xg * cos + rot * sin

    q = jnp.dot(h, w_ref[:, C_Q:C_K], preferred_element_type=F32)
    for g in range(AW // 128):
        q_ref[:, g * 128:(g + 1) * 128] = (rope128(q[:, g * 128:(g + 1) * 128]) * (HD ** -0.5)).astype(BF16)
    kv = jnp.dot(h, w_ref[:, C_K:C_Y], preferred_element_type=F32)
    lo_half = lane < HD
    for src, dst_ref in ((rope128(kv[:, :KVW]), kk_ref), (kv[:, KVW:], vv_ref)):
        swapped = pltpu.roll(src, HD, 1)
        dst_ref[:, 0:128] = jnp.where(lo_half, src, swapped).astype(BF16)
        dst_ref[:, 128:256] = jnp.where(lo_half, swapped, src).astype(BF16)
    gy_ref[...] = _gelu(jnp.dot(h, w_ref[:, C_Y:C_R], preferred_element_type=F32)).astype(gy_ref.dtype)
    xr = jnp.dot(h, w_ref[:, C_R:C_U], preferred_element_type=F32)
    for g in range(NLG):
        xr_ref[g] = xr[:, g * 128:(g + 1) * 128]
    u_ref[...] = _gelu(jnp.dot(h, w_ref[:, C_U:C_SV], preferred_element_type=F32)).astype(u_ref.dtype)
    sv = _gelu(jnp.dot(h, w_ref[:, C_SV:C_END], preferred_element_type=F32))
    vn_ref[...] = (_ln(sv) * sgg_ref[...]).astype(BF16)


def _pre_mixer(meta, x_parts, mod_l, w_ext, cos_t, sin_t, sgg):
    n = sum(p.shape[0] for p in x_parts)
    nt = n // TM
    tok = lambda w: pl.BlockSpec((TM, w), lambda i, b, p: (i, 0))
    full = lambda a: pl.BlockSpec(a.shape, lambda i, b, p: (0,) * a.ndim)
    x_specs, split = _token_specs(x_parts, TM)
    grid_spec = pltpu.PrefetchScalarGridSpec(
        num_scalar_prefetch=2, grid=(nt,),
        in_specs=x_specs + [
            pl.BlockSpec((None, 1, 3 * D), lambda i, b, p: (b[i], 0, 0)),
            full(w_ext),
            pl.BlockSpec((TM, 128), lambda i, b, p: (p[i], 0)),
            pl.BlockSpec((TM, 128), lambda i, b, p: (p[i], 0)),
            full(sgg)],
        out_specs=[tok(AW), tok(256), tok(256), tok(RW),
                   pl.BlockSpec((NLG, TM, 128), lambda i, b, p: (0, i, 0)), tok(SGW), tok(SGW)])
    sds = lambda w, dt: jax.ShapeDtypeStruct((n, w), dt)
    return pl.pallas_call(
        functools.partial(_pre_kernel, split), grid_spec=grid_spec,
        out_shape=[sds(AW, BF16), sds(256, BF16), sds(256, BF16), sds(RW, BF16),
                   jax.ShapeDtypeStruct((NLG, n, 128), F32), sds(SGW, BF16), sds(SGW, BF16)],
        compiler_params=pltpu.CompilerParams(dimension_semantics=("parallel",), vmem_limit_bytes=VMEM_LIMIT),
        name="pre_mixer",
    )(meta["bidx"], meta["pblk"], *x_parts, mod_l, w_ext, cos_t, sin_t, sgg)


def _attn_kernel(first_ref, last_ref, sink_ref, q_ref, k_ref, kp_ref, kn_ref, v_ref, vp_ref, vn_ref,
                 o_ref, kw_s, vw_s):
    i = pl.program_id(0)
    is_first = first_ref[i] == 1
    is_last = last_ref[i] == 1
    kw_s[0:QB] = kp_ref[...]
    kw_s[QB:QB + TM] = k_ref[...]
    kw_s[QB + TM:] = kn_ref[...]
    vw_s[0:QB] = vp_ref[...]
    vw_s[QB:QB + TM] = v_ref[...]
    vw_s[QB + TM:] = vn_ref[...]
    iq = lax.broadcasted_iota(jnp.int32, (QB, 3 * QB), 0)
    ik = lax.broadcasted_iota(jnp.int32, (QB, 3 * QB), 1)
    rel = ik - iq
    band = (rel >= 0) & (rel <= 2 * WINDOW)
    lane = lax.broadcasted_iota(jnp.int32, (1, 128), 1)
    lo_half = lane < HD
    zero = jnp.zeros((), BF16)
    gsz = AW // HD // 2
    row_head = lax.broadcasted_iota(jnp.int32, (gsz * QB, 1), 0) // QB
    for j in range(TM // QB):
        ok = band
        if j == 0:
            ok = ok & ((ik >= QB) | jnp.logical_not(is_first))
        if j == TM // QB - 1:
            ok = ok & ((ik < 2 * QB) | jnp.logical_not(is_last))
        bias = jnp.where(ok, 0.0, NEG)
        bias = jnp.concatenate([bias] * gsz, axis=0)
        rows = slice(j * QB, (j + 1) * QB)
        keys = slice(j * QB, j * QB + 3 * QB)
        outs = [None] * (AW // HD)
        for g in range(2):
            heads = range(g * gsz, (g + 1) * gsz)
            qs = []
            for hh in heads:
                qg = q_ref[rows, (hh // 2) * 128:(hh // 2 + 1) * 128]
                qs.append(jnp.where(lo_half if hh % 2 == 0 else jnp.logical_not(lo_half), qg, zero))
            s = lax.dot_general(jnp.concatenate(qs, axis=0), kw_s[keys, g * 128:(g + 1) * 128],
                                (((1,), (1,)), ((), ())), preferred_element_type=F32) + bias
            sink = jnp.full((gsz * QB, 1), sink_ref[heads[-1]], F32)
            for t in range(gsz - 1):
                sink = jnp.where(row_head == t, sink_ref[heads[t]], sink)
            m = jnp.maximum(jnp.max(s, axis=-1, keepdims=True), sink)
            p = jnp.exp(s - m)
            denom = jnp.sum(p, axis=-1, keepdims=True) + jnp.exp(sink - m)
            pv = jnp.dot(p.astype(BF16), vw_s[keys, g * 128:(g + 1) * 128], preferred_element_type=F32)
            o = pv * (1.0 / denom)
            for t, hh in enumerate(heads):
                outs[hh] = o[t * QB:(t + 1) * QB]
        for jg in range(AW // 128):
            o_ref[rows, jg * 128:(jg + 1) * 128] = jnp.where(lo_half, outs[2 * jg], outs[2 * jg + 1]).astype(o_ref.dtype)


def _attention(meta, sink, q, kk, vv):
    n = q.shape[0]
    nt = n // TM
    nqb = n // QB
    r = TM // QB
    main = lambda w: pl.BlockSpec((TM, w), lambda i, f, l, s: (i, 0))
    prev = pl.BlockSpec((QB, 256), lambda i, f, l, s: (jnp.maximum(i * r - 1, 0), 0))
    nxt = pl.BlockSpec((QB, 256), lambda i, f, l, s: (jnp.minimum(i * r + r, nqb - 1), 0))
    grid_spec = pltpu.PrefetchScalarGridSpec(
        num_scalar_prefetch=3, grid=(nt,),
        in_specs=[main(AW), main(256), prev, nxt, main(256), prev, nxt],
        out_specs=main(AW),
        scratch_shapes=[pltpu.VMEM((TM + 2 * QB, 256), BF16), pltpu.VMEM((TM + 2 * QB, 256), BF16)])
    return pl.pallas_call(
        _attn_kernel, grid_spec=grid_spec,
        out_shape=jax.ShapeDtypeStruct((n, AW), BF16),
        compiler_params=pltpu.CompilerParams(dimension_semantics=("parallel",), vmem_limit_bytes=VMEM_LIMIT),
        name="attention",
    )(meta["first"], meta["last"], sink, q, kk, kk, kk, vv, vv, vv)


def _rec_kernel(cf_ref, cb_ref, first_ref, last_ref,
                xf_ref, xfp_ref, xfn_ref, xb_ref, xbp_ref, xbn_ref,
                cw_ref, cbias_ref, wr_ref, wi_ref, br_ref, bi_ref, lam_ref,
                hf_ref, hb_ref,
                ext_s, a_s, b_s, hs_s, ps_s, carry_s):
    i = pl.program_id(0)
    cw = cw_ref[...]
    cbias = cbias_ref[...]
    sub = lax.broadcasted_iota(jnp.int32, (8, 128), 0)
    unroll = 8

    def run(d, c, x_ref, xp_ref, xn_ref, out_ref):
        first = first_ref[c] == 1
        last = last_ref[c] == 1

        def gather(jo, carry):
            for ji in range(unroll):
                j = jo * unroll + ji
                for g in range(NLG):
                    ext_s[g, pl.ds(pl.multiple_of(16 + j * 8, 8), 8), :] = x_ref[g, pl.ds(j, 8, stride=SEG), :]
            return carry

        lax.fori_loop(0, SEG // unroll, gather, 0)
        planes = []
        for g in range(NLG):
            prev = jnp.where(first, 0.0, xp_ref[g])
            nxt = jnp.where(last, 0.0, xn_ref[g])
            tail1 = ext_s[g, 16 + (SEG - 1) * 8:16 + SEG * 8, :]
            tail2 = ext_s[g, 16 + (SEG - 2) * 8:16 + (SEG - 1) * 8, :]
            head = ext_s[g, 16:24, :]
            ext_s[g, 0:8, :] = jnp.where(sub == 0, prev[6:7], pltpu.roll(tail2, 1, 0))
            ext_s[g, 8:16, :] = jnp.where(sub == 0, prev[7:8], pltpu.roll(tail1, 1, 0))
            ext_s[g, 16 + TM:24 + TM, :] = jnp.where(sub == 7, nxt[0:1], pltpu.roll(head, 7, 0))
            lanes = slice(g * 128, (g + 1) * 128)
            acc = ext_s[g, 0:TM, :] * cw[0:1, lanes]
            for t in range(1, CONV_WIDTH):
                acc = acc + ext_s[g, 8 * t:8 * t + TM, :] * cw[t:t + 1, lanes]
            planes.append(acc)
        xc = jnp.concatenate(planes, axis=1) + cbias
        xb16 = xc.astype(BF16)
        tr = jnp.tanh(jnp.dot(xb16, wr_ref[d], preferred_element_type=F32) + br_ref[d:d + 1])
        ti = jnp.tanh(jnp.dot(xb16, wi_ref[d], preferred_element_type=F32) + bi_ref[d:d + 1])
        nlam = -lam_ref[d:d + 1]
        softplus = jnp.maximum(nlam, 0.0) + jnp.log1p(jnp.exp(-jnp.abs(nlam)))
        half_c = (-0.5 * LRU_C) * softplus
        log_a = half_c * tr + half_c
        a = jnp.exp(log_a)
        b = jnp.sqrt((1.0 + a * a) * jnp.tanh(-log_a)) * (ti + 1.0) * (0.5 * xc)
        for g in range(NLG):
            a_s[g] = a[:, g * 128:(g + 1) * 128]
            b_s[g] = b[:, g * 128:(g + 1) * 128]

        reverse = d == 1
        reset = last if reverse else first

        @pl.when(reset)
        def _():
            carry_s[d] = jnp.zeros((NLG, 128), F32)

        def step(jo, hp):
            hp = list(hp)
            for ji in range(unroll):
                jj = jo * unroll + ji
                j = SEG - 1 - jj if reverse else jj
                r = pl.ds(pl.multiple_of(j * 8, 8), 8)
                for g in range(NLG):
                    h, p = hp[g]
                    ag = a_s[g, r, :]
                    h = ag * h + b_s[g, r, :]
                    p = p * ag
                    hs_s[g, r, :] = h
                    ps_s[g, r, :] = p
                    hp[g] = (h, p)
            return tuple(hp)

        init = tuple((jnp.zeros((8, 128), F32), jnp.ones((8, 128), F32)) for _ in range(NLG))
        ends = lax.fori_loop(0, SEG // unroll, step, init)
        cmats = []
        for g in range(NLG):
            e, pe = ends[g]
            c_in = carry_s[d, g:g + 1]
            rows = [None] * 8
            for s in (range(7, -1, -1) if reverse else range(8)):
                rows[s] = c_in
                c_in = e[s:s + 1] + pe[s:s + 1] * c_in
            carry_s[d, g:g + 1] = c_in
            cmats.append(jnp.concatenate(rows, axis=0))

        def fix(jo, carry):
            for ji in range(unroll):
                j = jo * unroll + ji
                r = pl.ds(pl.multiple_of(j * 8, 8), 8)
                for g in range(NLG):
                    out_ref[g, pl.ds(j, 8, stride=SEG), :] = hs_s[g, r, :] + ps_s[g, r, :] * cmats[g]
            return carry

        lax.fori_loop(0, SEG // unroll, fix, 0)

    run(0, cf_ref[i], xf_ref, xfp_ref, xfn_ref, hf_ref)
    run(1, cb_ref[i], xb_ref, xbp_ref, xbn_ref, hb_ref)


def _recurrent(meta, xr, cw, cbias, wr, wi, br, bi, lam):
    n = xr.shape[1]
    nt = n // TM
    n8 = n // 8
    r8 = TM // 8

    def specs(which):
        sel = (lambda cf, cb: cf) if which == 0 else (lambda cf, cb: cb)
        main = pl.BlockSpec((NLG, TM, 128), lambda i, cf, cb, f, l: (0, sel(cf, cb)[i], 0))
        prev = pl.BlockSpec((NLG, 8, 128),
                            lambda i, cf, cb, f, l: (0, jnp.maximum(sel(cf, cb)[i] * r8 - 1, 0), 0))
        nxt = pl.BlockSpec((NLG, 8, 128),
                           lambda i, cf, cb, f, l: (0, jnp.minimum(sel(cf, cb)[i] * r8 + r8, n8 - 1), 0))
        return main, prev, nxt

    full = lambda a: pl.BlockSpec(a.shape, lambda i, cf, cb, f, l: (0,) * a.ndim)
    mf, pf, nf = specs(0)
    mb, pb, nb = specs(1)
    grid_spec = pltpu.PrefetchScalarGridSpec(
        num_scalar_prefetch=4, grid=(nt,),
        in_specs=[mf, pf, nf, mb, pb, nb, full(cw), full(cbias), full(wr), full(wi), full(br), full(bi), full(lam)],
        out_specs=[pl.BlockSpec((NLG, TM, 128), lambda i, cf, cb, f, l: (0, cf[i], 0)),
                   pl.BlockSpec((NLG, TM, 128), lambda i, cf, cb, f, l: (0, cb[i], 0))],
        scratch_shapes=[pltpu.VMEM((NLG, TM + 24, 128), F32),
                        pltpu.VMEM((NLG, TM, 128), F32), pltpu.VMEM((NLG, TM, 128), F32),
                        pltpu.VMEM((NLG, TM, 128), F32), pltpu.VMEM((NLG, TM, 128), F32),
                        pltpu.VMEM((2, NLG, 128), F32)])
    return pl.pallas_call(
        _rec_kernel, grid_spec=grid_spec,
        out_shape=[jax.ShapeDtypeStruct((NLG, n, 128), F32), jax.ShapeDtypeStruct((NLG, n, 128), F32)],
        compiler_params=pltpu.CompilerParams(dimension_semantics=("arbitrary",), vmem_limit_bytes=VMEM_LIMIT),
        name="recurrent",
    )(meta["cf"], meta["cb"], meta["first"], meta["last"], xr, xr, xr, xr, xr, xr, cw, cbias, wr, wi, br, bi, lam)


def _route(sel, score):
    one = jnp.ones_like(sel[0])
    zero = jnp.zeros_like(sel[0])

    def before(vk, vj, k, j):
        return (vk > vj) | ((vk == vj) & (k < j)) if k < j else (vk > vj)

    in_top = []
    gscore = []
    for g in range(NG):
        ids = range(g * EPG, (g + 1) * EPG)
        gs = zero
        for j in ids:
            rank = zero
            for k in ids:
                if k != j:
                    rank = rank + jnp.where(before(sel[k], sel[j], k, j), one, zero)
            m = jnp.where(rank < 2.0, one, zero)
            in_top.append(m)
            gs = gs + m * sel[j]
        gscore.append(gs)
    mask = []
    for g in range(NG):
        worse = zero
        for k in range(NG):
            if k != g:
                worse = worse + jnp.where(before(gscore[k], gscore[g], k, g), one, zero)
        best = jnp.where(worse < 1.0, one, zero)
        for j in range(g * EPG, (g + 1) * EPG):
            mask.append(in_top[j] * best)
    total = zero
    for e in range(NE):
        total = total + mask[e] * score[e]
    inv = 1.0 / total
    gate = [mask[e] * score[e] * inv for e in range(NE)]
    return mask, gate


def _merge_kernel(split, bidx_ref, *refs):
    del bidx_ref
    nx = len(refs) - 22
    x_refs = refs[:nx]
    (oa_ref, gy_ref, hf_ref, hb_ref, u_ref, vn_ref, mod1_ref, mod2_ref, gmix_ref, sgw_ref, sgb_ref, wout_ref,
     lng_ref, lnb_ref, rwt_ref, rb_ref, x1_ref, h2_ref, gate_ref, mask_ref, cnt_ref, mrg_s) = refs[nx:]
    gmix = gmix_ref[...]
    mrg_s[:, 0:AW] = (_rms(oa_ref[...].astype(F32)) * gmix[:, 0:AW]).astype(BF16)
    hsum = jnp.concatenate([hf_ref[g] + hb_ref[g] for g in range(NLG)], axis=1)
    o_rec = gy_ref[...].astype(F32) * hsum
    mrg_s[:, AW:AW + RW] = (_rms(o_rec) * gmix[:, AW:AW + RW]).astype(BF16)
    lane = lax.broadcasted_iota(jnp.int32, (1, 128), 1)
    lo_half = lane < HD
    zero = jnp.zeros((), BF16)
    pieces = []
    for c in range(TM // CHUNK):
        rows = slice(c * CHUNK, (c + 1) * CHUNK)
        grp = []
        for g in range(SGW // 128):
            vg = vn_ref[rows, g * 128:(g + 1) * 128]
            mixed = jnp.dot(sgw_ref[2 * g], jnp.where(lo_half, vg, zero), preferred_element_type=F32)
            mixed += jnp.dot(sgw_ref[2 * g + 1], jnp.where(lo_half, zero, vg), preferred_element_type=F32)
            grp.append(mixed)
        mixed = jnp.concatenate(grp, axis=1) + sgb_ref[...]
        pieces.append(u_ref[rows, :].astype(F32) * mixed)
    o_sg = jnp.concatenate(pieces, axis=0)
    mrg_s[:, AW + RW:] = (_rms(o_sg) * gmix[:, AW + RW:]).astype(BF16)
    o = jnp.dot(mrg_s[...], wout_ref[...], preferred_element_type=F32)
    gate1 = mod1_ref[...][:, 2 * D:]
    x_in = _load_tokens(pl.program_id(0), x_refs, split)
    x1 = _ln(ALPHA * x_in + gate1 * o) * lng_ref[...] + lnb_ref[...]
    x1_ref[...] = x1
    mod2 = mod2_ref[...]
    h2 = (_ln(x1) * (1.0 + mod2[:, D:2 * D]) + mod2[:, :D]).astype(BF16)
    h2_ref[...] = h2
    logits = lax.dot_general(rwt_ref[...], h2, (((1,), (1,)), ((), ())), preferred_element_type=F32)
    score = _sigmoid(logits)
    sel = score + rb_ref[...]
    mask, gate = _route([sel[e:e + 1] for e in range(NE)], [score[e:e + 1] for e in range(NE)])
    gate_ref[...] = jnp.concatenate(gate, axis=0)
    mask_t = jnp.concatenate(mask, axis=0)
    mask_ref[...] = mask_t
    t_i = lax.broadcasted_iota(jnp.int32, (TM, 128), 0)
    j_i = lax.broadcasted_iota(jnp.int32, (TM, 128), 1)
    in_tile = jnp.where((t_i >= j_i * TD) & (t_i < (j_i + 1) * TD), 1.0, 0.0).astype(BF16)
    cnt_ref[...] = jnp.dot(mask_t.astype(BF16), in_tile, preferred_element_type=F32)


def _merge(meta, x_parts, oa, gy, hf, hb, u, vn, mod1, mod2, gmix, sgw, sgb, wout, lng, lnb, rwt, rb):
    n = oa.shape[0]
    nt = n // TM
    x_specs, split = _token_specs(x_parts, TM)
    tok = lambda w: pl.BlockSpec((TM, w), lambda i, b: (i, 0))
    full = lambda a: pl.BlockSpec(a.shape, lambda i, b: (0,) * a.ndim)
    modspec = pl.BlockSpec((None, 1, 3 * D), lambda i, b: (b[i], 0, 0))
    tspec = pl.BlockSpec((NE, TM), lambda i, b: (0, i))
    rec = pl.BlockSpec((NLG, TM, 128), lambda i, b: (0, i, 0))
    grid_spec = pltpu.PrefetchScalarGridSpec(
        num_scalar_prefetch=1, grid=(nt,),
        in_specs=x_specs + [
            tok(AW), tok(RW), rec, rec, tok(SGW), tok(SGW), modspec, modspec,
            full(gmix), full(sgw), full(sgb), full(wout), full(lng), full(lnb), full(rwt), full(rb)],
        out_specs=[tok(D), tok(D), tspec, tspec, pl.BlockSpec((None, NE, 128), lambda i, b: (i, 0, 0))],
        scratch_shapes=[pltpu.VMEM((TM, D), BF16)])
    return pl.pallas_call(
        functools.partial(_merge_kernel, split), grid_spec=grid_spec,
        out_shape=[jax.ShapeDtypeStruct((n, D), F32), jax.ShapeDtypeStruct((n, D), BF16),
                   jax.ShapeDtypeStruct((NE, n), F32), jax.ShapeDtypeStruct((NE, n), F32),
                   jax.ShapeDtypeStruct((nt, NE, 128), F32)],
        compiler_params=pltpu.CompilerParams(dimension_semantics=("parallel",), vmem_limit_bytes=VMEM_LIMIT),
        name="merge_route",
    )(meta["bidx"], *x_parts, oa, gy, hf, hb, u, vn, mod1, mod2, gmix, sgw, sgb, wout, lng, lnb, rwt, rb)


def _plan(cnt, n_mt):
    pc = (cnt + (ALIGN - 1)) // ALIGN * ALIGN
    lo = jnp.cumsum(pc, axis=1) - pc
    tot = jnp.sum(pc, axis=0)
    seg = (tot + (TMX - 1)) // TMX * TMX
    gend = jnp.cumsum(seg)
    gstart = gend - seg
    dst = gstart[None, :] + jnp.cumsum(pc, axis=0) - pc
    tile_row = jnp.arange(n_mt, dtype=jnp.int32) * TMX
    texp = jnp.minimum(jnp.sum((gend[None, :] <= tile_row[:, None]).astype(jnp.int32), axis=1), NE - 1)
    nact = (gend[-1] // TMX).reshape(1)
    grp = lambda a: (a // ALIGN).astype(jnp.int32)
    return dict(pc=grp(pc).reshape(-1), lo=grp(lo).reshape(-1), dst=grp(dst).reshape(-1),
                tsum=grp(jnp.sum(pc, axis=1)),
                zdst=grp(jnp.concatenate([gstart + tot, gend[-1:]])), zlen=grp(seg - tot),
                texp=texp.astype(jnp.int32), nact=nact.astype(jnp.int32))


def _run_copies(src, dst, src_off, dst_off, groups, sem, max_bits):
    def arm(b):
        @pl.when(((groups >> b) & 1) == 1)
        def _():
            off = (groups >> (b + 1)) << (b + 1)
            pltpu.make_async_copy(src.at[pl.ds(src_off + off, 1 << b)], dst.at[pl.ds(dst_off + off, 1 << b)],
                                  sem).start()

    for b in range(min(COMMON_BITS, max_bits)):
        arm(b)
    if max_bits > COMMON_BITS:
        @pl.when(groups >= (1 << COMMON_BITS))
        def _():
            for b in range(COMMON_BITS, max_bits):
                arm(b)


def _wait_groups(buf, groups, sem, max_bits):
    for b in range(max_bits):
        @pl.when(((groups >> b) & 1) == 1)
        def _():
            pltpu.make_async_copy(buf.at[pl.ds(0, 1 << b)], buf.at[pl.ds(0, 1 << b)], sem).wait()


COMMON_BITS = 3
RUN_BITS = (TD // ALIGN).bit_length()
TILE_BITS = (RL // ALIGN).bit_length()


def _dispatch_kernel(pc_ref, lo_ref, dst_ref, ts_ref, zd_ref, zl_ref, mask_ref, gate_ref, h_ref,
                     xs_ref, info_ref, loc_s, zero_s, sems):
    i = pl.program_id(0)
    last_step = pl.num_programs(0) - 1
    slot = i % 2
    mask = mask_ref[...]
    mb = mask.astype(BF16)
    s_i = lax.broadcasted_iota(jnp.int32, (TD, TD), 0)
    t_i = lax.broadcasted_iota(jnp.int32, (TD, TD), 1)
    earlier = jnp.where(s_i < t_i, 1.0, 0.0).astype(BF16)
    rank = jnp.dot(mb, earlier, preferred_element_type=F32)
    e_i = lax.broadcasted_iota(jnp.int32, (NE, NE), 0)
    f_i = lax.broadcasted_iota(jnp.int32, (NE, NE), 1)
    below = jnp.where(f_i < e_i, 1.0, 0.0).astype(BF16)
    lower = jnp.dot(below, mb, preferred_element_type=F32)
    row_e = lax.broadcasted_iota(jnp.int32, (NE, 1), 0)
    lo_vec = jnp.zeros((NE, 1), F32)
    for e in range(NE):
        lo_vec = jnp.where(row_e == e, (lo_ref[i * NE + e] * ALIGN).astype(F32), lo_vec)
    row = lo_vec + rank
    is0 = mask * jnp.where(lower == 0.0, 1.0, 0.0)
    is1 = mask - is0
    d0 = jnp.sum(is0 * row, axis=0, keepdims=True)
    d1 = jnp.sum(is1 * row, axis=0, keepdims=True)
    gate = gate_ref[...]
    w0 = jnp.sum(is0 * gate, axis=0, keepdims=True)
    w1 = jnp.sum(is1 * gate, axis=0, keepdims=True)
    info_ref[...] = jnp.concatenate([d0, d1, w0, w1, jnp.zeros((4, TD), F32)], axis=0)
    r_i = lax.broadcasted_iota(jnp.int32, (RL, TD), 0)
    perm = jnp.where((r_i == d0.astype(jnp.int32)) | (r_i == d1.astype(jnp.int32)), 1.0, 0.0).astype(BF16)
    loc = loc_s.at[slot]
    loc[...] = jnp.dot(perm, h_ref[...], preferred_element_type=F32).astype(BF16).reshape(RL // ALIGN, ALIGN, D)
    for e in range(NE):
        _run_copies(loc, xs_ref, lo_ref[i * NE + e], dst_ref[i * NE + e], pc_ref[i * NE + e], sems.at[slot],
                    RUN_BITS)

    @pl.when(i > 0)
    def _():
        _wait_groups(loc_s.at[1 - slot], ts_ref[jnp.maximum(i - 1, 0)], sems.at[1 - slot], TILE_BITS)

    @pl.when(i == last_step)
    def _():
        _wait_groups(loc, ts_ref[i], sems.at[slot], TILE_BITS)
        sem = sems.at[0]
        zero_s[...] = jnp.zeros_like(zero_s)
        zg = ZROWS // ALIGN
        for wait in (False, True):
            for e in range(NE):
                for part in range(TMX // ZROWS):
                    groups = jnp.clip(zl_ref[e] - part * zg, 0, zg)
                    if wait:
                        _wait_groups(zero_s, groups, sem, zg.bit_length())
                    else:
                        _run_copies(zero_s, xs_ref, 0, zd_ref[e] + part * zg, groups, sem, zg.bit_length())
        tail = zd_ref[NE]
        chunks = (xs_ref.shape[0] - tail) // zg

        def fill(c, carry):
            pltpu.make_async_copy(zero_s, xs_ref.at[pl.ds(tail + c * zg, zg)], sem).start()
            return carry

        def drain(c, carry):
            pltpu.make_async_copy(zero_s, zero_s, sem).wait()
            return carry

        lax.fori_loop(0, chunks, fill, 0)
        lax.fori_loop(0, chunks, drain, 0)


def _dispatch(plan, mask_t, gate_t, h2, rtot):
    n = h2.shape[0]
    ntd = n // TD
    tspec = pl.BlockSpec((NE, TD), lambda i, *_: (0, i))
    grid_spec = pltpu.PrefetchScalarGridSpec(
        num_scalar_prefetch=6, grid=(ntd,),
        in_specs=[tspec, tspec, pl.BlockSpec((TD, D), lambda i, *_: (i, 0))],
        out_specs=[pl.BlockSpec(memory_space=pl.ANY), pl.BlockSpec((8, TD), lambda i, *_: (0, i))],
        scratch_shapes=[pltpu.VMEM((2, RL // ALIGN, ALIGN, D), BF16), pltpu.VMEM((ZROWS // ALIGN, ALIGN, D), BF16),
                        pltpu.SemaphoreType.DMA((2,))])
    return pl.pallas_call(
        _dispatch_kernel, grid_spec=grid_spec,
        out_shape=[jax.ShapeDtypeStruct((rtot // ALIGN, ALIGN, D), BF16), jax.ShapeDtypeStruct((8, n), F32)],
        compiler_params=pltpu.CompilerParams(dimension_semantics=("arbitrary",), vmem_limit_bytes=VMEM_LIMIT),
        name="dispatch",
    )(plan["pc"], plan["lo"], plan["dst"], plan["tsum"], plan["zdst"], plan["zlen"], mask_t, gate_t, h2)


def _expert_kernel(texp_ref, nact_ref, x_ref, w1_ref, w3_ref, w2_ref, y_ref, w1_s, w3_s, w2_s):
    m = pl.program_id(0)
    new_expert = (m == 0) | (texp_ref[m] != texp_ref[jnp.maximum(m - 1, 0)])

    @pl.when(new_expert)
    def _():
        w1_s[...] = w1_ref[...].astype(BF16)
        w3_s[...] = w3_ref[...].astype(BF16)
        w2_s[...] = w2_ref[...].astype(BF16)

    @pl.when(m < nact_ref[0])
    def _():
        x = x_ref[...]
        a = jnp.dot(x, w1_s[...], preferred_element_type=F32)
        a = a * _sigmoid(a) * jnp.dot(x, w3_s[...], preferred_element_type=F32)
        y_ref[...] = jnp.dot(a.astype(BF16), w2_s[...], preferred_element_type=F32).astype(BF16)

    @pl.when(m >= nact_ref[0])
    def _():
        y_ref[...] = jnp.zeros_like(y_ref)


def _experts(plan, xs, layer, w1, w3, w2):
    rtot = xs.shape[0]
    n_mt = rtot // TMX
    grid_spec = pltpu.PrefetchScalarGridSpec(
        num_scalar_prefetch=2, grid=(n_mt,),
        in_specs=[pl.BlockSpec((TMX, D), lambda m, te, na: (jnp.minimum(m, na[0] - 1), 0)),
                  pl.BlockSpec((None, None, D, FF), lambda m, te, na: (layer, te[m], 0, 0)),
                  pl.BlockSpec((None, None, D, FF), lambda m, te, na: (layer, te[m], 0, 0)),
                  pl.BlockSpec((None, None, FF, D), lambda m, te, na: (layer, te[m], 0, 0))],
        out_specs=pl.BlockSpec((TMX, D), lambda m, te, na: (m, 0)),
        scratch_shapes=[pltpu.VMEM((D, FF), BF16), pltpu.VMEM((D, FF), BF16), pltpu.VMEM((FF, D), BF16)])
    return pl.pallas_call(
        _expert_kernel, grid_spec=grid_spec,
        out_shape=jax.ShapeDtypeStruct((rtot, D), BF16),
        compiler_params=pltpu.CompilerParams(dimension_semantics=("arbitrary",), vmem_limit_bytes=VMEM_LIMIT),
        name="experts",
    )(plan["texp"], plan["nact"], xs, w1, w3, w2)


def _combine_kernel(split, bidx_ref, pc_ref, lo_ref, dst_ref, ts_ref, info_ref, ys_ref, x1_ref, mod_ref,
                    lng_ref, lnb_ref, *rest):
    del bidx_ref
    o_refs, (loc_s, sems) = rest[:-2], rest[-2:]
    i = pl.program_id(0)
    slot = i % 2

    def fetch(tile, buf):
        for e in range(NE):
            _run_copies(ys_ref, loc_s.at[buf], dst_ref[tile * NE + e], lo_ref[tile * NE + e],
                        pc_ref[tile * NE + e], sems.at[buf], RUN_BITS)

    @pl.when(i == 0)
    def _():
        loc_s[...] = jnp.zeros_like(loc_s)
        fetch(0, 0)

    @pl.when(i + 1 < pl.num_programs(0))
    def _():
        fetch(i + 1, 1 - slot)

    _wait_groups(loc_s.at[slot], ts_ref[i], sems.at[slot], TILE_BITS)
    info = info_ref[...]
    d0 = info[0:1].astype(jnp.int32)
    d1 = info[1:2].astype(jnp.int32)
    r_i = lax.broadcasted_iota(jnp.int32, (RL, TD), 0)
    wperm = (jnp.where(r_i == d0, info[2:3], 0.0) + jnp.where(r_i == d1, info[3:4], 0.0)).astype(BF16)
    moe = lax.dot_general(wperm, loc_s[slot].reshape(RL, D), (((0,), (0,)), ((), ())),
                          preferred_element_type=F32)
    gate = mod_ref[...][:, 2 * D:]
    out = _ln(ALPHA * x1_ref[...] + gate * moe) * lng_ref[...] + lnb_ref[...]
    if len(o_refs) == 1:
        o_refs[0][...] = out
    else:
        @pl.when(i < split)
        def _():
            o_refs[0][...] = out

        @pl.when(i >= split)
        def _():
            o_refs[1][...] = out


def _combine(bidx_d, plan, info, ys, x1, mod2, lng, lnb, out_rows):
    n = x1.shape[0]
    ntd = n // TD
    tok = pl.BlockSpec((TD, D), lambda i, *_: (i, 0))
    full = lambda a: pl.BlockSpec(a.shape, lambda i, *_: (0,) * a.ndim)
    if len(out_rows) == 1:
        split, out_specs = 0, [tok]
    else:
        split = out_rows[0] // TD
        out_specs = [pl.BlockSpec((TD, D), lambda i, *_: (jnp.minimum(i, split - 1), 0)),
                     pl.BlockSpec((TD, D), lambda i, *_: (jnp.maximum(i - split, 0), 0))]
    grid_spec = pltpu.PrefetchScalarGridSpec(
        num_scalar_prefetch=5, grid=(ntd,),
        in_specs=[pl.BlockSpec((8, TD), lambda i, *_: (0, i)),
                  pl.BlockSpec(memory_space=pl.ANY),
                  tok,
                  pl.BlockSpec((None, 1, 3 * D), lambda i, b, *_: (b[i], 0, 0)),
                  full(lng), full(lnb)],
        out_specs=out_specs,
        scratch_shapes=[pltpu.VMEM((2, RL // ALIGN, ALIGN, D), BF16), pltpu.SemaphoreType.DMA((2,))])
    return pl.pallas_call(
        functools.partial(_combine_kernel, split), grid_spec=grid_spec,
        out_shape=[jax.ShapeDtypeStruct((r, D), F32) for r in out_rows],
        compiler_params=pltpu.CompilerParams(dimension_semantics=("arbitrary",), vmem_limit_bytes=VMEM_LIMIT),
        name="combine",
    )(bidx_d, plan["pc"], plan["lo"], plan["dst"], plan["tsum"], info, ys, x1, mod2, lng, lnb)


def _moe(meta, h2, mask_t, gate_t, cnt, layer, w1, w3, w2, x1, mod2, lng, lnb, out_rows):
    n = h2.shape[0]
    ntd = n // TD
    rtot = -(-(ntd * RL) // TMX) * TMX + NE * TMX
    cnt = cnt[:, :, :TM // TD].transpose(0, 2, 1).reshape(ntd, NE).astype(jnp.int32)
    plan = _plan(cnt, rtot // TMX)
    xs, info = _dispatch(plan, mask_t, gate_t, h2, rtot)
    ys = _experts(plan, xs.reshape(rtot, D), layer, w1, w3, w2).reshape(rtot // ALIGN, ALIGN, D)
    return _combine(meta["bidx_d"], plan, info, ys, x1, mod2, lng, lnb, out_rows)


def _tile_meta(groups):
    bidx, pblk, first, last, cf, cb, bidx_d = [], [], [], [], [], [], []
    row = 0
    tile = 0
    for (b, s) in groups:
        per = s // TM
        for bi in range(b):
            for j in range(per):
                bidx.append(row + bi)
                pblk.append(j)
                first.append(1 if j == 0 else 0)
                last.append(1 if j == per - 1 else 0)
                cf.append(tile + j)
                cb.append(tile + per - 1 - j)
            tile += per
            bidx_d += [row + bi] * (s // TD)
        row += b
    as_i32 = lambda v: jnp.asarray(np.asarray(v, np.int32))
    return dict(bidx=as_i32(bidx), pblk=as_i32(pblk), first=as_i32(first), last=as_i32(last),
                cf=as_i32(cf), cb=as_i32(cb), bidx_d=as_i32(bidx_d))


def _rope_tables(s_max):
    inv = jnp.power(ROPE_THETA, -jnp.arange(0, HD, 2, dtype=F32) / HD)
    ang = jnp.arange(s_max, dtype=F32)[:, None] * inv[None, :]
    cos, sin = jnp.cos(ang), jnp.sin(ang)
    cos128 = jnp.tile(cos, (1, 4))
    sin128 = jnp.tile(jnp.concatenate([-sin, sin], axis=1), (1, 2))
    return cos128, sin128


def _block_diag(w):
    d, nb, c, f = w.shape
    eye = jnp.eye(nb, dtype=w.dtype)
    return jnp.einsum('dncf,nm->dncmf', w, eye).reshape(d, nb * c, nb * f)


def _forward(xs, cs, w_mod, b_mod, w_in, attn_sink, conv_w, conv_b, lru_w_r, lru_b_r, lru_w_i, lru_b_i,
             lru_lambda, sg_norm_g, sg_w, sg_b, mix_norm_g, w_out, ln_g, ln_b, router_w, router_bias,
             exp_w1, exp_w3, exp_w2):
    groups = [(x.shape[0], x.shape[1]) for x in xs]
    assert len(groups) == 2
    for (_, s) in groups:
        assert s % TM == 0
    meta = _tile_meta(groups)
    x_parts = [xx.reshape(-1, D) for xx in xs]
    group_rows = [p.shape[0] for p in x_parts]
    c_all = jnp.concatenate(cs, axis=0)
    bt = c_all.shape[0]
    mods = _modulation(c_all, w_mod, b_mod).reshape(2 * DEPTH, bt, 1, 3 * D)
    cos_t, sin_t = _rope_tables(max(s for _, s in groups))
    rwt = router_w.T.astype(BF16)
    rb = router_bias.reshape(NE, 1)
    for l in range(DEPTH):
        q, kk, vv, gy, xr, u, vn = _pre_mixer(meta, x_parts, mods[2 * l], w_in[l].astype(BF16), cos_t, sin_t,
                                              sg_norm_g[l].reshape(1, SGW))
        oa = _attention(meta, attn_sink[l], q, kk, vv)
        hf, hb = _recurrent(meta, xr, conv_w[l], conv_b[l].reshape(1, RW),
                            _block_diag(0.5 * lru_w_r[l]).astype(BF16), _block_diag(0.5 * lru_w_i[l]).astype(BF16),
                            0.5 * lru_b_r[l], 0.5 * lru_b_i[l], lru_lambda[l])
        sgb = jnp.repeat(sg_b[l].T, HD, axis=1)
        x1, h2, gate_t, mask_t, cnt = _merge(meta, x_parts, oa, gy, hf, hb, u, vn, mods[2 * l], mods[2 * l + 1],
                                             mix_norm_g[l].reshape(1, D), sg_w[l].astype(BF16), sgb,
                                             w_out[l].astype(BF16), ln_g[l, 0].reshape(1, D),
                                             ln_b[l, 0].reshape(1, D), rwt, rb)
        out_rows = group_rows if l == DEPTH - 1 else [sum(group_rows)]
        x_parts = _moe(meta, h2, mask_t, gate_t, cnt, l, exp_w1, exp_w3, exp_w2, x1, mods[2 * l + 1],
                       ln_g[l, 1].reshape(1, D), ln_b[l, 1].reshape(1, D), out_rows)
    return tuple(p.reshape(b, s, D) for p, (b, s) in zip(x_parts, groups))


def kernel(x_prompt, x_sample, c_prompt, c_sample, w_mod, b_mod, w_in, attn_sink, conv_w, conv_b, lru_w_r, lru_b_r,
           lru_w_i, lru_b_i, lru_lambda, sg_norm_g, sg_w, sg_b, mix_norm_g, w_out, ln_g, ln_b, router_w,
           router_bias, exp_w1, exp_w3, exp_w2):
    return _forward([x_prompt, x_sample], [c_prompt, c_sample], w_mod, b_mod, w_in, attn_sink, conv_w, conv_b,
                    lru_w_r, lru_b_r, lru_w_i, lru_b_i, lru_lambda, sg_norm_g, sg_w, sg_b, mix_norm_g, w_out,
                    ln_g, ln_b, router_w, router_bias, exp_w1, exp_w3, exp_w2)
```

```python
import functools

import numpy as np
import jax
import jax.numpy as jnp
from jax import lax
from jax.experimental import pallas as pl
from jax.experimental.pallas import tpu as pltpu

F32 = jnp.float32
BF16 = jnp.bfloat16

D = 1024
DEPTH = 2
HD = 64
AW = 384
KVW = 128
RW = 384
SGW = 256
WINDOW = 128
CONV_WIDTH = 4
LRU_C = 8.0
CHUNK = 128
NE = 16
NG = 4
EPG = NE // NG
FF = 512
ALPHA = (2 * DEPTH) ** 0.25
LN_EPS = 1e-5
RMS_EPS = 1e-6
ROPE_THETA = 10000.0

TM = 512
QB = 128
SEG = TM // 8
NLG = RW // 128
TD = 256
ALIGN = 16
RL = 2 * TD + NE * ALIGN
TMX = 1024
ZROWS = 256
NEG = -1e30
VMEM_LIMIT = 48 * 1024 * 1024

C_Q, C_K, C_V, C_Y, C_R, C_U, C_SV, C_END = 0, 384, 512, 640, 1024, 1408, 1664, 1920


def _ln(x):
    mu = jnp.mean(x, axis=-1, keepdims=True)
    xc = x - mu
    var = jnp.mean(xc * xc, axis=-1, keepdims=True)
    return xc * lax.rsqrt(var + LN_EPS)


def _rms(x):
    return x * lax.rsqrt(jnp.mean(x * x, axis=-1, keepdims=True) + RMS_EPS)


def _sigmoid(z):
    return 0.5 * (jnp.tanh(0.5 * z) + 1.0)


def _gelu(x):
    return 0.5 * x * (1.0 + jnp.tanh(0.7978845608028654 * (x + 0.044715 * (x * x * x))))


def _split_bf16(a):
    hi = a.astype(BF16)
    lo = (a - hi.astype(F32)).astype(BF16)
    return hi, lo


def _mod_kernel(c_ref, w_ref, b_ref, o_ref):
    c = c_ref[...]
    ca = c * _sigmoid(c)
    ch, cl = _split_bf16(ca)
    wh, wl = _split_bf16(w_ref[...])
    acc = jnp.dot(ch, wh, preferred_element_type=F32)
    acc += jnp.dot(ch, wl, preferred_element_type=F32)
    acc += jnp.dot(cl, wh, preferred_element_type=F32)
    o_ref[...] = acc + b_ref[...]


def _modulation(c_all, w_mod, b_mod):
    bt = c_all.shape[0]
    cb = 768
    w = w_mod.reshape(2 * DEPTH, D, 3 * D)
    b = b_mod.reshape(2 * DEPTH, 1, 3 * D)
    return pl.pallas_call(
        _mod_kernel,
        grid=(2 * DEPTH, 3 * D // cb),
        in_specs=[pl.BlockSpec((bt, D), lambda s, j: (0, 0)),
                  pl.BlockSpec((None, D, cb), lambda s, j: (s, 0, j)),
                  pl.BlockSpec((None, 1, cb), lambda s, j: (s, 0, j))],
        out_specs=pl.BlockSpec((None, bt, cb), lambda s, j: (s, 0, j)),
        out_shape=jax.ShapeDtypeStruct((2 * DEPTH, bt, 3 * D), F32),
        name="modulation",
    )(c_all, w, b)


def _load_tokens(i, x_refs, split):
    if len(x_refs) == 1:
        return x_refs[0][...]
    return jnp.where(i < split, x_refs[0][...], x_refs[1][...])


def _token_specs(x_parts, rows):
    if len(x_parts) == 1:
        return [pl.BlockSpec((rows, D), lambda i, *_: (i, 0))], 0
    split = x_parts[0].shape[0] // rows
    return [pl.BlockSpec((rows, D), lambda i, *_: (jnp.minimum(i, split - 1), 0)),
            pl.BlockSpec((rows, D), lambda i, *_: (jnp.maximum(i - split, 0), 0))], split


def _pre_kernel(split, bidx_ref, pblk_ref, *refs):
    del bidx_ref, pblk_ref
    nx = len(refs) - 12
    x_refs = refs[:nx]
    mod_ref, w_ref, cos_ref, sin_ref, sgg_ref, q_ref, kk_ref, vv_ref, gy_ref, xr_ref, u_ref, vn_ref = refs[nx:]
    mod = mod_ref[...]
    shift, scale = mod[:, :D], mod[:, D:2 * D]
    h = (_ln(_load_tokens(pl.program_id(0), x_refs, split)) * (1.0 + scale) + shift).astype(BF16)
    cos = cos_ref[...]
    sin = sin_ref[...]
    lane = lax.broadcasted_iota(jnp.int32, (1, 128), 1)
    first_half = (lane % HD) < (HD // 2)

    def rope128(xg):
        rot = jnp.where(first_half, pltpu.roll(xg, 128 - HD // 2, 1), pltpu.roll(xg, HD // 2, 1))
        return xg * cos + rot * sin

    q = jnp.dot(h, w_ref[:, C_Q:C_K], preferred_element_type=F32)
    for g in range(AW // 128):
        q_ref[:, g * 128:(g + 1) * 128] = (rope128(q[:, g * 128:(g + 1) * 128]) * (HD ** -0.5)).astype(BF16)
    kv = jnp.dot(h, w_ref[:, C_K:C_Y], preferred_element_type=F32)
    lo_half = lane < HD
    for src, dst_ref in ((rope128(kv[:, :KVW]), kk_ref), (kv[:, KVW:], vv_ref)):
        swapped = pltpu.roll(src, HD, 1)
        dst_ref[:, 0:128] = jnp.where(lo_half, src, swapped).astype(BF16)
        dst_ref[:, 128:256] = jnp.where(lo_half, swapped, src).astype(BF16)
    gy_ref[...] = _gelu(jnp.dot(h, w_ref[:, C_Y:C_R], preferred_element_type=F32)).astype(gy_ref.dtype)
    xr = jnp.dot(h, w_ref[:, C_R:C_U], preferred_element_type=F32)
    for g in range(NLG):
        xr_ref[g] = xr[:, g * 128:(g + 1) * 128]
    u_ref[...] = _gelu(jnp.dot(h, w_ref[:, C_U:C_SV], preferred_element_type=F32)).astype(u_ref.dtype)
    sv = _gelu(jnp.dot(h, w_ref[:, C_SV:C_END], preferred_element_type=F32))
    vn_ref[...] = (_ln(sv) * sgg_ref[...]).astype(BF16)


def _pre_mixer(meta, x_parts, mod_l, w_ext, cos_t, sin_t, sgg):
    n = sum(p.shape[0] for p in x_parts)
    nt = n // TM
    tok = lambda w: pl.BlockSpec((TM, w), lambda i, b, p: (i, 0))
    full = lambda a: pl.BlockSpec(a.shape, lambda i, b, p: (0,) * a.ndim)
    x_specs, split = _token_specs(x_parts, TM)
    grid_spec = pltpu.PrefetchScalarGridSpec(
        num_scalar_prefetch=2, grid=(nt,),
        in_specs=x_specs + [
            pl.BlockSpec((None, 1, 3 * D), lambda i, b, p: (b[i], 0, 0)),
            full(w_ext),
            pl.BlockSpec((TM, 128), lambda i, b, p: (p[i], 0)),
            pl.BlockSpec((TM, 128), lambda i, b, p: (p[i], 0)),
            full(sgg)],
        out_specs=[tok(AW), tok(256), tok(256), tok(RW),
                   pl.BlockSpec((NLG, TM, 128), lambda i, b, p: (0, i, 0)), tok(SGW), tok(SGW)])
    sds = lambda w, dt: jax.ShapeDtypeStruct((n, w), dt)
    return pl.pallas_call(
        functools.partial(_pre_kernel, split), grid_spec=grid_spec,
        out_shape=[sds(AW, BF16), sds(256, BF16), sds(256, BF16), sds(RW, BF16),
                   jax.ShapeDtypeStruct((NLG, n, 128), F32), sds(SGW, BF16), sds(SGW, BF16)],
        compiler_params=pltpu.CompilerParams(dimension_semantics=("parallel",), vmem_limit_bytes=VMEM_LIMIT),
        name="pre_mixer",
    )(meta["bidx"], meta["pblk"], *x_parts, mod_l, w_ext, cos_t, sin_t, sgg)


def _attn_kernel(first_ref, last_ref, sink_ref, q_ref, k_ref, kp_ref, kn_ref, v_ref, vp_ref, vn_ref,
                 o_ref, kw_s, vw_s):
    i = pl.program_id(0)
    is_first = first_ref[i] == 1
    is_last = last_ref[i] == 1
    kw_s[0:QB] = kp_ref[...]
    kw_s[QB:QB + TM] = k_ref[...]
    kw_s[QB + TM:] = kn_ref[...]
    vw_s[0:QB] = vp_ref[...]
    vw_s[QB:QB + TM] = v_ref[...]
    vw_s[QB + TM:] = vn_ref[...]
    iq = lax.broadcasted_iota(jnp.int32, (QB, 3 * QB), 0)
    ik = lax.broadcasted_iota(jnp.int32, (QB, 3 * QB), 1)
    rel = ik - iq
    band = (rel >= 0) & (rel <= 2 * WINDOW)
    lane = lax.broadcasted_iota(jnp.int32, (1, 128), 1)
    lo_half = lane < HD
    zero = jnp.zeros((), BF16)
    gsz = AW // HD // 2
    row_head = lax.broadcasted_iota(jnp.int32, (gsz * QB, 1), 0) // QB
    for j in range(TM // QB):
        ok = band
        if j == 0:
            ok = ok & ((ik >= QB) | jnp.logical_not(is_first))
        if j == TM // QB - 1:
            ok = ok & ((ik < 2 * QB) | jnp.logical_not(is_last))
        bias = jnp.where(ok, 0.0, NEG)
        bias = jnp.concatenate([bias] * gsz, axis=0)
        rows = slice(j * QB, (j + 1) * QB)
        keys = slice(j * QB, j * QB + 3 * QB)
        outs = [None] * (AW // HD)
        for g in range(2):
            heads = range(g * gsz, (g + 1) * gsz)
            qs = []
            for hh in heads:
                qg = q_ref[rows, (hh // 2) * 128:(hh // 2 + 1) * 128]
                qs.append(jnp.where(lo_half if hh % 2 == 0 else jnp.logical_not(lo_half), qg, zero))
            s = lax.dot_general(jnp.concatenate(qs, axis=0), kw_s[keys, g * 128:(g + 1) * 128],
                                (((1,), (1,)), ((), ())), preferred_element_type=F32) + bias
            sink = jnp.full((gsz * QB, 1), sink_ref[heads[-1]], F32)
            for t in range(gsz - 1):
                sink = jnp.where(row_head == t, sink_ref[heads[t]], sink)
            m = jnp.maximum(jnp.max(s, axis=-1, keepdims=True), sink)
            p = jnp.exp(s - m)
            denom = jnp.sum(p, axis=-1, keepdims=True) + jnp.exp(sink - m)
            pv = jnp.dot(p.astype(BF16), vw_s[keys, g * 128:(g + 1) * 128], preferred_element_type=F32)
            o = pv * (1.0 / denom)
            for t, hh in enumerate(heads):
                outs[hh] = o[t * QB:(t + 1) * QB]
        for jg in range(AW // 128):
            o_ref[rows, jg * 128:(jg + 1) * 128] = jnp.where(lo_half, outs[2 * jg], outs[2 * jg + 1]).astype(o_ref.dtype)


def _attention(meta, sink, q, kk, vv):
    n = q.shape[0]
    nt = n // TM
    nqb = n // QB
    r = TM // QB
    main = lambda w: pl.BlockSpec((TM, w), lambda i, f, l, s: (i, 0))
    prev = pl.BlockSpec((QB, 256), lambda i, f, l, s: (jnp.maximum(i * r - 1, 0), 0))
    nxt = pl.BlockSpec((QB, 256), lambda i, f, l, s: (jnp.minimum(i * r + r, nqb - 1), 0))
    grid_spec = pltpu.PrefetchScalarGridSpec(
        num_scalar_prefetch=3, grid=(nt,),
        in_specs=[main(AW), main(256), prev, nxt, main(256), prev, nxt],
        out_specs=main(AW),
        scratch_shapes=[pltpu.VMEM((TM + 2 * QB, 256), BF16), pltpu.VMEM((TM + 2 * QB, 256), BF16)])
    return pl.pallas_call(
        _attn_kernel, grid_spec=grid_spec,
        out_shape=jax.ShapeDtypeStruct((n, AW), BF16),
        compiler_params=pltpu.CompilerParams(dimension_semantics=("parallel",), vmem_limit_bytes=VMEM_LIMIT),
        name="attention",
    )(meta["first"], meta["last"], sink, q, kk, kk, kk, vv, vv, vv)


def _rec_kernel(cf_ref, cb_ref, first_ref, last_ref,
                x_hbm, xfp_ref, xfn_ref, xbp_ref, xbn_ref,
                cw_ref, cbias_ref, wr_ref, wi_ref, br_ref, bi_ref, lam_ref,
                hf_hbm, hb_hbm,
                ext_s, a_s, b_s, hs_s, ps_s, out_s, carry_s, gsem, ssem):
    i = pl.program_id(0)
    nsteps = pl.num_programs(0)
    slot = i % 2
    cw = cw_ref[...]
    cbias = cbias_ref[...]
    sub = lax.broadcasted_iota(jnp.int32, (8, 128), 0)
    unroll = 8
    chunk_refs = (cf_ref, cb_ref)
    out_hbm = (hf_hbm, hb_hbm)

    def gather(step, buf, d, start):
        c = chunk_refs[d][step]
        for g in range(NLG):
            for s in range(8):
                cp = pltpu.make_async_copy(x_hbm.at[g, pl.ds(pl.multiple_of(c * TM + s * SEG, SEG), SEG), :],
                                           ext_s.at[buf, d, g, pl.ds(2, SEG), s, :], gsem.at[buf, d])
                if start:
                    cp.start()
                else:
                    cp.wait()

    def scatter(d, c, start):
        for g in range(NLG):
            for s in range(8):
                cp = pltpu.make_async_copy(out_s.at[d, g, :, s, :],
                                           out_hbm[d].at[g, pl.ds(pl.multiple_of(c * TM + s * SEG, SEG), SEG), :],
                                           ssem.at[d])
                if start:
                    cp.start()
                else:
                    cp.wait()

    @pl.when(i == 0)
    def _():
        gather(0, 0, 0, True)
        gather(0, 0, 1, True)

    @pl.when(i + 1 < nsteps)
    def _():
        gather(i + 1, 1 - slot, 0, True)
        gather(i + 1, 1 - slot, 1, True)

    def run(d, xp_ref, xn_ref):
        c = chunk_refs[d][i]
        first = first_ref[c] == 1
        last = last_ref[c] == 1
        gather(i, slot, d, False)
        ext = ext_s.at[slot, d]
        planes = []
        for g in range(NLG):
            prev = jnp.where(first, 0.0, xp_ref[g])
            nxt = jnp.where(last, 0.0, xn_ref[g])
            ext[g, 0] = jnp.where(sub == 0, prev[6:7], pltpu.roll(ext[g, SEG], 1, 0))
            ext[g, 1] = jnp.where(sub == 0, prev[7:8], pltpu.roll(ext[g, SEG + 1], 1, 0))
            ext[g, SEG + 2] = jnp.where(sub == 7, nxt[0:1], pltpu.roll(ext[g, 2], 7, 0))
            lanes = slice(g * 128, (g + 1) * 128)
            acc = ext[g, 0:SEG] * cw[0:1, lanes]
            for t in range(1, CONV_WIDTH):
                acc = acc + ext[g, t:t + SEG] * cw[t:t + 1, lanes]
            planes.append(acc.reshape(TM, 128))
        xc = jnp.concatenate(planes, axis=1) + cbias
        xb16 = xc.astype(BF16)
        tr = jnp.tanh(jnp.dot(xb16, wr_ref[d], preferred_element_type=F32) + br_ref[d:d + 1])
        ti = jnp.tanh(jnp.dot(xb16, wi_ref[d], preferred_element_type=F32) + bi_ref[d:d + 1])
        nlam = -lam_ref[d:d + 1]
        softplus = jnp.maximum(nlam, 0.0) + jnp.log1p(jnp.exp(-jnp.abs(nlam)))
        half_c = (-0.5 * LRU_C) * softplus
        log_a = half_c * tr + half_c
        a = jnp.exp(log_a)
        v = (1.0 + a * a) * jnp.tanh(-log_a)
        b = jnp.where(v > 0.0, v * lax.rsqrt(v), 0.0) * (ti + 1.0) * (0.5 * xc)
        for g in range(NLG):
            a_s[g] = a[:, g * 128:(g + 1) * 128].reshape(SEG, 8, 128)
            b_s[g] = b[:, g * 128:(g + 1) * 128].reshape(SEG, 8, 128)

        reverse = d == 1
        reset = last if reverse else first

        @pl.when(reset)
        def _():
            carry_s[d] = jnp.zeros((NLG, 128), F32)

        def step(jo, hp):
            hp = list(hp)
            for ji in range(unroll):
                jj = jo * unroll + ji
                j = SEG - 1 - jj if reverse else jj
                for g in range(NLG):
                    h, p = hp[g]
                    ag = a_s[g, j]
                    h = ag * h + b_s[g, j]
                    p = p * ag
                    hs_s[g, j] = h
                    ps_s[g, j] = p
                    hp[g] = (h, p)
            return tuple(hp)

        init = tuple((jnp.zeros((8, 128), F32), jnp.ones((8, 128), F32)) for _ in range(NLG))
        ends = lax.fori_loop(0, SEG // unroll, step, init)
        cmats = []
        for g in range(NLG):
            e, pe = ends[g]
            c_in = carry_s[d, g:g + 1]
            rows = [None] * 8
            for s in (range(7, -1, -1) if reverse else range(8)):
                rows[s] = c_in
                c_in = e[s:s + 1] + pe[s:s + 1] * c_in
            carry_s[d, g:g + 1] = c_in
            cmats.append(jnp.concatenate(rows, axis=0))

        @pl.when(i > 0)
        def _():
            scatter(d, 0, False)

        def fix(jo, carry):
            for ji in range(unroll):
                j = jo * unroll + ji
                for g in range(NLG):
                    out_s[d, g, j] = hs_s[g, j] + ps_s[g, j] * cmats[g]
            return carry

        lax.fori_loop(0, SEG // unroll, fix, 0)
        scatter(d, c, True)

    run(0, xfp_ref, xfn_ref)
    run(1, xbp_ref, xbn_ref)

    @pl.when(i == nsteps - 1)
    def _():
        scatter(0, 0, False)
        scatter(1, 0, False)


def _recurrent(meta, xr, cw, cbias, wr, wi, br, bi, lam):
    n = xr.shape[1]
    nt = n // TM
    n8 = n // 8
    r8 = TM // 8

    def halos(which):
        sel = (lambda cf, cb: cf) if which == 0 else (lambda cf, cb: cb)
        prev = pl.BlockSpec((NLG, 8, 128),
                            lambda i, cf, cb, f, l: (0, jnp.maximum(sel(cf, cb)[i] * r8 - 1, 0), 0))
        nxt = pl.BlockSpec((NLG, 8, 128),
                           lambda i, cf, cb, f, l: (0, jnp.minimum(sel(cf, cb)[i] * r8 + r8, n8 - 1), 0))
        return prev, nxt

    full = lambda a: pl.BlockSpec(a.shape, lambda i, cf, cb, f, l: (0,) * a.ndim)
    anywhere = pl.BlockSpec(memory_space=pl.ANY)
    plane = lambda lead: pltpu.VMEM(lead + (NLG, SEG, 8, 128), F32)
    grid_spec = pltpu.PrefetchScalarGridSpec(
        num_scalar_prefetch=4, grid=(nt,),
        in_specs=[anywhere, *halos(0), *halos(1),
                  full(cw), full(cbias), full(wr), full(wi), full(br), full(bi), full(lam)],
        out_specs=[anywhere, anywhere],
        scratch_shapes=[pltpu.VMEM((2, 2, NLG, SEG + 3, 8, 128), F32),
                        plane(()), plane(()), plane(()), plane(()), plane((2,)),
                        pltpu.VMEM((2, NLG, 128), F32),
                        pltpu.SemaphoreType.DMA((2, 2)), pltpu.SemaphoreType.DMA((2,))])
    return pl.pallas_call(
        _rec_kernel, grid_spec=grid_spec,
        out_shape=[jax.ShapeDtypeStruct((NLG, n, 128), F32), jax.ShapeDtypeStruct((NLG, n, 128), F32)],
        compiler_params=pltpu.CompilerParams(dimension_semantics=("arbitrary",), vmem_limit_bytes=VMEM_LIMIT),
        name="recurrent",
    )(meta["cf"], meta["cb"], meta["first"], meta["last"], xr, xr, xr, xr, xr, cw, cbias, wr, wi, br, bi, lam)


def _route(sel, score):
    one = jnp.ones_like(sel[0])
    zero = jnp.zeros_like(sel[0])

    def before(vk, vj, k, j):
        return (vk > vj) | ((vk == vj) & (k < j)) if k < j else (vk > vj)

    in_top = []
    gscore = []
    for g in range(NG):
        ids = range(g * EPG, (g + 1) * EPG)
        gs = zero
        for j in ids:
            rank = zero
            for k in ids:
                if k != j:
                    rank = rank + jnp.where(before(sel[k], sel[j], k, j), one, zero)
            m = jnp.where(rank < 2.0, one, zero)
            in_top.append(m)
            gs = gs + m * sel[j]
        gscore.append(gs)
    mask = []
    for g in range(NG):
        worse = zero
        for k in range(NG):
            if k != g:
                worse = worse + jnp.where(before(gscore[k], gscore[g], k, g), one, zero)
        best = jnp.where(worse < 1.0, one, zero)
        for j in range(g * EPG, (g + 1) * EPG):
            mask.append(in_top[j] * best)
    total = zero
    for e in range(NE):
        total = total + mask[e] * score[e]
    inv = 1.0 / total
    gate = [mask[e] * score[e] * inv for e in range(NE)]
    return mask, gate


def _merge_kernel(split, bidx_ref, *refs):
    del bidx_ref
    nx = len(refs) - 22
    x_refs = refs[:nx]
    (oa_ref, gy_ref, hf_ref, hb_ref, u_ref, vn_ref, mod1_ref, mod2_ref, gmix_ref, sgw_ref, sgb_ref, wout_ref,
     lng_ref, lnb_ref, rwt_ref, rb_ref, x1_ref, h2_ref, gate_ref, mask_ref, cnt_ref, mrg_s) = refs[nx:]
    gmix = gmix_ref[...]
    mrg_s[:, 0:AW] = (_rms(oa_ref[...].astype(F32)) * gmix[:, 0:AW]).astype(BF16)
    hsum = jnp.concatenate([hf_ref[g] + hb_ref[g] for g in range(NLG)], axis=1)
    o_rec = gy_ref[...].astype(F32) * hsum
    mrg_s[:, AW:AW + RW] = (_rms(o_rec) * gmix[:, AW:AW + RW]).astype(BF16)
    lane = lax.broadcasted_iota(jnp.int32, (1, 128), 1)
    lo_half = lane < HD
    zero = jnp.zeros((), BF16)
    pieces = []
    for c in range(TM // CHUNK):
        rows = slice(c * CHUNK, (c + 1) * CHUNK)
        grp = []
        for g in range(SGW // 128):
            vg = vn_ref[rows, g * 128:(g + 1) * 128]
            mixed = jnp.dot(sgw_ref[2 * g], jnp.where(lo_half, vg, zero), preferred_element_type=F32)
            mixed += jnp.dot(sgw_ref[2 * g + 1], jnp.where(lo_half, zero, vg), preferred_element_type=F32)
            grp.append(mixed)
        mixed = jnp.concatenate(grp, axis=1) + sgb_ref[...]
        pieces.append(u_ref[rows, :].astype(F32) * mixed)
    o_sg = jnp.concatenate(pieces, axis=0)
    mrg_s[:, AW + RW:] = (_rms(o_sg) * gmix[:, AW + RW:]).astype(BF16)
    o = jnp.dot(mrg_s[...], wout_ref[...], preferred_element_type=F32)
    gate1 = mod1_ref[...][:, 2 * D:]
    x_in = _load_tokens(pl.program_id(0), x_refs, split)
    x1 = _ln(ALPHA * x_in + gate1 * o) * lng_ref[...] + lnb_ref[...]
    x1_ref[...] = x1
    mod2 = mod2_ref[...]
    h2 = (_ln(x1) * (1.0 + mod2[:, D:2 * D]) + mod2[:, :D]).astype(BF16)
    h2_ref[...] = h2
    logits = lax.dot_general(rwt_ref[...], h2, (((1,), (1,)), ((), ())), preferred_element_type=F32)
    score = _sigmoid(logits)
    sel = score + rb_ref[...]
    mask, gate = _route([sel[e:e + 1] for e in range(NE)], [score[e:e + 1] for e in range(NE)])
    gate_ref[...] = jnp.concatenate(gate, axis=0)
    mask_t = jnp.concatenate(mask, axis=0)
    mask_ref[...] = mask_t
    t_i = lax.broadcasted_iota(jnp.int32, (TM, 128), 0)
    j_i = lax.broadcasted_iota(jnp.int32, (TM, 128), 1)
    in_tile = jnp.where((t_i >= j_i * TD) & (t_i < (j_i + 1) * TD), 1.0, 0.0).astype(BF16)
    cnt_ref[...] = jnp.dot(mask_t.astype(BF16), in_tile, preferred_element_type=F32)


def _merge(meta, x_parts, oa, gy, hf, hb, u, vn, mod1, mod2, gmix, sgw, sgb, wout, lng, lnb, rwt, rb):
    n = oa.shape[0]
    nt = n // TM
    x_specs, split = _token_specs(x_parts, TM)
    tok = lambda w: pl.BlockSpec((TM, w), lambda i, b: (i, 0))
    full = lambda a: pl.BlockSpec(a.shape, lambda i, b: (0,) * a.ndim)
    modspec = pl.BlockSpec((None, 1, 3 * D), lambda i, b: (b[i], 0, 0))
    tspec = pl.BlockSpec((NE, TM), lambda i, b: (0, i))
    rec = pl.BlockSpec((NLG, TM, 128), lambda i, b: (0, i, 0))
    grid_spec = pltpu.PrefetchScalarGridSpec(
        num_scalar_prefetch=1, grid=(nt,),
        in_specs=x_specs + [
            tok(AW), tok(RW), rec, rec, tok(SGW), tok(SGW), modspec, modspec,
            full(gmix), full(sgw), full(sgb), full(wout), full(lng), full(lnb), full(rwt), full(rb)],
        out_specs=[tok(D), tok(D), tspec, tspec, pl.BlockSpec((None, NE, 128), lambda i, b: (i, 0, 0))],
        scratch_shapes=[pltpu.VMEM((TM, D), BF16)])
    return pl.pallas_call(
        functools.partial(_merge_kernel, split), grid_spec=grid_spec,
        out_shape=[jax.ShapeDtypeStruct((n, D), F32), jax.ShapeDtypeStruct((n, D), BF16),
                   jax.ShapeDtypeStruct((NE, n), F32), jax.ShapeDtypeStruct((NE, n), F32),
                   jax.ShapeDtypeStruct((nt, NE, 128), F32)],
        compiler_params=pltpu.CompilerParams(dimension_semantics=("parallel",), vmem_limit_bytes=VMEM_LIMIT),
        name="merge_route",
    )(meta["bidx"], *x_parts, oa, gy, hf, hb, u, vn, mod1, mod2, gmix, sgw, sgb, wout, lng, lnb, rwt, rb)


def _plan(cnt, n_mt):
    pc = (cnt + (ALIGN - 1)) // ALIGN * ALIGN
    lo = jnp.cumsum(pc, axis=1) - pc
    tot = jnp.sum(pc, axis=0)
    win = sum(jnp.where(pc > lo_w, hi_w - lo_w, 0) for lo_w, hi_w in zip((0,) + WINDOWS[:-1], WINDOWS))
    seg = (tot + WIN_OVER + (TMX - 1)) // TMX * TMX
    gend = jnp.cumsum(seg)
    gstart = gend - seg
    dst = gstart[None, :] + jnp.cumsum(pc, axis=0) - pc
    tile_row = jnp.arange(n_mt, dtype=jnp.int32) * TMX
    texp = jnp.minimum(jnp.sum((gend[None, :] <= tile_row[:, None]).astype(jnp.int32), axis=1), NE - 1)
    nact = (gend[-1] // TMX).reshape(1)
    grp = lambda a: (a // ALIGN).astype(jnp.int32)
    return dict(pc=grp(pc).reshape(-1), lo=grp(lo).reshape(-1), dst=grp(dst).reshape(-1),
                tsum=grp(jnp.sum(pc, axis=1)), twin=grp(jnp.sum(win, axis=1)),
                zdst=grp(jnp.concatenate([gstart + tot, gend[-1:]])), zlen=grp(seg - tot),
                texp=texp.astype(jnp.int32), nact=nact.astype(jnp.int32))


def _run_copies(src, dst, src_off, dst_off, groups, sem, max_bits):
    def arm(b):
        @pl.when(((groups >> b) & 1) == 1)
        def _():
            off = (groups >> (b + 1)) << (b + 1)
            pltpu.make_async_copy(src.at[pl.ds(src_off + off, 1 << b)], dst.at[pl.ds(dst_off + off, 1 << b)],
                                  sem).start()

    for b in range(min(COMMON_BITS, max_bits)):
        arm(b)
    if max_bits > COMMON_BITS:
        @pl.when(groups >= (1 << COMMON_BITS))
        def _():
            for b in range(COMMON_BITS, max_bits):
                arm(b)


def _wait_groups(buf, groups, sem, max_bits):
    for b in range(max_bits):
        @pl.when(((groups >> b) & 1) == 1)
        def _():
            pltpu.make_async_copy(buf.at[pl.ds(0, 1 << b)], buf.at[pl.ds(0, 1 << b)], sem).wait()


WINDOWS = (64, 128, TD)
WIN_OVER = max(w - p - ALIGN for w, p in zip(WINDOWS, (0,) + WINDOWS[:-1]))
WIN_BITS = (NE * TD // ALIGN).bit_length()
COMMON_BITS = 2
RUN_BITS = (TD // ALIGN).bit_length()
TILE_BITS = (RL // ALIGN).bit_length()


def _dispatch_kernel(pc_ref, lo_ref, dst_ref, ts_ref, zd_ref, zl_ref, mask_ref, gate_ref, h_ref,
                     xs_ref, info_ref, loc_s, zero_s, sems):
    i = pl.program_id(0)
    last_step = pl.num_programs(0) - 1
    slot = i % 2
    mask = mask_ref[...]
    mb = mask.astype(BF16)
    s_i = lax.broadcasted_iota(jnp.int32, (TD, TD), 0)
    t_i = lax.broadcasted_iota(jnp.int32, (TD, TD), 1)
    earlier = jnp.where(s_i < t_i, 1.0, 0.0).astype(BF16)
    rank = jnp.dot(mb, earlier, preferred_element_type=F32)
    e_i = lax.broadcasted_iota(jnp.int32, (NE, NE), 0)
    f_i = lax.broadcasted_iota(jnp.int32, (NE, NE), 1)
    below = jnp.where(f_i < e_i, 1.0, 0.0).astype(BF16)
    lower = jnp.dot(below, mb, preferred_element_type=F32)
    row_e = lax.broadcasted_iota(jnp.int32, (NE, 1), 0)
    lo_vec = jnp.zeros((NE, 1), F32)
    for e in range(NE):
        lo_vec = jnp.where(row_e == e, (lo_ref[i * NE + e] * ALIGN).astype(F32), lo_vec)
    row = lo_vec + rank
    is0 = mask * jnp.where(lower == 0.0, 1.0, 0.0)
    is1 = mask - is0
    d0 = jnp.sum(is0 * row, axis=0, keepdims=True)
    d1 = jnp.sum(is1 * row, axis=0, keepdims=True)
    gate = gate_ref[...]
    w0 = jnp.sum(is0 * gate, axis=0, keepdims=True)
    w1 = jnp.sum(is1 * gate, axis=0, keepdims=True)
    info_ref[...] = jnp.concatenate([d0, d1, w0, w1, jnp.zeros((4, TD), F32)], axis=0)
    r_i = lax.broadcasted_iota(jnp.int32, (RL, TD), 0)
    perm = jnp.where((r_i == d0.astype(jnp.int32)) | (r_i == d1.astype(jnp.int32)), 1.0, 0.0).astype(BF16)
    @pl.when(i == 0)
    def _():
        loc_s[:, RL // ALIGN:] = jnp.zeros((2, WIN_OVER // ALIGN + 1, ALIGN, D), BF16)

    loc = loc_s.at[slot]
    loc[0:RL // ALIGN] = jnp.dot(perm, h_ref[...], preferred_element_type=F32).astype(BF16).reshape(
        RL // ALIGN, ALIGN, D)

    @pl.when(i > 0)
    def _():
        _wait_groups(xs_ref, ts_ref[jnp.maximum(i - 1, 0)], sems.at[1 - slot], WIN_BITS)

    for e in range(NE):
        groups = pc_ref[i * NE + e]

        def window(below, size):
            @pl.when((groups > below // ALIGN) & (groups <= size // ALIGN))
            def _():
                pltpu.make_async_copy(loc.at[pl.ds(lo_ref[i * NE + e], size // ALIGN)],
                                      xs_ref.at[pl.ds(dst_ref[i * NE + e], size // ALIGN)], sems.at[slot]).start()

        window(0, WINDOWS[0])

        @pl.when(groups > WINDOWS[0] // ALIGN)
        def _():
            for below, size in zip(WINDOWS[:-1], WINDOWS[1:]):
                window(below, size)

    @pl.when(i == last_step)
    def _():
        _wait_groups(xs_ref, ts_ref[i], sems.at[slot], WIN_BITS)
        sem = sems.at[0]
        zero_s[...] = jnp.zeros_like(zero_s)
        zg = ZROWS // ALIGN
        for wait in (False, True):
            for e in range(NE):
                for part in range(-(-(TMX + WIN_OVER) // ZROWS)):
                    groups = jnp.clip(zl_ref[e] - part * zg, 0, zg)
                    if wait:
                        _wait_groups(zero_s, groups, sem, zg.bit_length())
                    else:
                        _run_copies(zero_s, xs_ref, 0, zd_ref[e] + part * zg, groups, sem, zg.bit_length())
        tail = zd_ref[NE]
        chunks = (xs_ref.shape[0] - tail) // zg

        def fill(c, carry):
            pltpu.make_async_copy(zero_s, xs_ref.at[pl.ds(tail + c * zg, zg)], sem).start()
            return carry

        def drain(c, carry):
            pltpu.make_async_copy(zero_s, zero_s, sem).wait()
            return carry

        lax.fori_loop(0, chunks, fill, 0)
        lax.fori_loop(0, chunks, drain, 0)


def _dispatch(plan, mask_t, gate_t, h2, rtot):
    n = h2.shape[0]
    ntd = n // TD
    tspec = pl.BlockSpec((NE, TD), lambda i, *_: (0, i))
    grid_spec = pltpu.PrefetchScalarGridSpec(
        num_scalar_prefetch=6, grid=(ntd,),
        in_specs=[tspec, tspec, pl.BlockSpec((TD, D), lambda i, *_: (i, 0))],
        out_specs=[pl.BlockSpec(memory_space=pl.ANY), pl.BlockSpec((8, TD), lambda i, *_: (0, i))],
        scratch_shapes=[pltpu.VMEM((2, (RL + WIN_OVER) // ALIGN + 1, ALIGN, D), BF16),
                        pltpu.VMEM((ZROWS // ALIGN, ALIGN, D), BF16),
                        pltpu.SemaphoreType.DMA((2,))])
    return pl.pallas_call(
        _dispatch_kernel, grid_spec=grid_spec,
        out_shape=[jax.ShapeDtypeStruct((rtot // ALIGN, ALIGN, D), BF16), jax.ShapeDtypeStruct((8, n), F32)],
        compiler_params=pltpu.CompilerParams(dimension_semantics=("arbitrary",), vmem_limit_bytes=VMEM_LIMIT),
        name="dispatch",
    )(plan["pc"], plan["lo"], plan["dst"], plan["twin"], plan["zdst"], plan["zlen"], mask_t, gate_t, h2)


def _expert_kernel(texp_ref, nact_ref, x_ref, w1_ref, w3_ref, w2_ref, y_ref, w1_s, w3_s, w2_s):
    m = pl.program_id(0)
    new_expert = (m == 0) | (texp_ref[m] != texp_ref[jnp.maximum(m - 1, 0)])

    @pl.when(new_expert)
    def _():
        w1_s[...] = w1_ref[...].astype(BF16)
        w3_s[...] = w3_ref[...].astype(BF16)
        w2_s[...] = w2_ref[...].astype(BF16)

    @pl.when(m < nact_ref[0])
    def _():
        x = x_ref[...]
        a = jnp.dot(x, w1_s[...], preferred_element_type=F32)
        a = a * _sigmoid(a) * jnp.dot(x, w3_s[...], preferred_element_type=F32)
        y_ref[...] = jnp.dot(a.astype(BF16), w2_s[...], preferred_element_type=F32).astype(BF16)

    @pl.when(m >= nact_ref[0])
    def _():
        y_ref[...] = jnp.zeros_like(y_ref)


def _experts(plan, xs, layer, w1, w3, w2):
    rtot = xs.shape[0]
    n_mt = rtot // TMX
    grid_spec = pltpu.PrefetchScalarGridSpec(
        num_scalar_prefetch=2, grid=(n_mt,),
        in_specs=[pl.BlockSpec((TMX, D), lambda m, te, na: (jnp.minimum(m, na[0] - 1), 0)),
                  pl.BlockSpec((None, None, D, FF), lambda m, te, na: (layer, te[m], 0, 0)),
                  pl.BlockSpec((None, None, D, FF), lambda m, te, na: (layer, te[m], 0, 0)),
                  pl.BlockSpec((None, None, FF, D), lambda m, te, na: (layer, te[m], 0, 0))],
        out_specs=pl.BlockSpec((TMX, D), lambda m, te, na: (m, 0)),
        scratch_shapes=[pltpu.VMEM((D, FF), BF16), pltpu.VMEM((D, FF), BF16), pltpu.VMEM((FF, D), BF16)])
    return pl.pallas_call(
        _expert_kernel, grid_spec=grid_spec,
        out_shape=jax.ShapeDtypeStruct((rtot, D), BF16),
        compiler_params=pltpu.CompilerParams(dimension_semantics=("arbitrary",), vmem_limit_bytes=VMEM_LIMIT),
        name="experts",
    )(plan["texp"], plan["nact"], xs, w1, w3, w2)


def _combine_kernel(split, bidx_ref, pc_ref, lo_ref, dst_ref, ts_ref, info_ref, ys_ref, x1_ref, mod_ref,
                    lng_ref, lnb_ref, *rest):
    del bidx_ref
    o_refs, (loc_s, sems) = rest[:-2], rest[-2:]
    i = pl.program_id(0)
    slot = i % 2

    def fetch(tile, buf):
        for e in range(NE):
            _run_copies(ys_ref, loc_s.at[buf], dst_ref[tile * NE + e], lo_ref[tile * NE + e],
                        pc_ref[tile * NE + e], sems.at[buf], RUN_BITS)

    @pl.when(i == 0)
    def _():
        loc_s[...] = jnp.zeros_like(loc_s)
        fetch(0, 0)

    @pl.when(i + 1 < pl.num_programs(0))
    def _():
        fetch(i + 1, 1 - slot)

    _wait_groups(loc_s.at[slot], ts_ref[i], sems.at[slot], TILE_BITS)
    info = info_ref[...]
    d0 = info[0:1].astype(jnp.int32)
    d1 = info[1:2].astype(jnp.int32)
    r_i = lax.broadcasted_iota(jnp.int32, (RL, TD), 0)
    wperm = (jnp.where(r_i == d0, info[2:3], 0.0) + jnp.where(r_i == d1, info[3:4], 0.0)).astype(BF16)
    moe = lax.dot_general(wperm, loc_s[slot].reshape(RL, D), (((0,), (0,)), ((), ())),
                          preferred_element_type=F32)
    gate = mod_ref[...][:, 2 * D:]
    out = _ln(ALPHA * x1_ref[...] + gate * moe) * lng_ref[...] + lnb_ref[...]
    if len(o_refs) == 1:
        o_refs[0][...] = out
    else:
        @pl.when(i < split)
        def _():
            o_refs[0][...] = out

        @pl.when(i >= split)
        def _():
            o_refs[1][...] = out


def _combine(bidx_d, plan, info, ys, x1, mod2, lng, lnb, out_rows):
    n = x1.shape[0]
    ntd = n // TD
    tok = pl.BlockSpec((TD, D), lambda i, *_: (i, 0))
    full = lambda a: pl.BlockSpec(a.shape, lambda i, *_: (0,) * a.ndim)
    if len(out_rows) == 1:
        split, out_specs = 0, [tok]
    else:
        split = out_rows[0] // TD
        out_specs = [pl.BlockSpec((TD, D), lambda i, *_: (jnp.minimum(i, split - 1), 0)),
                     pl.BlockSpec((TD, D), lambda i, *_: (jnp.maximum(i - split, 0), 0))]
    grid_spec = pltpu.PrefetchScalarGridSpec(
        num_scalar_prefetch=5, grid=(ntd,),
        in_specs=[pl.BlockSpec((8, TD), lambda i, *_: (0, i)),
                  pl.BlockSpec(memory_space=pl.ANY),
                  tok,
                  pl.BlockSpec((None, 1, 3 * D), lambda i, b, *_: (b[i], 0, 0)),
                  full(lng), full(lnb)],
        out_specs=out_specs,
        scratch_shapes=[pltpu.VMEM((2, RL // ALIGN, ALIGN, D), BF16), pltpu.SemaphoreType.DMA((2,))])
    return pl.pallas_call(
        functools.partial(_combine_kernel, split), grid_spec=grid_spec,
        out_shape=[jax.ShapeDtypeStruct((r, D), F32) for r in out_rows],
        compiler_params=pltpu.CompilerParams(dimension_semantics=("arbitrary",), vmem_limit_bytes=VMEM_LIMIT),
        name="combine",
    )(bidx_d, plan["pc"], plan["lo"], plan["dst"], plan["tsum"], info, ys, x1, mod2, lng, lnb)


def _moe(meta, h2, mask_t, gate_t, cnt, layer, w1, w3, w2, x1, mod2, lng, lnb, out_rows):
    n = h2.shape[0]
    ntd = n // TD
    rtot = -(-(ntd * RL + NE * (WIN_OVER + ALIGN)) // TMX) * TMX + NE * TMX
    cnt = cnt[:, :, :TM // TD].transpose(0, 2, 1).reshape(ntd, NE).astype(jnp.int32)
    plan = _plan(cnt, rtot // TMX)
    xs, info = _dispatch(plan, mask_t, gate_t, h2, rtot)
    ys = _experts(plan, xs.reshape(rtot, D), layer, w1, w3, w2).reshape(rtot // ALIGN, ALIGN, D)
    return _combine(meta["bidx_d"], plan, info, ys, x1, mod2, lng, lnb, out_rows)


def _tile_meta(groups):
    bidx, pblk, first, last, cf, cb, bidx_d = [], [], [], [], [], [], []
    row = 0
    tile = 0
    for (b, s) in groups:
        per = s // TM
        for bi in range(b):
            for j in range(per):
                bidx.append(row + bi)
                pblk.append(j)
                first.append(1 if j == 0 else 0)
                last.append(1 if j == per - 1 else 0)
                cf.append(tile + j)
                cb.append(tile + per - 1 - j)
            tile += per
            bidx_d += [row + bi] * (s // TD)
        row += b
    as_i32 = lambda v: jnp.asarray(np.asarray(v, np.int32))
    return dict(bidx=as_i32(bidx), pblk=as_i32(pblk), first=as_i32(first), last=as_i32(last),
                cf=as_i32(cf), cb=as_i32(cb), bidx_d=as_i32(bidx_d))


def _rope_tables(s_max):
    inv = jnp.power(ROPE_THETA, -jnp.arange(0, HD, 2, dtype=F32) / HD)
    ang = jnp.arange(s_max, dtype=F32)[:, None] * inv[None, :]
    cos, sin = jnp.cos(ang), jnp.sin(ang)
    cos128 = jnp.tile(cos, (1, 4))
    sin128 = jnp.tile(jnp.concatenate([-sin, sin], axis=1), (1, 2))
    return cos128, sin128


def _block_diag(w):
    d, nb, c, f = w.shape
    eye = jnp.eye(nb, dtype=w.dtype)
    return jnp.einsum('dncf,nm->dncmf', w, eye).reshape(d, nb * c, nb * f)


def _forward(xs, cs, w_mod, b_mod, w_in, attn_sink, conv_w, conv_b, lru_w_r, lru_b_r, lru_w_i, lru_b_i,
             lru_lambda, sg_norm_g, sg_w, sg_b, mix_norm_g, w_out, ln_g, ln_b, router_w, router_bias,
             exp_w1, exp_w3, exp_w2):
    groups = [(x.shape[0], x.shape[1]) for x in xs]
    assert len(groups) == 2
    for (_, s) in groups:
        assert s % TM == 0
    meta = _tile_meta(groups)
    x_parts = [xx.reshape(-1, D) for xx in xs]
    group_rows = [p.shape[0] for p in x_parts]
    c_all = jnp.concatenate(cs, axis=0)
    bt = c_all.shape[0]
    mods = _modulation(c_all, w_mod, b_mod).reshape(2 * DEPTH, bt, 1, 3 * D)
    cos_t, sin_t = _rope_tables(max(s for _, s in groups))
    rwt = router_w.T.astype(BF16)
    rb = router_bias.reshape(NE, 1)
    for l in range(DEPTH):
        q, kk, vv, gy, xr, u, vn = _pre_mixer(meta, x_parts, mods[2 * l], w_in[l].astype(BF16), cos_t, sin_t,
                                              sg_norm_g[l].reshape(1, SGW))
        oa = _attention(meta, attn_sink[l], q, kk, vv)
        hf, hb = _recurrent(meta, xr, conv_w[l], conv_b[l].reshape(1, RW),
                            _block_diag(0.5 * lru_w_r[l]).astype(BF16), _block_diag(0.5 * lru_w_i[l]).astype(BF16),
                            0.5 * lru_b_r[l], 0.5 * lru_b_i[l], lru_lambda[l])
        sgb = jnp.repeat(sg_b[l].T, HD, axis=1)
        x1, h2, gate_t, mask_t, cnt = _merge(meta, x_parts, oa, gy, hf, hb, u, vn, mods[2 * l], mods[2 * l + 1],
                                             mix_norm_g[l].reshape(1, D), sg_w[l].astype(BF16), sgb,
                                             w_out[l].astype(BF16), ln_g[l, 0].reshape(1, D),
                                             ln_b[l, 0].reshape(1, D), rwt, rb)
        out_rows = group_rows if l == DEPTH - 1 else [sum(group_rows)]
        x_parts = _moe(meta, h2, mask_t, gate_t, cnt, l, exp_w1, exp_w3, exp_w2, x1, mods[2 * l + 1],
                       ln_g[l, 1].reshape(1, D), ln_b[l, 1].reshape(1, D), out_rows)
    return tuple(p.reshape(b, s, D) for p, (b, s) in zip(x_parts, groups))


def kernel(x_prompt, x_sample, c_prompt, c_sample, w_mod, b_mod, w_in, attn_sink, conv_w, conv_b, lru_w_r, lru_b_r,
           lru_w_i, lru_b_i, lru_lambda, sg_norm_g, sg_w, sg_b, mix_norm_g, w_out, ln_g, ln_b, router_w,
           router_bias, exp_w1, exp_w3, exp_w2):
    return _forward([x_prompt, x_sample], [c_prompt, c_sample], w_mod, b_mod, w_in, attn_sink, conv_w, conv_b,
                    lru_w_r, lru_b_r, lru_w_i, lru_b_i, lru_lambda, sg_norm_g, sg_w, sg_b, mix_norm_g, w_out,
                    ln_g, ln_b, router_w, router_bias, exp_w1, exp_w3, exp_w2)
```

```python
import functools

import numpy as np
import jax
import jax.numpy as jnp
from jax import lax
from jax.experimental import pallas as pl
from jax.experimental.pallas import tpu as pltpu

F32 = jnp.float32
BF16 = jnp.bfloat16

D = 1024
DEPTH = 2
HD = 64
AW = 384
KVW = 128
RW = 384
SGW = 256
WINDOW = 128
CONV_WIDTH = 4
LRU_C = 8.0
CHUNK = 128
NE = 16
NG = 4
EPG = NE // NG
FF = 512
ALPHA = (2 * DEPTH) ** 0.25
LN_EPS = 1e-5
RMS_EPS = 1e-6
ROPE_THETA = 10000.0

TM = 512
QB = 128
SEG = TM // 8
NLG = RW // 128
TD = 512
ALIGN = 16
RL = 2 * TD + NE * ALIGN
TMX = 1024
ZROWS = 256
NEG = -1e30
VMEM_LIMIT = 48 * 1024 * 1024

C_Q, C_K, C_V, C_Y, C_R, C_U, C_SV, C_END = 0, 384, 512, 640, 1024, 1408, 1664, 1920


def _ln(x):
    mu = jnp.mean(x, axis=-1, keepdims=True)
    xc = x - mu
    var = jnp.mean(xc * xc, axis=-1, keepdims=True)
    return xc * lax.rsqrt(var + LN_EPS)


def _rms(x):
    return x * lax.rsqrt(jnp.mean(x * x, axis=-1, keepdims=True) + RMS_EPS)


def _sigmoid(z):
    return 0.5 * (jnp.tanh(0.5 * z) + 1.0)


def _gelu(x):
    return 0.5 * x * (1.0 + jnp.tanh(0.7978845608028654 * (x + 0.044715 * (x * x * x))))


def _split_bf16(a):
    hi = a.astype(BF16)
    lo = (a - hi.astype(F32)).astype(BF16)
    return hi, lo


def _mod_kernel(c_ref, w_ref, b_ref, o_ref):
    c = c_ref[...]
    ca = c * _sigmoid(c)
    ch, cl = _split_bf16(ca)
    wh, wl = _split_bf16(w_ref[...])
    acc = jnp.dot(ch, wh, preferred_element_type=F32)
    acc += jnp.dot(ch, wl, preferred_element_type=F32)
    acc += jnp.dot(cl, wh, preferred_element_type=F32)
    o_ref[...] = acc + b_ref[...]


def _modulation(c_all, w_mod, b_mod):
    bt = c_all.shape[0]
    cb = 768
    w = w_mod.reshape(2 * DEPTH, D, 3 * D)
    b = b_mod.reshape(2 * DEPTH, 1, 3 * D)
    return pl.pallas_call(
        _mod_kernel,
        grid=(2 * DEPTH, 3 * D // cb),
        in_specs=[pl.BlockSpec((bt, D), lambda s, j: (0, 0)),
                  pl.BlockSpec((None, D, cb), lambda s, j: (s, 0, j)),
                  pl.BlockSpec((None, 1, cb), lambda s, j: (s, 0, j))],
        out_specs=pl.BlockSpec((None, bt, cb), lambda s, j: (s, 0, j)),
        out_shape=jax.ShapeDtypeStruct((2 * DEPTH, bt, 3 * D), F32),
        name="modulation",
    )(c_all, w, b)


def _load_tokens(i, x_refs, split):
    if len(x_refs) == 1:
        return x_refs[0][...]
    return jnp.where(i < split, x_refs[0][...], x_refs[1][...])


def _token_specs(x_parts, rows):
    if len(x_parts) == 1:
        return [pl.BlockSpec((rows, D), lambda i, *_: (i, 0))], 0
    split = x_parts[0].shape[0] // rows
    return [pl.BlockSpec((rows, D), lambda i, *_: (jnp.minimum(i, split - 1), 0)),
            pl.BlockSpec((rows, D), lambda i, *_: (jnp.maximum(i - split, 0), 0))], split


def _pre_kernel(split, bidx_ref, pblk_ref, *refs):
    del bidx_ref, pblk_ref
    nx = len(refs) - 12
    x_refs = refs[:nx]
    mod_ref, w_ref, cos_ref, sin_ref, sgg_ref, q_ref, kk_ref, vv_ref, gy_ref, xr_ref, u_ref, vn_ref = refs[nx:]
    mod = mod_ref[...]
    shift, scale = mod[:, :D], mod[:, D:2 * D]
    h = (_ln(_load_tokens(pl.program_id(0), x_refs, split)) * (1.0 + scale) + shift).astype(BF16)
    cos = cos_ref[...]
    sin = sin_ref[...]
    lane = lax.broadcasted_iota(jnp.int32, (1, 128), 1)
    first_half = (lane % HD) < (HD // 2)

    def rope128(xg):
        rot = jnp.where(first_half, pltpu.roll(xg, 128 - HD // 2, 1), pltpu.roll(xg, HD // 2, 1))
        return xg * cos + rot * sin

    q = jnp.dot(h, w_ref[:, C_Q:C_K], preferred_element_type=F32)
    for g in range(AW // 128):
        q_ref[:, g * 128:(g + 1) * 128] = (rope128(q[:, g * 128:(g + 1) * 128]) * (HD ** -0.5)).astype(BF16)
    kv = jnp.dot(h, w_ref[:, C_K:C_Y], preferred_element_type=F32)
    lo_half = lane < HD
    for src, dst_ref in ((rope128(kv[:, :KVW]), kk_ref), (kv[:, KVW:], vv_ref)):
        swapped = pltpu.roll(src, HD, 1)
        dst_ref[:, 0:128] = jnp.where(lo_half, src, swapped).astype(BF16)
        dst_ref[:, 128:256] = jnp.where(lo_half, swapped, src).astype(BF16)
    gy_ref[...] = _gelu(jnp.dot(h, w_ref[:, C_Y:C_R], preferred_element_type=F32)).astype(gy_ref.dtype)
    xr = jnp.dot(h, w_ref[:, C_R:C_U], preferred_element_type=F32)
    for g in range(NLG):
        xr_ref[g] = xr[:, g * 128:(g + 1) * 128]
    u_ref[...] = _gelu(jnp.dot(h, w_ref[:, C_U:C_SV], preferred_element_type=F32)).astype(u_ref.dtype)
    sv = _gelu(jnp.dot(h, w_ref[:, C_SV:C_END], preferred_element_type=F32))
    vn_ref[...] = (_ln(sv) * sgg_ref[...]).astype(BF16)


def _pre_mixer(meta, x_parts, mod_l, w_ext, cos_t, sin_t, sgg):
    n = sum(p.shape[0] for p in x_parts)
    nt = n // TM
    tok = lambda w: pl.BlockSpec((TM, w), lambda i, b, p: (i, 0))
    full = lambda a: pl.BlockSpec(a.shape, lambda i, b, p: (0,) * a.ndim)
    x_specs, split = _token_specs(x_parts, TM)
    grid_spec = pltpu.PrefetchScalarGridSpec(
        num_scalar_prefetch=2, grid=(nt,),
        in_specs=x_specs + [
            pl.BlockSpec((None, 1, 3 * D), lambda i, b, p: (b[i], 0, 0)),
            full(w_ext),
            pl.BlockSpec((TM, 128), lambda i, b, p: (p[i], 0)),
            pl.BlockSpec((TM, 128), lambda i, b, p: (p[i], 0)),
            full(sgg)],
        out_specs=[tok(AW), tok(256), tok(256), tok(RW),
                   pl.BlockSpec((NLG, TM, 128), lambda i, b, p: (0, i, 0)), tok(SGW), tok(SGW)])
    sds = lambda w, dt: jax.ShapeDtypeStruct((n, w), dt)
    return pl.pallas_call(
        functools.partial(_pre_kernel, split), grid_spec=grid_spec,
        out_shape=[sds(AW, BF16), sds(256, BF16), sds(256, BF16), sds(RW, BF16),
                   jax.ShapeDtypeStruct((NLG, n, 128), F32), sds(SGW, BF16), sds(SGW, BF16)],
        compiler_params=pltpu.CompilerParams(dimension_semantics=("parallel",), vmem_limit_bytes=VMEM_LIMIT),
        name="pre_mixer",
    )(meta["bidx"], meta["pblk"], *x_parts, mod_l, w_ext, cos_t, sin_t, sgg)


def _attn_kernel(first_ref, last_ref, sink_ref, q_ref, k_ref, kp_ref, kn_ref, v_ref, vp_ref, vn_ref,
                 o_ref, kw_s, vw_s):
    i = pl.program_id(0)
    is_first = first_ref[i] == 1
    is_last = last_ref[i] == 1
    kw_s[0:QB] = kp_ref[...]
    kw_s[QB:QB + TM] = k_ref[...]
    kw_s[QB + TM:] = kn_ref[...]
    vw_s[0:QB] = vp_ref[...]
    vw_s[QB:QB + TM] = v_ref[...]
    vw_s[QB + TM:] = vn_ref[...]
    iq = lax.broadcasted_iota(jnp.int32, (QB, 3 * QB), 0)
    ik = lax.broadcasted_iota(jnp.int32, (QB, 3 * QB), 1)
    rel = ik - iq
    band = (rel >= 0) & (rel <= 2 * WINDOW)
    lane = lax.broadcasted_iota(jnp.int32, (1, 128), 1)
    lo_half = lane < HD
    zero = jnp.zeros((), BF16)
    gsz = AW // HD // 2
    row_head = lax.broadcasted_iota(jnp.int32, (gsz * QB, 1), 0) // QB
    for j in range(TM // QB):
        ok = band
        if j == 0:
            ok = ok & ((ik >= QB) | jnp.logical_not(is_first))
        if j == TM // QB - 1:
            ok = ok & ((ik < 2 * QB) | jnp.logical_not(is_last))
        bias = jnp.where(ok, 0.0, NEG)
        bias = jnp.concatenate([bias] * gsz, axis=0)
        rows = slice(j * QB, (j + 1) * QB)
        keys = slice(j * QB, j * QB + 3 * QB)
        outs = [None] * (AW // HD)
        for g in range(2):
            heads = range(g * gsz, (g + 1) * gsz)
            qs = []
            for hh in heads:
                qg = q_ref[rows, (hh // 2) * 128:(hh // 2 + 1) * 128]
                qs.append(jnp.where(lo_half if hh % 2 == 0 else jnp.logical_not(lo_half), qg, zero))
            s = lax.dot_general(jnp.concatenate(qs, axis=0), kw_s[keys, g * 128:(g + 1) * 128],
                                (((1,), (1,)), ((), ())), preferred_element_type=F32) + bias
            sink = jnp.full((gsz * QB, 1), sink_ref[heads[-1]], F32)
            for t in range(gsz - 1):
                sink = jnp.where(row_head == t, sink_ref[heads[t]], sink)
            m = jnp.maximum(jnp.max(s, axis=-1, keepdims=True), sink)
            p = jnp.exp(s - m)
            denom = jnp.sum(p, axis=-1, keepdims=True) + jnp.exp(sink - m)
            pv = jnp.dot(p.astype(BF16), vw_s[keys, g * 128:(g + 1) * 128], preferred_element_type=F32)
            o = pv * (1.0 / denom)
            for t, hh in enumerate(heads):
                outs[hh] = o[t * QB:(t + 1) * QB]
        for jg in range(AW // 128):
            o_ref[rows, jg * 128:(jg + 1) * 128] = jnp.where(lo_half, outs[2 * jg], outs[2 * jg + 1]).astype(o_ref.dtype)


def _attention(meta, sink, q, kk, vv):
    n = q.shape[0]
    nt = n // TM
    nqb = n // QB
    r = TM // QB
    main = lambda w: pl.BlockSpec((TM, w), lambda i, f, l, s: (i, 0))
    prev = pl.BlockSpec((QB, 256), lambda i, f, l, s: (jnp.maximum(i * r - 1, 0), 0))
    nxt = pl.BlockSpec((QB, 256), lambda i, f, l, s: (jnp.minimum(i * r + r, nqb - 1), 0))
    grid_spec = pltpu.PrefetchScalarGridSpec(
        num_scalar_prefetch=3, grid=(nt,),
        in_specs=[main(AW), main(256), prev, nxt, main(256), prev, nxt],
        out_specs=main(AW),
        scratch_shapes=[pltpu.VMEM((TM + 2 * QB, 256), BF16), pltpu.VMEM((TM + 2 * QB, 256), BF16)])
    return pl.pallas_call(
        _attn_kernel, grid_spec=grid_spec,
        out_shape=jax.ShapeDtypeStruct((n, AW), BF16),
        compiler_params=pltpu.CompilerParams(dimension_semantics=("parallel",), vmem_limit_bytes=VMEM_LIMIT),
        name="attention",
    )(meta["first"], meta["last"], sink, q, kk, kk, kk, vv, vv, vv)


def _rec_kernel(cf_ref, cb_ref, first_ref, last_ref,
                x_hbm, xfp_ref, xfn_ref, xbp_ref, xbn_ref,
                cw_ref, cbias_ref, wr_ref, wi_ref, br_ref, bi_ref, lam_ref,
                hf_hbm, hb_hbm,
                ext_s, a_s, b_s, hs_s, ps_s, out_s, carry_s, gsem, ssem):
    i = pl.program_id(0)
    nsteps = pl.num_programs(0)
    slot = i % 2
    cw = cw_ref[...]
    cbias = cbias_ref[...]
    sub = lax.broadcasted_iota(jnp.int32, (8, 128), 0)
    unroll = 8
    chunk_refs = (cf_ref, cb_ref)
    out_hbm = (hf_hbm, hb_hbm)

    def gather(step, buf, d, start):
        c = chunk_refs[d][step]
        for g in range(NLG):
            for s in range(8):
                cp = pltpu.make_async_copy(x_hbm.at[g, pl.ds(pl.multiple_of(c * TM + s * SEG, SEG), SEG), :],
                                           ext_s.at[buf, d, g, pl.ds(2, SEG), s, :], gsem.at[buf, d])
                if start:
                    cp.start()
                else:
                    cp.wait()

    def scatter(d, c, start):
        for g in range(NLG):
            for s in range(8):
                cp = pltpu.make_async_copy(out_s.at[d, g, :, s, :],
                                           out_hbm[d].at[g, pl.ds(pl.multiple_of(c * TM + s * SEG, SEG), SEG), :],
                                           ssem.at[d])
                if start:
                    cp.start()
                else:
                    cp.wait()

    @pl.when(i == 0)
    def _():
        gather(0, 0, 0, True)
        gather(0, 0, 1, True)

    @pl.when(i + 1 < nsteps)
    def _():
        gather(i + 1, 1 - slot, 0, True)
        gather(i + 1, 1 - slot, 1, True)

    def run(d, xp_ref, xn_ref):
        c = chunk_refs[d][i]
        first = first_ref[c] == 1
        last = last_ref[c] == 1
        gather(i, slot, d, False)
        ext = ext_s.at[slot, d]
        planes = []
        for g in range(NLG):
            prev = jnp.where(first, 0.0, xp_ref[g])
            nxt = jnp.where(last, 0.0, xn_ref[g])
            ext[g, 0] = jnp.where(sub == 0, prev[6:7], pltpu.roll(ext[g, SEG], 1, 0))
            ext[g, 1] = jnp.where(sub == 0, prev[7:8], pltpu.roll(ext[g, SEG + 1], 1, 0))
            ext[g, SEG + 2] = jnp.where(sub == 7, nxt[0:1], pltpu.roll(ext[g, 2], 7, 0))
            lanes = slice(g * 128, (g + 1) * 128)
            acc = ext[g, 0:SEG] * cw[0:1, lanes]
            for t in range(1, CONV_WIDTH):
                acc = acc + ext[g, t:t + SEG] * cw[t:t + 1, lanes]
            planes.append(acc.reshape(TM, 128))
        xc = jnp.concatenate(planes, axis=1) + cbias
        xb16 = xc.astype(BF16)
        tr = jnp.tanh(jnp.dot(xb16, wr_ref[d], preferred_element_type=F32) + br_ref[d:d + 1])
        ti = jnp.tanh(jnp.dot(xb16, wi_ref[d], preferred_element_type=F32) + bi_ref[d:d + 1])
        nlam = -lam_ref[d:d + 1]
        softplus = jnp.maximum(nlam, 0.0) + jnp.log1p(jnp.exp(-jnp.abs(nlam)))
        half_c = (-0.5 * LRU_C) * softplus
        log_a = half_c * tr + half_c
        a = jnp.exp(log_a)
        v = (1.0 + a * a) * jnp.tanh(-log_a)
        b = jnp.where(v > 0.0, v * lax.rsqrt(v), 0.0) * (ti + 1.0) * (0.5 * xc)
        for g in range(NLG):
            a_s[g] = a[:, g * 128:(g + 1) * 128].reshape(SEG, 8, 128)
            b_s[g] = b[:, g * 128:(g + 1) * 128].reshape(SEG, 8, 128)

        reverse = d == 1
        reset = last if reverse else first

        @pl.when(reset)
        def _():
            carry_s[d] = jnp.zeros((NLG, 128), F32)

        def step(jo, hp):
            hp = list(hp)
            for ji in range(unroll):
                jj = jo * unroll + ji
                j = SEG - 1 - jj if reverse else jj
                for g in range(NLG):
                    h, p = hp[g]
                    ag = a_s[g, j]
                    h = ag * h + b_s[g, j]
                    p = p * ag
                    hs_s[g, j] = h
                    ps_s[g, j] = p
                    hp[g] = (h, p)
            return tuple(hp)

        init = tuple((jnp.zeros((8, 128), F32), jnp.ones((8, 128), F32)) for _ in range(NLG))
        ends = lax.fori_loop(0, SEG // unroll, step, init)
        cmats = []
        for g in range(NLG):
            e, pe = ends[g]
            c_in = carry_s[d, g:g + 1]
            rows = [None] * 8
            for s in (range(7, -1, -1) if reverse else range(8)):
                rows[s] = c_in
                c_in = e[s:s + 1] + pe[s:s + 1] * c_in
            carry_s[d, g:g + 1] = c_in
            cmats.append(jnp.concatenate(rows, axis=0))

        @pl.when(i > 0)
        def _():
            scatter(d, 0, False)

        def fix(jo, carry):
            for ji in range(unroll):
                j = jo * unroll + ji
                for g in range(NLG):
                    out_s[d, g, j] = hs_s[g, j] + ps_s[g, j] * cmats[g]
            return carry

        lax.fori_loop(0, SEG // unroll, fix, 0)
        scatter(d, c, True)

    run(0, xfp_ref, xfn_ref)
    run(1, xbp_ref, xbn_ref)

    @pl.when(i == nsteps - 1)
    def _():
        scatter(0, 0, False)
        scatter(1, 0, False)


def _recurrent(meta, xr, cw, cbias, wr, wi, br, bi, lam):
    n = xr.shape[1]
    nt = n // TM
    n8 = n // 8
    r8 = TM // 8

    def halos(which):
        sel = (lambda cf, cb: cf) if which == 0 else (lambda cf, cb: cb)
        prev = pl.BlockSpec((NLG, 8, 128),
                            lambda i, cf, cb, f, l: (0, jnp.maximum(sel(cf, cb)[i] * r8 - 1, 0), 0))
        nxt = pl.BlockSpec((NLG, 8, 128),
                           lambda i, cf, cb, f, l: (0, jnp.minimum(sel(cf, cb)[i] * r8 + r8, n8 - 1), 0))
        return prev, nxt

    full = lambda a: pl.BlockSpec(a.shape, lambda i, cf, cb, f, l: (0,) * a.ndim)
    anywhere = pl.BlockSpec(memory_space=pl.ANY)
    plane = lambda lead: pltpu.VMEM(lead + (NLG, SEG, 8, 128), F32)
    grid_spec = pltpu.PrefetchScalarGridSpec(
        num_scalar_prefetch=4, grid=(nt,),
        in_specs=[anywhere, *halos(0), *halos(1),
                  full(cw), full(cbias), full(wr), full(wi), full(br), full(bi), full(lam)],
        out_specs=[anywhere, anywhere],
        scratch_shapes=[pltpu.VMEM((2, 2, NLG, SEG + 3, 8, 128), F32),
                        plane(()), plane(()), plane(()), plane(()), plane((2,)),
                        pltpu.VMEM((2, NLG, 128), F32),
                        pltpu.SemaphoreType.DMA((2, 2)), pltpu.SemaphoreType.DMA((2,))])
    return pl.pallas_call(
        _rec_kernel, grid_spec=grid_spec,
        out_shape=[jax.ShapeDtypeStruct((NLG, n, 128), F32), jax.ShapeDtypeStruct((NLG, n, 128), F32)],
        compiler_params=pltpu.CompilerParams(dimension_semantics=("arbitrary",), vmem_limit_bytes=VMEM_LIMIT),
        name="recurrent",
    )(meta["cf"], meta["cb"], meta["first"], meta["last"], xr, xr, xr, xr, xr, cw, cbias, wr, wi, br, bi, lam)


def _route(sel, score):
    one = jnp.ones_like(sel[0])
    zero = jnp.zeros_like(sel[0])

    def before(vk, vj, k, j):
        return (vk > vj) | ((vk == vj) & (k < j)) if k < j else (vk > vj)

    in_top = []
    gscore = []
    for g in range(NG):
        ids = range(g * EPG, (g + 1) * EPG)
        gs = zero
        for j in ids:
            rank = zero
            for k in ids:
                if k != j:
                    rank = rank + jnp.where(before(sel[k], sel[j], k, j), one, zero)
            m = jnp.where(rank < 2.0, one, zero)
            in_top.append(m)
            gs = gs + m * sel[j]
        gscore.append(gs)
    mask = []
    for g in range(NG):
        worse = zero
        for k in range(NG):
            if k != g:
                worse = worse + jnp.where(before(gscore[k], gscore[g], k, g), one, zero)
        best = jnp.where(worse < 1.0, one, zero)
        for j in range(g * EPG, (g + 1) * EPG):
            mask.append(in_top[j] * best)
    total = zero
    for e in range(NE):
        total = total + mask[e] * score[e]
    inv = 1.0 / total
    gate = [mask[e] * score[e] * inv for e in range(NE)]
    return mask, gate


def _merge_kernel(split, bidx_ref, *refs):
    del bidx_ref
    nx = len(refs) - 22
    x_refs = refs[:nx]
    (oa_ref, gy_ref, hf_ref, hb_ref, u_ref, vn_ref, mod1_ref, mod2_ref, gmix_ref, sgw_ref, sgb_ref, wout_ref,
     lng_ref, lnb_ref, rwt_ref, rb_ref, x1_ref, h2_ref, gate_ref, mask_ref, cnt_ref, mrg_s) = refs[nx:]
    gmix = gmix_ref[...]
    mrg_s[:, 0:AW] = (_rms(oa_ref[...].astype(F32)) * gmix[:, 0:AW]).astype(BF16)
    hsum = jnp.concatenate([hf_ref[g] + hb_ref[g] for g in range(NLG)], axis=1)
    o_rec = gy_ref[...].astype(F32) * hsum
    mrg_s[:, AW:AW + RW] = (_rms(o_rec) * gmix[:, AW:AW + RW]).astype(BF16)
    lane = lax.broadcasted_iota(jnp.int32, (1, 128), 1)
    lo_half = lane < HD
    zero = jnp.zeros((), BF16)
    pieces = []
    for c in range(TM // CHUNK):
        rows = slice(c * CHUNK, (c + 1) * CHUNK)
        grp = []
        for g in range(SGW // 128):
            vg = vn_ref[rows, g * 128:(g + 1) * 128]
            mixed = jnp.dot(sgw_ref[2 * g], jnp.where(lo_half, vg, zero), preferred_element_type=F32)
            mixed += jnp.dot(sgw_ref[2 * g + 1], jnp.where(lo_half, zero, vg), preferred_element_type=F32)
            grp.append(mixed)
        mixed = jnp.concatenate(grp, axis=1) + sgb_ref[...]
        pieces.append(u_ref[rows, :].astype(F32) * mixed)
    o_sg = jnp.concatenate(pieces, axis=0)
    mrg_s[:, AW + RW:] = (_rms(o_sg) * gmix[:, AW + RW:]).astype(BF16)
    o = jnp.dot(mrg_s[...], wout_ref[...], preferred_element_type=F32)
    gate1 = mod1_ref[...][:, 2 * D:]
    x_in = _load_tokens(pl.program_id(0), x_refs, split)
    x1 = _ln(ALPHA * x_in + gate1 * o) * lng_ref[...] + lnb_ref[...]
    x1_ref[...] = x1
    mod2 = mod2_ref[...]
    h2 = (_ln(x1) * (1.0 + mod2[:, D:2 * D]) + mod2[:, :D]).astype(BF16)
    h2_ref[...] = h2
    logits = lax.dot_general(rwt_ref[...], h2, (((1,), (1,)), ((), ())), preferred_element_type=F32)
    score = _sigmoid(logits)
    sel = score + rb_ref[...]
    mask, gate = _route([sel[e:e + 1] for e in range(NE)], [score[e:e + 1] for e in range(NE)])
    gate_ref[...] = jnp.concatenate(gate, axis=0)
    mask_t = jnp.concatenate(mask, axis=0)
    mask_ref[...] = mask_t
    t_i = lax.broadcasted_iota(jnp.int32, (TM, 128), 0)
    j_i = lax.broadcasted_iota(jnp.int32, (TM, 128), 1)
    in_tile = jnp.where((t_i >= j_i * TD) & (t_i < (j_i + 1) * TD), 1.0, 0.0).astype(BF16)
    cnt_ref[...] = jnp.dot(mask_t.astype(BF16), in_tile, preferred_element_type=F32)


def _merge(meta, x_parts, oa, gy, hf, hb, u, vn, mod1, mod2, gmix, sgw, sgb, wout, lng, lnb, rwt, rb):
    n = oa.shape[0]
    nt = n // TM
    x_specs, split = _token_specs(x_parts, TM)
    tok = lambda w: pl.BlockSpec((TM, w), lambda i, b: (i, 0))
    full = lambda a: pl.BlockSpec(a.shape, lambda i, b: (0,) * a.ndim)
    modspec = pl.BlockSpec((None, 1, 3 * D), lambda i, b: (b[i], 0, 0))
    tspec = pl.BlockSpec((NE, TM), lambda i, b: (0, i))
    rec = pl.BlockSpec((NLG, TM, 128), lambda i, b: (0, i, 0))
    grid_spec = pltpu.PrefetchScalarGridSpec(
        num_scalar_prefetch=1, grid=(nt,),
        in_specs=x_specs + [
            tok(AW), tok(RW), rec, rec, tok(SGW), tok(SGW), modspec, modspec,
            full(gmix), full(sgw), full(sgb), full(wout), full(lng), full(lnb), full(rwt), full(rb)],
        out_specs=[tok(D), tok(D), tspec, tspec, pl.BlockSpec((None, NE, 128), lambda i, b: (i, 0, 0))],
        scratch_shapes=[pltpu.VMEM((TM, D), BF16)])
    return pl.pallas_call(
        functools.partial(_merge_kernel, split), grid_spec=grid_spec,
        out_shape=[jax.ShapeDtypeStruct((n, D), F32), jax.ShapeDtypeStruct((n, D), BF16),
                   jax.ShapeDtypeStruct((NE, n), F32), jax.ShapeDtypeStruct((NE, n), F32),
                   jax.ShapeDtypeStruct((nt, NE, 128), F32)],
        compiler_params=pltpu.CompilerParams(dimension_semantics=("parallel",), vmem_limit_bytes=VMEM_LIMIT),
        name="merge_route",
    )(meta["bidx"], *x_parts, oa, gy, hf, hb, u, vn, mod1, mod2, gmix, sgw, sgb, wout, lng, lnb, rwt, rb)


def _plan(cnt, n_mt):
    pc = (cnt + (ALIGN - 1)) // ALIGN * ALIGN
    lo = jnp.cumsum(pc, axis=1) - pc
    tot = jnp.sum(pc, axis=0)
    seg = (tot + (TMX - 1)) // TMX * TMX
    gend = jnp.cumsum(seg)
    gstart = gend - seg
    dst = gstart[None, :] + jnp.cumsum(pc, axis=0) - pc
    tile_row = jnp.arange(n_mt, dtype=jnp.int32) * TMX
    texp = jnp.minimum(jnp.sum((gend[None, :] <= tile_row[:, None]).astype(jnp.int32), axis=1), NE - 1)
    nact = (gend[-1] // TMX).reshape(1)
    grp = lambda a: (a // ALIGN).astype(jnp.int32)
    return dict(pc=grp(pc).reshape(-1), lo=grp(lo).reshape(-1), dst=grp(dst).reshape(-1),
                tsum=grp(jnp.sum(pc, axis=1)),
                zdst=grp(jnp.concatenate([gstart + tot, gend[-1:]])), zlen=grp(seg - tot),
                texp=texp.astype(jnp.int32), nact=nact.astype(jnp.int32))


def _run_copies(src, dst, src_off, dst_off, groups, sem, max_bits):
    def arm(b):
        @pl.when(((groups >> b) & 1) == 1)
        def _():
            off = (groups >> (b + 1)) << (b + 1)
            pltpu.make_async_copy(src.at[pl.ds(src_off + off, 1 << b)], dst.at[pl.ds(dst_off + off, 1 << b)],
                                  sem).start()

    for b in range(min(COMMON_BITS, max_bits)):
        arm(b)
    if max_bits > COMMON_BITS:
        @pl.when(groups >= (1 << COMMON_BITS))
        def _():
            for b in range(COMMON_BITS, max_bits):
                arm(b)


def _wait_groups(buf, groups, sem, max_bits):
    for b in range(max_bits):
        @pl.when(((groups >> b) & 1) == 1)
        def _():
            pltpu.make_async_copy(buf.at[pl.ds(0, 1 << b)], buf.at[pl.ds(0, 1 << b)], sem).wait()


COMMON_BITS = 3
RUN_BITS = (TD // ALIGN).bit_length()
TILE_BITS = (RL // ALIGN).bit_length()


def _dispatch_kernel(pc_ref, lo_ref, dst_ref, ts_ref, zd_ref, zl_ref, mask_ref, gate_ref, h_ref,
                     xs_ref, info_ref, loc_s, zero_s, earlier_s, sems):
    i = pl.program_id(0)
    last_step = pl.num_programs(0) - 1
    slot = i % 2
    mask = mask_ref[...]
    mb = mask.astype(BF16)
    @pl.when(i == 0)
    def _():
        s_i = lax.broadcasted_iota(jnp.int32, (TD, TD), 0)
        t_i = lax.broadcasted_iota(jnp.int32, (TD, TD), 1)
        earlier_s[...] = jnp.where(s_i < t_i, 1.0, 0.0).astype(BF16)

    rank = jnp.dot(mb, earlier_s[...], preferred_element_type=F32)
    e_i = lax.broadcasted_iota(jnp.int32, (NE, NE), 0)
    f_i = lax.broadcasted_iota(jnp.int32, (NE, NE), 1)
    below = jnp.where(f_i < e_i, 1.0, 0.0).astype(BF16)
    lower = jnp.dot(below, mb, preferred_element_type=F32)
    row_e = lax.broadcasted_iota(jnp.int32, (NE, 1), 0)
    lo_vec = jnp.zeros((NE, 1), F32)
    for e in range(NE):
        lo_vec = jnp.where(row_e == e, (lo_ref[i * NE + e] * ALIGN).astype(F32), lo_vec)
    row = lo_vec + rank
    is0 = mask * jnp.where(lower == 0.0, 1.0, 0.0)
    is1 = mask - is0
    d0 = jnp.sum(is0 * row, axis=0, keepdims=True)
    d1 = jnp.sum(is1 * row, axis=0, keepdims=True)
    gate = gate_ref[...]
    w0 = jnp.sum(is0 * gate, axis=0, keepdims=True)
    w1 = jnp.sum(is1 * gate, axis=0, keepdims=True)
    info_ref[...] = jnp.concatenate([d0, d1, w0, w1, jnp.zeros((4, TD), F32)], axis=0)
    r_i = lax.broadcasted_iota(jnp.int32, (RL, TD), 0)
    perm = jnp.where((r_i == d0.astype(jnp.int32)) | (r_i == d1.astype(jnp.int32)), 1.0, 0.0).astype(BF16)
    loc = loc_s.at[slot]
    loc[...] = jnp.dot(perm, h_ref[...], preferred_element_type=F32).astype(BF16).reshape(RL // ALIGN, ALIGN, D)
    for e in range(NE):
        _run_copies(loc, xs_ref, lo_ref[i * NE + e], dst_ref[i * NE + e], pc_ref[i * NE + e], sems.at[slot],
                    RUN_BITS)

    @pl.when(i > 0)
    def _():
        _wait_groups(loc_s.at[1 - slot], ts_ref[jnp.maximum(i - 1, 0)], sems.at[1 - slot], TILE_BITS)

    @pl.when(i == last_step)
    def _():
        _wait_groups(loc, ts_ref[i], sems.at[slot], TILE_BITS)
        sem = sems.at[0]
        zero_s[...] = jnp.zeros_like(zero_s)
        zg = ZROWS // ALIGN
        for wait in (False, True):
            for e in range(NE):
                for part in range(TMX // ZROWS):
                    groups = jnp.clip(zl_ref[e] - part * zg, 0, zg)
                    if wait:
                        _wait_groups(zero_s, groups, sem, zg.bit_length())
                    else:
                        _run_copies(zero_s, xs_ref, 0, zd_ref[e] + part * zg, groups, sem, zg.bit_length())
        tail = zd_ref[NE]
        chunks = (xs_ref.shape[0] - tail) // zg

        def fill(c, carry):
            pltpu.make_async_copy(zero_s, xs_ref.at[pl.ds(tail + c * zg, zg)], sem).start()
            return carry

        def drain(c, carry):
            pltpu.make_async_copy(zero_s, zero_s, sem).wait()
            return carry

        lax.fori_loop(0, chunks, fill, 0)
        lax.fori_loop(0, chunks, drain, 0)


def _dispatch(plan, mask_t, gate_t, h2, rtot):
    n = h2.shape[0]
    ntd = n // TD
    tspec = pl.BlockSpec((NE, TD), lambda i, *_: (0, i))
    grid_spec = pltpu.PrefetchScalarGridSpec(
        num_scalar_prefetch=6, grid=(ntd,),
        in_specs=[tspec, tspec, pl.BlockSpec((TD, D), lambda i, *_: (i, 0))],
        out_specs=[pl.BlockSpec(memory_space=pl.ANY), pl.BlockSpec((8, TD), lambda i, *_: (0, i))],
        scratch_shapes=[pltpu.VMEM((2, RL // ALIGN, ALIGN, D), BF16), pltpu.VMEM((ZROWS // ALIGN, ALIGN, D), BF16),
                        pltpu.VMEM((TD, TD), BF16), pltpu.SemaphoreType.DMA((2,))])
    return pl.pallas_call(
        _dispatch_kernel, grid_spec=grid_spec,
        out_shape=[jax.ShapeDtypeStruct((rtot // ALIGN, ALIGN, D), BF16), jax.ShapeDtypeStruct((8, n), F32)],
        compiler_params=pltpu.CompilerParams(dimension_semantics=("arbitrary",), vmem_limit_bytes=VMEM_LIMIT),
        name="dispatch",
    )(plan["pc"], plan["lo"], plan["dst"], plan["tsum"], plan["zdst"], plan["zlen"], mask_t, gate_t, h2)


def _expert_kernel(texp_ref, nact_ref, x_ref, w1_ref, w3_ref, w2_ref, y_ref, w1_s, w3_s, w2_s):
    m = pl.program_id(0)
    new_expert = (m == 0) | (texp_ref[m] != texp_ref[jnp.maximum(m - 1, 0)])

    @pl.when(new_expert)
    def _():
        w1_s[...] = w1_ref[...].astype(BF16)
        w3_s[...] = w3_ref[...].astype(BF16)
        w2_s[...] = w2_ref[...].astype(BF16)

    @pl.when(m < nact_ref[0])
    def _():
        x = x_ref[...]
        a = jnp.dot(x, w1_s[...], preferred_element_type=F32)
        a = a * _sigmoid(a) * jnp.dot(x, w3_s[...], preferred_element_type=F32)
        y_ref[...] = jnp.dot(a.astype(BF16), w2_s[...], preferred_element_type=F32).astype(BF16)

    @pl.when(m >= nact_ref[0])
    def _():
        y_ref[...] = jnp.zeros_like(y_ref)


def _experts(plan, xs, layer, w1, w3, w2):
    rtot = xs.shape[0]
    n_mt = rtot // TMX
    grid_spec = pltpu.PrefetchScalarGridSpec(
        num_scalar_prefetch=2, grid=(n_mt,),
        in_specs=[pl.BlockSpec((TMX, D), lambda m, te, na: (jnp.minimum(m, na[0] - 1), 0)),
                  pl.BlockSpec((None, None, D, FF), lambda m, te, na: (layer, te[m], 0, 0)),
                  pl.BlockSpec((None, None, D, FF), lambda m, te, na: (layer, te[m], 0, 0)),
                  pl.BlockSpec((None, None, FF, D), lambda m, te, na: (layer, te[m], 0, 0))],
        out_specs=pl.BlockSpec((TMX, D), lambda m, te, na: (m, 0)),
        scratch_shapes=[pltpu.VMEM((D, FF), BF16), pltpu.VMEM((D, FF), BF16), pltpu.VMEM((FF, D), BF16)])
    return pl.pallas_call(
        _expert_kernel, grid_spec=grid_spec,
        out_shape=jax.ShapeDtypeStruct((rtot, D), BF16),
        compiler_params=pltpu.CompilerParams(dimension_semantics=("arbitrary",), vmem_limit_bytes=VMEM_LIMIT),
        name="experts",
    )(plan["texp"], plan["nact"], xs, w1, w3, w2)


def _combine_kernel(split, bidx_ref, pc_ref, lo_ref, dst_ref, ts_ref, info_ref, ys_ref, x1_ref, mod_ref,
                    lng_ref, lnb_ref, *rest):
    del bidx_ref
    o_refs, (loc_s, sems) = rest[:-2], rest[-2:]
    i = pl.program_id(0)
    slot = i % 2

    def fetch(tile, buf):
        for e in range(NE):
            _run_copies(ys_ref, loc_s.at[buf], dst_ref[tile * NE + e], lo_ref[tile * NE + e],
                        pc_ref[tile * NE + e], sems.at[buf], RUN_BITS)

    @pl.when(i == 0)
    def _():
        loc_s[...] = jnp.zeros_like(loc_s)
        fetch(0, 0)

    @pl.when(i + 1 < pl.num_programs(0))
    def _():
        fetch(i + 1, 1 - slot)

    _wait_groups(loc_s.at[slot], ts_ref[i], sems.at[slot], TILE_BITS)
    info = info_ref[...]
    d0 = info[0:1].astype(jnp.int32)
    d1 = info[1:2].astype(jnp.int32)
    r_i = lax.broadcasted_iota(jnp.int32, (RL, TD), 0)
    wperm = (jnp.where(r_i == d0, info[2:3], 0.0) + jnp.where(r_i == d1, info[3:4], 0.0)).astype(BF16)
    moe = lax.dot_general(wperm, loc_s[slot].reshape(RL, D), (((0,), (0,)), ((), ())),
                          preferred_element_type=F32)
    gate = mod_ref[...][:, 2 * D:]
    out = _ln(ALPHA * x1_ref[...] + gate * moe) * lng_ref[...] + lnb_ref[...]
    if len(o_refs) == 1:
        o_refs[0][...] = out
    else:
        @pl.when(i < split)
        def _():
            o_refs[0][...] = out

        @pl.when(i >= split)
        def _():
            o_refs[1][...] = out


def _combine(bidx_d, plan, info, ys, x1, mod2, lng, lnb, out_rows):
    n = x1.shape[0]
    ntd = n // TD
    tok = pl.BlockSpec((TD, D), lambda i, *_: (i, 0))
    full = lambda a: pl.BlockSpec(a.shape, lambda i, *_: (0,) * a.ndim)
    if len(out_rows) == 1:
        split, out_specs = 0, [tok]
    else:
        split = out_rows[0] // TD
        out_specs = [pl.BlockSpec((TD, D), lambda i, *_: (jnp.minimum(i, split - 1), 0)),
                     pl.BlockSpec((TD, D), lambda i, *_: (jnp.maximum(i - split, 0), 0))]
    grid_spec = pltpu.PrefetchScalarGridSpec(
        num_scalar_prefetch=5, grid=(ntd,),
        in_specs=[pl.BlockSpec((8, TD), lambda i, *_: (0, i)),
                  pl.BlockSpec(memory_space=pl.ANY),
                  tok,
                  pl.BlockSpec((None, 1, 3 * D), lambda i, b, *_: (b[i], 0, 0)),
                  full(lng), full(lnb)],
        out_specs=out_specs,
        scratch_shapes=[pltpu.VMEM((2, RL // ALIGN, ALIGN, D), BF16), pltpu.SemaphoreType.DMA((2,))])
    return pl.pallas_call(
        functools.partial(_combine_kernel, split), grid_spec=grid_spec,
        out_shape=[jax.ShapeDtypeStruct((r, D), F32) for r in out_rows],
        compiler_params=pltpu.CompilerParams(dimension_semantics=("arbitrary",), vmem_limit_bytes=VMEM_LIMIT),
        name="combine",
    )(bidx_d, plan["pc"], plan["lo"], plan["dst"], plan["tsum"], info, ys, x1, mod2, lng, lnb)


def _moe(meta, h2, mask_t, gate_t, cnt, layer, w1, w3, w2, x1, mod2, lng, lnb, out_rows):
    n = h2.shape[0]
    ntd = n // TD
    rtot = -(-(ntd * RL) // TMX) * TMX + NE * TMX
    cnt = cnt[:, :, :TM // TD].transpose(0, 2, 1).reshape(ntd, NE).astype(jnp.int32)
    plan = _plan(cnt, rtot // TMX)
    xs, info = _dispatch(plan, mask_t, gate_t, h2, rtot)
    ys = _experts(plan, xs.reshape(rtot, D), layer, w1, w3, w2).reshape(rtot // ALIGN, ALIGN, D)
    return _combine(meta["bidx_d"], plan, info, ys, x1, mod2, lng, lnb, out_rows)


def _tile_meta(groups):
    bidx, pblk, first, last, cf, cb, bidx_d = [], [], [], [], [], [], []
    row = 0
    tile = 0
    for (b, s) in groups:
        per = s // TM
        for bi in range(b):
            for j in range(per):
                bidx.append(row + bi)
                pblk.append(j)
                first.append(1 if j == 0 else 0)
                last.append(1 if j == per - 1 else 0)
                cf.append(tile + j)
                cb.append(tile + per - 1 - j)
            tile += per
            bidx_d += [row + bi] * (s // TD)
        row += b
    as_i32 = lambda v: jnp.asarray(np.asarray(v, np.int32))
    return dict(bidx=as_i32(bidx), pblk=as_i32(pblk), first=as_i32(first), last=as_i32(last),
                cf=as_i32(cf), cb=as_i32(cb), bidx_d=as_i32(bidx_d))


def _rope_tables(s_max):
    inv = jnp.power(ROPE_THETA, -jnp.arange(0, HD, 2, dtype=F32) / HD)
    ang = jnp.arange(s_max, dtype=F32)[:, None] * inv[None, :]
    cos, sin = jnp.cos(ang), jnp.sin(ang)
    cos128 = jnp.tile(cos, (1, 4))
    sin128 = jnp.tile(jnp.concatenate([-sin, sin], axis=1), (1, 2))
    return cos128, sin128


def _block_diag(w):
    d, nb, c, f = w.shape
    eye = jnp.eye(nb, dtype=w.dtype)
    return jnp.einsum('dncf,nm->dncmf', w, eye).reshape(d, nb * c, nb * f)


def _forward(xs, cs, w_mod, b_mod, w_in, attn_sink, conv_w, conv_b, lru_w_r, lru_b_r, lru_w_i, lru_b_i,
             lru_lambda, sg_norm_g, sg_w, sg_b, mix_norm_g, w_out, ln_g, ln_b, router_w, router_bias,
             exp_w1, exp_w3, exp_w2):
    groups = [(x.shape[0], x.shape[1]) for x in xs]
    assert len(groups) == 2
    for (_, s) in groups:
        assert s % TM == 0
    meta = _tile_meta(groups)
    x_parts = [xx.reshape(-1, D) for xx in xs]
    group_rows = [p.shape[0] for p in x_parts]
    c_all = jnp.concatenate(cs, axis=0)
    bt = c_all.shape[0]
    mods = _modulation(c_all, w_mod, b_mod).reshape(2 * DEPTH, bt, 1, 3 * D)
    cos_t, sin_t = _rope_tables(max(s for _, s in groups))
    rwt = router_w.T.astype(BF16)
    rb = router_bias.reshape(NE, 1)
    for l in range(DEPTH):
        q, kk, vv, gy, xr, u, vn = _pre_mixer(meta, x_parts, mods[2 * l], w_in[l].astype(BF16), cos_t, sin_t,
                                              sg_norm_g[l].reshape(1, SGW))
        oa = _attention(meta, attn_sink[l], q, kk, vv)
        hf, hb = _recurrent(meta, xr, conv_w[l], conv_b[l].reshape(1, RW),
                            _block_diag(0.5 * lru_w_r[l]).astype(BF16), _block_diag(0.5 * lru_w_i[l]).astype(BF16),
                            0.5 * lru_b_r[l], 0.5 * lru_b_i[l], lru_lambda[l])
        sgb = jnp.repeat(sg_b[l].T, HD, axis=1)
        x1, h2, gate_t, mask_t, cnt = _merge(meta, x_parts, oa, gy, hf, hb, u, vn, mods[2 * l], mods[2 * l + 1],
                                             mix_norm_g[l].reshape(1, D), sg_w[l].astype(BF16), sgb,
                                             w_out[l].astype(BF16), ln_g[l, 0].reshape(1, D),
                                             ln_b[l, 0].reshape(1, D), rwt, rb)
        out_rows = group_rows if l == DEPTH - 1 else [sum(group_rows)]
        x_parts = _moe(meta, h2, mask_t, gate_t, cnt, l, exp_w1, exp_w3, exp_w2, x1, mods[2 * l + 1],
                       ln_g[l, 1].reshape(1, D), ln_b[l, 1].reshape(1, D), out_rows)
    return tuple(p.reshape(b, s, D) for p, (b, s) in zip(x_parts, groups))


def kernel(x_prompt, x_sample, c_prompt, c_sample, w_mod, b_mod, w_in, attn_sink, conv_w, conv_b, lru_w_r, lru_b_r,
           lru_w_i, lru_b_i, lru_lambda, sg_norm_g, sg_w, sg_b, mix_norm_g, w_out, ln_g, ln_b, router_w,
           router_bias, exp_w1, exp_w3, exp_w2):
    return _forward([x_prompt, x_sample], [c_prompt, c_sample], w_mod, b_mod, w_in, attn_sink, conv_w, conv_b,
                    lru_w_r, lru_b_r, lru_w_i, lru_b_i, lru_lambda, sg_norm_g, sg_w, sg_b, mix_norm_g, w_out,
                    ln_g, ln_b, router_w, router_bias, exp_w1, exp_w3, exp_w2)
```

```python
import functools

import numpy as np
import jax
import jax.numpy as jnp
from jax import lax
from jax.experimental import pallas as pl
from jax.experimental.pallas import tpu as pltpu

F32 = jnp.float32
BF16 = jnp.bfloat16

D = 1024
DEPTH = 2
HD = 64
AW = 384
KVW = 128
RW = 384
SGW = 256
WINDOW = 128
CONV_WIDTH = 4
LRU_C = 8.0
CHUNK = 128
NE = 16
NG = 4
EPG = NE // NG
FF = 512
ALPHA = (2 * DEPTH) ** 0.25
LN_EPS = 1e-5
RMS_EPS = 1e-6
ROPE_THETA = 10000.0

TM = 512
TP = 1024
TA = 1024
TR = 1024
QB = 128
SEG = TR // 8
NLG = RW // 128
TD = 512
ALIGN = 16
RL = 2 * TD + NE * ALIGN
TMX = 1024
ZROWS = 256
NEG = -1e30
VMEM_LIMIT = 48 * 1024 * 1024

C_Q, C_K, C_V, C_Y, C_R, C_U, C_SV, C_END = 0, 384, 512, 640, 1024, 1408, 1664, 1920


def _ln(x):
    mu = jnp.mean(x, axis=-1, keepdims=True)
    xc = x - mu
    var = jnp.mean(xc * xc, axis=-1, keepdims=True)
    return xc * lax.rsqrt(var + LN_EPS)


def _rms(x):
    return x * lax.rsqrt(jnp.mean(x * x, axis=-1, keepdims=True) + RMS_EPS)


def _sigmoid(z):
    return 0.5 * (jnp.tanh(0.5 * z) + 1.0)


def _gelu(x):
    return 0.5 * x * (1.0 + jnp.tanh(0.7978845608028654 * (x + 0.044715 * (x * x * x))))


def _split_bf16(a):
    hi = a.astype(BF16)
    lo = (a - hi.astype(F32)).astype(BF16)
    return hi, lo


def _mod_kernel(c_ref, w_ref, b_ref, o_ref):
    c = c_ref[...]
    ca = c * _sigmoid(c)
    ch, cl = _split_bf16(ca)
    wh, wl = _split_bf16(w_ref[...])
    acc = jnp.dot(ch, wh, preferred_element_type=F32)
    acc += jnp.dot(ch, wl, preferred_element_type=F32)
    acc += jnp.dot(cl, wh, preferred_element_type=F32)
    o_ref[...] = acc + b_ref[...]


def _modulation(c_all, w_mod, b_mod):
    bt = c_all.shape[0]
    cb = 768
    w = w_mod.reshape(2 * DEPTH, D, 3 * D)
    b = b_mod.reshape(2 * DEPTH, 1, 3 * D)
    return pl.pallas_call(
        _mod_kernel,
        grid=(2 * DEPTH, 3 * D // cb),
        in_specs=[pl.BlockSpec((bt, D), lambda s, j: (0, 0)),
                  pl.BlockSpec((None, D, cb), lambda s, j: (s, 0, j)),
                  pl.BlockSpec((None, 1, cb), lambda s, j: (s, 0, j))],
        out_specs=pl.BlockSpec((None, bt, cb), lambda s, j: (s, 0, j)),
        out_shape=jax.ShapeDtypeStruct((2 * DEPTH, bt, 3 * D), F32),
        name="modulation",
    )(c_all, w, b)


def _load_tokens(i, x_refs, split):
    if len(x_refs) == 1:
        return x_refs[0][...]
    return jnp.where(i < split, x_refs[0][...], x_refs[1][...])


def _token_specs(x_parts, rows):
    if len(x_parts) == 1:
        return [pl.BlockSpec((rows, D), lambda i, *_: (i, 0))], 0
    split = x_parts[0].shape[0] // rows
    return [pl.BlockSpec((rows, D), lambda i, *_: (jnp.minimum(i, split - 1), 0)),
            pl.BlockSpec((rows, D), lambda i, *_: (jnp.maximum(i - split, 0), 0))], split


def _pre_kernel(split, bidx_ref, pblk_ref, *refs):
    del bidx_ref, pblk_ref
    nx = len(refs) - 12
    x_refs = refs[:nx]
    mod_ref, w_ref, cos_ref, sin_ref, sgg_ref, q_ref, kk_ref, vv_ref, gy_ref, xr_ref, u_ref, vn_ref = refs[nx:]
    mod = mod_ref[...]
    shift, scale = mod[:, :D], mod[:, D:2 * D]
    h = (_ln(_load_tokens(pl.program_id(0), x_refs, split)) * (1.0 + scale) + shift).astype(BF16)
    cos = cos_ref[...]
    sin = sin_ref[...]
    lane = lax.broadcasted_iota(jnp.int32, (1, 128), 1)
    first_half = (lane % HD) < (HD // 2)

    def rope128(xg):
        rot = jnp.where(first_half, pltpu.roll(xg, 128 - HD // 2, 1), pltpu.roll(xg, HD // 2, 1))
        return xg * cos + rot * sin

    q = jnp.dot(h, w_ref[:, C_Q:C_K], preferred_element_type=F32)
    for g in range(AW // 128):
        q_ref[:, g * 128:(g + 1) * 128] = (rope128(q[:, g * 128:(g + 1) * 128]) * (HD ** -0.5)).astype(BF16)
    kv = jnp.dot(h, w_ref[:, C_K:C_Y], preferred_element_type=F32)
    lo_half = lane < HD
    for src, dst_ref in ((rope128(kv[:, :KVW]), kk_ref), (kv[:, KVW:], vv_ref)):
        swapped = pltpu.roll(src, HD, 1)
        dst_ref[:, 0:128] = jnp.where(lo_half, src, swapped).astype(BF16)
        dst_ref[:, 128:256] = jnp.where(lo_half, swapped, src).astype(BF16)
    gy_ref[...] = _gelu(jnp.dot(h, w_ref[:, C_Y:C_R], preferred_element_type=F32)).astype(gy_ref.dtype)
    xr = jnp.dot(h, w_ref[:, C_R:C_U], preferred_element_type=F32)
    for g in range(NLG):
        xr_ref[g] = xr[:, g * 128:(g + 1) * 128]
    u_ref[...] = _gelu(jnp.dot(h, w_ref[:, C_U:C_SV], preferred_element_type=F32)).astype(u_ref.dtype)
    sv = _gelu(jnp.dot(h, w_ref[:, C_SV:C_END], preferred_element_type=F32))
    vn_ref[...] = (_ln(sv) * sgg_ref[...]).astype(BF16)


def _pre_mixer(meta, x_parts, mod_l, w_ext, cos_t, sin_t, sgg):
    n = sum(p.shape[0] for p in x_parts)
    nt = n // TP
    tok = lambda w: pl.BlockSpec((TP, w), lambda i, b, p: (i, 0))
    full = lambda a: pl.BlockSpec(a.shape, lambda i, b, p: (0,) * a.ndim)
    x_specs, split = _token_specs(x_parts, TP)
    grid_spec = pltpu.PrefetchScalarGridSpec(
        num_scalar_prefetch=2, grid=(nt,),
        in_specs=x_specs + [
            pl.BlockSpec((None, 1, 3 * D), lambda i, b, p: (b[i], 0, 0)),
            full(w_ext),
            pl.BlockSpec((TP, 128), lambda i, b, p: (p[i], 0)),
            pl.BlockSpec((TP, 128), lambda i, b, p: (p[i], 0)),
            full(sgg)],
        out_specs=[tok(AW), tok(256), tok(256), tok(RW),
                   pl.BlockSpec((NLG, TP, 128), lambda i, b, p: (0, i, 0)), tok(SGW), tok(SGW)])
    sds = lambda w, dt: jax.ShapeDtypeStruct((n, w), dt)
    return pl.pallas_call(
        functools.partial(_pre_kernel, split), grid_spec=grid_spec,
        out_shape=[sds(AW, BF16), sds(256, BF16), sds(256, BF16), sds(RW, BF16),
                   jax.ShapeDtypeStruct((NLG, n, 128), F32), sds(SGW, BF16), sds(SGW, BF16)],
        compiler_params=pltpu.CompilerParams(dimension_semantics=("parallel",), vmem_limit_bytes=VMEM_LIMIT),
        name="pre_mixer",
    )(meta["bidx"], meta["pblk"], *x_parts, mod_l, w_ext, cos_t, sin_t, sgg)


def _attn_kernel(first_ref, last_ref, sink_ref, q_ref, k_ref, kp_ref, kn_ref, v_ref, vp_ref, vn_ref,
                 o_ref, kw_s, vw_s):
    i = pl.program_id(0)
    is_first = first_ref[i] == 1
    is_last = last_ref[i] == 1
    kw_s[0:QB] = kp_ref[...]
    kw_s[QB:QB + TA] = k_ref[...]
    kw_s[QB + TA:] = kn_ref[...]
    vw_s[0:QB] = vp_ref[...]
    vw_s[QB:QB + TA] = v_ref[...]
    vw_s[QB + TA:] = vn_ref[...]
    iq = lax.broadcasted_iota(jnp.int32, (QB, 3 * QB), 0)
    ik = lax.broadcasted_iota(jnp.int32, (QB, 3 * QB), 1)
    rel = ik - iq
    band = (rel >= 0) & (rel <= 2 * WINDOW)
    lane = lax.broadcasted_iota(jnp.int32, (1, 128), 1)
    lo_half = lane < HD
    zero = jnp.zeros((), BF16)
    gsz = AW // HD // 2
    row_head = lax.broadcasted_iota(jnp.int32, (gsz * QB, 1), 0) // QB
    for j in range(TA // QB):
        ok = band
        if j == 0:
            ok = ok & ((ik >= QB) | jnp.logical_not(is_first))
        if j == TA // QB - 1:
            ok = ok & ((ik < 2 * QB) | jnp.logical_not(is_last))
        bias = jnp.where(ok, 0.0, NEG)
        bias = jnp.concatenate([bias] * gsz, axis=0)
        rows = slice(j * QB, (j + 1) * QB)
        keys = slice(j * QB, j * QB + 3 * QB)
        outs = [None] * (AW // HD)
        for g in range(2):
            heads = range(g * gsz, (g + 1) * gsz)
            qs = []
            for hh in heads:
                qg = q_ref[rows, (hh // 2) * 128:(hh // 2 + 1) * 128]
                qs.append(jnp.where(lo_half if hh % 2 == 0 else jnp.logical_not(lo_half), qg, zero))
            s = lax.dot_general(jnp.concatenate(qs, axis=0), kw_s[keys, g * 128:(g + 1) * 128],
                                (((1,), (1,)), ((), ())), preferred_element_type=F32) + bias
            sink = jnp.full((gsz * QB, 1), sink_ref[heads[-1]], F32)
            for t in range(gsz - 1):
                sink = jnp.where(row_head == t, sink_ref[heads[t]], sink)
            m = jnp.maximum(jnp.max(s, axis=-1, keepdims=True), sink)
            p = jnp.exp(s - m)
            denom = jnp.sum(p, axis=-1, keepdims=True) + jnp.exp(sink - m)
            pv = jnp.dot(p.astype(BF16), vw_s[keys, g * 128:(g + 1) * 128], preferred_element_type=F32)
            o = pv * (1.0 / denom)
            for t, hh in enumerate(heads):
                outs[hh] = o[t * QB:(t + 1) * QB]
        for jg in range(AW // 128):
            o_ref[rows, jg * 128:(jg + 1) * 128] = jnp.where(lo_half, outs[2 * jg], outs[2 * jg + 1]).astype(o_ref.dtype)


def _attention(meta, sink, q, kk, vv):
    n = q.shape[0]
    nt = n // TA
    nqb = n // QB
    r = TA // QB
    main = lambda w: pl.BlockSpec((TA, w), lambda i, f, l, s: (i, 0))
    prev = pl.BlockSpec((QB, 256), lambda i, f, l, s: (jnp.maximum(i * r - 1, 0), 0))
    nxt = pl.BlockSpec((QB, 256), lambda i, f, l, s: (jnp.minimum(i * r + r, nqb - 1), 0))
    grid_spec = pltpu.PrefetchScalarGridSpec(
        num_scalar_prefetch=3, grid=(nt,),
        in_specs=[main(AW), main(256), prev, nxt, main(256), prev, nxt],
        out_specs=main(AW),
        scratch_shapes=[pltpu.VMEM((TA + 2 * QB, 256), BF16), pltpu.VMEM((TA + 2 * QB, 256), BF16)])
    return pl.pallas_call(
        _attn_kernel, grid_spec=grid_spec,
        out_shape=jax.ShapeDtypeStruct((n, AW), BF16),
        compiler_params=pltpu.CompilerParams(dimension_semantics=("parallel",), vmem_limit_bytes=VMEM_LIMIT),
        name="attention",
    )(meta["first"], meta["last"], sink, q, kk, kk, kk, vv, vv, vv)


def _rec_kernel(cf_ref, cb_ref, first_ref, last_ref,
                x_hbm, xfp_ref, xfn_ref, xbp_ref, xbn_ref,
                cw_ref, cbias_ref, wr_ref, wi_ref, br_ref, bi_ref, lam_ref,
                hf_hbm, hb_hbm,
                ext_s, a_s, b_s, hs_s, ps_s, out_s, carry_s, gsem, ssem):
    i = pl.program_id(0)
    nsteps = pl.num_programs(0)
    slot = i % 2
    cw = cw_ref[...]
    cbias = cbias_ref[...]
    sub = lax.broadcasted_iota(jnp.int32, (8, 128), 0)
    unroll = 8
    chunk_refs = (cf_ref, cb_ref)
    out_hbm = (hf_hbm, hb_hbm)

    def gather(step, buf, d, start):
        c = chunk_refs[d][step]
        for g in range(NLG):
            for s in range(8):
                cp = pltpu.make_async_copy(x_hbm.at[g, pl.ds(pl.multiple_of(c * TR + s * SEG, SEG), SEG), :],
                                           ext_s.at[buf, d, g, pl.ds(2, SEG), s, :], gsem.at[buf, d])
                if start:
                    cp.start()
                else:
                    cp.wait()

    def scatter(d, c, start):
        for g in range(NLG):
            for s in range(8):
                cp = pltpu.make_async_copy(out_s.at[d, g, :, s, :],
                                           out_hbm[d].at[g, pl.ds(pl.multiple_of(c * TR + s * SEG, SEG), SEG), :],
                                           ssem.at[d])
                if start:
                    cp.start()
                else:
                    cp.wait()

    @pl.when(i == 0)
    def _():
        gather(0, 0, 0, True)
        gather(0, 0, 1, True)

    @pl.when(i + 1 < nsteps)
    def _():
        gather(i + 1, 1 - slot, 0, True)
        gather(i + 1, 1 - slot, 1, True)

    def run(d, xp_ref, xn_ref):
        c = chunk_refs[d][i]
        first = first_ref[c] == 1
        last = last_ref[c] == 1
        gather(i, slot, d, False)
        ext = ext_s.at[slot, d]
        planes = []
        for g in range(NLG):
            prev = jnp.where(first, 0.0, xp_ref[g])
            nxt = jnp.where(last, 0.0, xn_ref[g])
            ext[g, 0] = jnp.where(sub == 0, prev[6:7], pltpu.roll(ext[g, SEG], 1, 0))
            ext[g, 1] = jnp.where(sub == 0, prev[7:8], pltpu.roll(ext[g, SEG + 1], 1, 0))
            ext[g, SEG + 2] = jnp.where(sub == 7, nxt[0:1], pltpu.roll(ext[g, 2], 7, 0))
            lanes = slice(g * 128, (g + 1) * 128)
            acc = ext[g, 0:SEG] * cw[0:1, lanes]
            for t in range(1, CONV_WIDTH):
                acc = acc + ext[g, t:t + SEG] * cw[t:t + 1, lanes]
            planes.append(acc.reshape(TR, 128))
        xc = jnp.concatenate(planes, axis=1) + cbias
        xb16 = xc.astype(BF16)
        tr = jnp.tanh(jnp.dot(xb16, wr_ref[d], preferred_element_type=F32) + br_ref[d:d + 1])
        ti = jnp.tanh(jnp.dot(xb16, wi_ref[d], preferred_element_type=F32) + bi_ref[d:d + 1])
        nlam = -lam_ref[d:d + 1]
        softplus = jnp.maximum(nlam, 0.0) + jnp.log1p(jnp.exp(-jnp.abs(nlam)))
        half_c = (-0.5 * LRU_C) * softplus
        log_a = half_c * tr + half_c
        a = jnp.exp(log_a)
        v = (1.0 + a * a) * jnp.tanh(-log_a)
        b = jnp.where(v > 0.0, v * lax.rsqrt(v), 0.0) * (ti + 1.0) * (0.5 * xc)
        for g in range(NLG):
            a_s[g] = a[:, g * 128:(g + 1) * 128].reshape(SEG, 8, 128)
            b_s[g] = b[:, g * 128:(g + 1) * 128].reshape(SEG, 8, 128)

        reverse = d == 1
        reset = last if reverse else first

        @pl.when(reset)
        def _():
            carry_s[d] = jnp.zeros((NLG, 128), F32)

        def step(jo, hp):
            hp = list(hp)
            for ji in range(unroll):
                jj = jo * unroll + ji
                j = SEG - 1 - jj if reverse else jj
                for g in range(NLG):
                    h, p = hp[g]
                    ag = a_s[g, j]
                    h = ag * h + b_s[g, j]
                    p = p * ag
                    hs_s[g, j] = h
                    ps_s[g, j] = p
                    hp[g] = (h, p)
            return tuple(hp)

        init = tuple((jnp.zeros((8, 128), F32), jnp.ones((8, 128), F32)) for _ in range(NLG))
        ends = lax.fori_loop(0, SEG // unroll, step, init)
        cmats = []
        for g in range(NLG):
            e, pe = ends[g]
            c_in = carry_s[d, g:g + 1]
            rows = [None] * 8
            for s in (range(7, -1, -1) if reverse else range(8)):
                rows[s] = c_in
                c_in = e[s:s + 1] + pe[s:s + 1] * c_in
            carry_s[d, g:g + 1] = c_in
            cmats.append(jnp.concatenate(rows, axis=0))

        @pl.when(i > 0)
        def _():
            scatter(d, 0, False)

        def fix(jo, carry):
            for ji in range(unroll):
                j = jo * unroll + ji
                for g in range(NLG):
                    out_s[d, g, j] = hs_s[g, j] + ps_s[g, j] * cmats[g]
            return carry

        lax.fori_loop(0, SEG // unroll, fix, 0)
        scatter(d, c, True)

    run(0, xfp_ref, xfn_ref)
    run(1, xbp_ref, xbn_ref)

    @pl.when(i == nsteps - 1)
    def _():
        scatter(0, 0, False)
        scatter(1, 0, False)


def _recurrent(meta, xr, cw, cbias, wr, wi, br, bi, lam):
    n = xr.shape[1]
    nt = n // TR
    n8 = n // 8
    r8 = TR // 8

    def halos(which):
        sel = (lambda cf, cb: cf) if which == 0 else (lambda cf, cb: cb)
        prev = pl.BlockSpec((NLG, 8, 128),
                            lambda i, cf, cb, f, l: (0, jnp.maximum(sel(cf, cb)[i] * r8 - 1, 0), 0))
        nxt = pl.BlockSpec((NLG, 8, 128),
                           lambda i, cf, cb, f, l: (0, jnp.minimum(sel(cf, cb)[i] * r8 + r8, n8 - 1), 0))
        return prev, nxt

    full = lambda a: pl.BlockSpec(a.shape, lambda i, cf, cb, f, l: (0,) * a.ndim)
    anywhere = pl.BlockSpec(memory_space=pl.ANY)
    plane = lambda lead: pltpu.VMEM(lead + (NLG, SEG, 8, 128), F32)
    grid_spec = pltpu.PrefetchScalarGridSpec(
        num_scalar_prefetch=4, grid=(nt,),
        in_specs=[anywhere, *halos(0), *halos(1),
                  full(cw), full(cbias), full(wr), full(wi), full(br), full(bi), full(lam)],
        out_specs=[anywhere, anywhere],
        scratch_shapes=[pltpu.VMEM((2, 2, NLG, SEG + 3, 8, 128), F32),
                        plane(()), plane(()), plane(()), plane(()), plane((2,)),
                        pltpu.VMEM((2, NLG, 128), F32),
                        pltpu.SemaphoreType.DMA((2, 2)), pltpu.SemaphoreType.DMA((2,))])
    return pl.pallas_call(
        _rec_kernel, grid_spec=grid_spec,
        out_shape=[jax.ShapeDtypeStruct((NLG, n, 128), F32), jax.ShapeDtypeStruct((NLG, n, 128), F32)],
        compiler_params=pltpu.CompilerParams(dimension_semantics=("arbitrary",), vmem_limit_bytes=VMEM_LIMIT),
        name="recurrent",
    )(meta["cf"], meta["cb"], meta["first"], meta["last"], xr, xr, xr, xr, xr, cw, cbias, wr, wi, br, bi, lam)


def _route(sel, score):
    one = jnp.ones_like(sel[0])
    zero = jnp.zeros_like(sel[0])

    def before(vk, vj, k, j):
        return (vk > vj) | ((vk == vj) & (k < j)) if k < j else (vk > vj)

    in_top = []
    gscore = []
    for g in range(NG):
        ids = range(g * EPG, (g + 1) * EPG)
        gs = zero
        for j in ids:
            rank = zero
            for k in ids:
                if k != j:
                    rank = rank + jnp.where(before(sel[k], sel[j], k, j), one, zero)
            m = jnp.where(rank < 2.0, one, zero)
            in_top.append(m)
            gs = gs + m * sel[j]
        gscore.append(gs)
    mask = []
    for g in range(NG):
        worse = zero
        for k in range(NG):
            if k != g:
                worse = worse + jnp.where(before(gscore[k], gscore[g], k, g), one, zero)
        best = jnp.where(worse < 1.0, one, zero)
        for j in range(g * EPG, (g + 1) * EPG):
            mask.append(in_top[j] * best)
    total = zero
    for e in range(NE):
        total = total + mask[e] * score[e]
    inv = 1.0 / total
    gate = [mask[e] * score[e] * inv for e in range(NE)]
    return mask, gate


def _merge_kernel(split, bidx_ref, *refs):
    del bidx_ref
    nx = len(refs) - 22
    x_refs = refs[:nx]
    (oa_ref, gy_ref, hf_ref, hb_ref, u_ref, vn_ref, mod1_ref, mod2_ref, gmix_ref, sgw_ref, sgb_ref, wout_ref,
     lng_ref, lnb_ref, rwt_ref, rb_ref, x1_ref, h2_ref, gate_ref, mask_ref, cnt_ref, mrg_s) = refs[nx:]
    gmix = gmix_ref[...]
    mrg_s[:, 0:AW] = (_rms(oa_ref[...].astype(F32)) * gmix[:, 0:AW]).astype(BF16)
    hsum = jnp.concatenate([hf_ref[g] + hb_ref[g] for g in range(NLG)], axis=1)
    o_rec = gy_ref[...].astype(F32) * hsum
    mrg_s[:, AW:AW + RW] = (_rms(o_rec) * gmix[:, AW:AW + RW]).astype(BF16)
    lane = lax.broadcasted_iota(jnp.int32, (1, 128), 1)
    lo_half = lane < HD
    zero = jnp.zeros((), BF16)
    pieces = []
    for c in range(TM // CHUNK):
        rows = slice(c * CHUNK, (c + 1) * CHUNK)
        grp = []
        for g in range(SGW // 128):
            vg = vn_ref[rows, g * 128:(g + 1) * 128]
            mixed = jnp.dot(sgw_ref[2 * g], jnp.where(lo_half, vg, zero), preferred_element_type=F32)
            mixed += jnp.dot(sgw_ref[2 * g + 1], jnp.where(lo_half, zero, vg), preferred_element_type=F32)
            grp.append(mixed)
        mixed = jnp.concatenate(grp, axis=1) + sgb_ref[...]
        pieces.append(u_ref[rows, :].astype(F32) * mixed)
    o_sg = jnp.concatenate(pieces, axis=0)
    mrg_s[:, AW + RW:] = (_rms(o_sg) * gmix[:, AW + RW:]).astype(BF16)
    o = jnp.dot(mrg_s[...], wout_ref[...], preferred_element_type=F32)
    gate1 = mod1_ref[...][:, 2 * D:]
    x_in = _load_tokens(pl.program_id(0), x_refs, split)
    x1 = _ln(ALPHA * x_in + gate1 * o) * lng_ref[...] + lnb_ref[...]
    x1_ref[...] = x1
    mod2 = mod2_ref[...]
    h2 = (_ln(x1) * (1.0 + mod2[:, D:2 * D]) + mod2[:, :D]).astype(BF16)
    h2_ref[...] = h2
    logits = lax.dot_general(rwt_ref[...], h2, (((1,), (1,)), ((), ())), preferred_element_type=F32)
    score = _sigmoid(logits)
    sel = score + rb_ref[...]
    mask, gate = _route([sel[e:e + 1] for e in range(NE)], [score[e:e + 1] for e in range(NE)])
    gate_ref[...] = jnp.concatenate(gate, axis=0)
    mask_t = jnp.concatenate(mask, axis=0)
    mask_ref[...] = mask_t
    t_i = lax.broadcasted_iota(jnp.int32, (TM, 128), 0)
    j_i = lax.broadcasted_iota(jnp.int32, (TM, 128), 1)
    in_tile = jnp.where((t_i >= j_i * TD) & (t_i < (j_i + 1) * TD), 1.0, 0.0).astype(BF16)
    cnt_ref[...] = jnp.dot(mask_t.astype(BF16), in_tile, preferred_element_type=F32)


def _merge(meta, x_parts, oa, gy, hf, hb, u, vn, mod1, mod2, gmix, sgw, sgb, wout, lng, lnb, rwt, rb):
    n = oa.shape[0]
    nt = n // TM
    x_specs, split = _token_specs(x_parts, TM)
    tok = lambda w: pl.BlockSpec((TM, w), lambda i, b: (i, 0))
    full = lambda a: pl.BlockSpec(a.shape, lambda i, b: (0,) * a.ndim)
    modspec = pl.BlockSpec((None, 1, 3 * D), lambda i, b: (b[i], 0, 0))
    tspec = pl.BlockSpec((NE, TM), lambda i, b: (0, i))
    rec = pl.BlockSpec((NLG, TM, 128), lambda i, b: (0, i, 0))
    grid_spec = pltpu.PrefetchScalarGridSpec(
        num_scalar_prefetch=1, grid=(nt,),
        in_specs=x_specs + [
            tok(AW), tok(RW), rec, rec, tok(SGW), tok(SGW), modspec, modspec,
            full(gmix), full(sgw), full(sgb), full(wout), full(lng), full(lnb), full(rwt), full(rb)],
        out_specs=[tok(D), tok(D), tspec, tspec, pl.BlockSpec((None, NE, 128), lambda i, b: (i, 0, 0))],
        scratch_shapes=[pltpu.VMEM((TM, D), BF16)])
    return pl.pallas_call(
        functools.partial(_merge_kernel, split), grid_spec=grid_spec,
        out_shape=[jax.ShapeDtypeStruct((n, D), F32), jax.ShapeDtypeStruct((n, D), BF16),
                   jax.ShapeDtypeStruct((NE, n), F32), jax.ShapeDtypeStruct((NE, n), F32),
                   jax.ShapeDtypeStruct((nt, NE, 128), F32)],
        compiler_params=pltpu.CompilerParams(dimension_semantics=("parallel",), vmem_limit_bytes=VMEM_LIMIT),
        name="merge_route",
    )(meta["bidx"], *x_parts, oa, gy, hf, hb, u, vn, mod1, mod2, gmix, sgw, sgb, wout, lng, lnb, rwt, rb)


def _plan(cnt, n_mt):
    pc = (cnt + (ALIGN - 1)) // ALIGN * ALIGN
    lo = jnp.cumsum(pc, axis=1) - pc
    tot = jnp.sum(pc, axis=0)
    seg = (tot + (TMX - 1)) // TMX * TMX
    gend = jnp.cumsum(seg)
    gstart = gend - seg
    dst = gstart[None, :] + jnp.cumsum(pc, axis=0) - pc
    tile_row = jnp.arange(n_mt, dtype=jnp.int32) * TMX
    texp = jnp.minimum(jnp.sum((gend[None, :] <= tile_row[:, None]).astype(jnp.int32), axis=1), NE - 1)
    nact = (gend[-1] // TMX).reshape(1)
    grp = lambda a: (a // ALIGN).astype(jnp.int32)
    return dict(pc=grp(pc).reshape(-1), lo=grp(lo).reshape(-1), dst=grp(dst).reshape(-1),
                tsum=grp(jnp.sum(pc, axis=1)),
                zdst=grp(jnp.concatenate([gstart + tot, gend[-1:]])), zlen=grp(seg - tot),
                texp=texp.astype(jnp.int32), nact=nact.astype(jnp.int32))


def _run_copies(src, dst, src_off, dst_off, groups, sem, max_bits):
    def arm(b):
        @pl.when(((groups >> b) & 1) == 1)
        def _():
            off = (groups >> (b + 1)) << (b + 1)
            pltpu.make_async_copy(src.at[pl.ds(src_off + off, 1 << b)], dst.at[pl.ds(dst_off + off, 1 << b)],
                                  sem).start()

    for b in range(min(COMMON_BITS, max_bits)):
        arm(b)
    if max_bits > COMMON_BITS:
        @pl.when(groups >= (1 << COMMON_BITS))
        def _():
            for b in range(COMMON_BITS, max_bits):
                arm(b)


def _wait_groups(buf, groups, sem, max_bits):
    for b in range(max_bits):
        @pl.when(((groups >> b) & 1) == 1)
        def _():
            pltpu.make_async_copy(buf.at[pl.ds(0, 1 << b)], buf.at[pl.ds(0, 1 << b)], sem).wait()


COMMON_BITS = 3
RUN_BITS = (TD // ALIGN).bit_length()
TILE_BITS = (RL // ALIGN).bit_length()


def _dispatch_kernel(pc_ref, lo_ref, dst_ref, ts_ref, zd_ref, zl_ref, mask_ref, gate_ref, h_ref,
                     xs_ref, info_ref, loc_s, zero_s, earlier_s, sems):
    i = pl.program_id(0)
    last_step = pl.num_programs(0) - 1
    slot = i % 2
    mask = mask_ref[...]
    mb = mask.astype(BF16)
    @pl.when(i == 0)
    def _():
        s_i = lax.broadcasted_iota(jnp.int32, (TD, TD), 0)
        t_i = lax.broadcasted_iota(jnp.int32, (TD, TD), 1)
        earlier_s[...] = jnp.where(s_i < t_i, 1.0, 0.0).astype(BF16)

    rank = jnp.dot(mb, earlier_s[...], preferred_element_type=F32)
    e_i = lax.broadcasted_iota(jnp.int32, (NE, NE), 0)
    f_i = lax.broadcasted_iota(jnp.int32, (NE, NE), 1)
    below = jnp.where(f_i < e_i, 1.0, 0.0).astype(BF16)
    lower = jnp.dot(below, mb, preferred_element_type=F32)
    row_e = lax.broadcasted_iota(jnp.int32, (NE, 1), 0)
    lo_vec = jnp.zeros((NE, 1), F32)
    for e in range(NE):
        lo_vec = jnp.where(row_e == e, (lo_ref[i * NE + e] * ALIGN).astype(F32), lo_vec)
    row = lo_vec + rank
    is0 = mask * jnp.where(lower == 0.0, 1.0, 0.0)
    is1 = mask - is0
    d0 = jnp.sum(is0 * row, axis=0, keepdims=True)
    d1 = jnp.sum(is1 * row, axis=0, keepdims=True)
    gate = gate_ref[...]
    w0 = jnp.sum(is0 * gate, axis=0, keepdims=True)
    w1 = jnp.sum(is1 * gate, axis=0, keepdims=True)
    info_ref[...] = jnp.concatenate([d0, d1, w0, w1, jnp.zeros((4, TD), F32)], axis=0)
    r_i = lax.broadcasted_iota(jnp.int32, (RL, TD), 0)
    perm = jnp.where((r_i == d0.astype(jnp.int32)) | (r_i == d1.astype(jnp.int32)), 1.0, 0.0).astype(BF16)
    loc = loc_s.at[slot]
    loc[...] = jnp.dot(perm, h_ref[...], preferred_element_type=F32).astype(BF16).reshape(RL // ALIGN, ALIGN, D)
    for e in range(NE):
        _run_copies(loc, xs_ref, lo_ref[i * NE + e], dst_ref[i * NE + e], pc_ref[i * NE + e], sems.at[slot],
                    RUN_BITS)

    @pl.when(i > 0)
    def _():
        _wait_groups(loc_s.at[1 - slot], ts_ref[jnp.maximum(i - 1, 0)], sems.at[1 - slot], TILE_BITS)

    @pl.when(i == last_step)
    def _():
        _wait_groups(loc, ts_ref[i], sems.at[slot], TILE_BITS)
        sem = sems.at[0]
        zero_s[...] = jnp.zeros_like(zero_s)
        zg = ZROWS // ALIGN
        for wait in (False, True):
            for e in range(NE):
                for part in range(TMX // ZROWS):
                    groups = jnp.clip(zl_ref[e] - part * zg, 0, zg)
                    if wait:
                        _wait_groups(zero_s, groups, sem, zg.bit_length())
                    else:
                        _run_copies(zero_s, xs_ref, 0, zd_ref[e] + part * zg, groups, sem, zg.bit_length())
        tail = zd_ref[NE]
        chunks = (xs_ref.shape[0] - tail) // zg

        def fill(c, carry):
            pltpu.make_async_copy(zero_s, xs_ref.at[pl.ds(tail + c * zg, zg)], sem).start()
            return carry

        def drain(c, carry):
            pltpu.make_async_copy(zero_s, zero_s, sem).wait()
            return carry

        lax.fori_loop(0, chunks, fill, 0)
        lax.fori_loop(0, chunks, drain, 0)


def _dispatch(plan, mask_t, gate_t, h2, rtot):
    n = h2.shape[0]
    ntd = n // TD
    tspec = pl.BlockSpec((NE, TD), lambda i, *_: (0, i))
    grid_spec = pltpu.PrefetchScalarGridSpec(
        num_scalar_prefetch=6, grid=(ntd,),
        in_specs=[tspec, tspec, pl.BlockSpec((TD, D), lambda i, *_: (i, 0))],
        out_specs=[pl.BlockSpec(memory_space=pl.ANY), pl.BlockSpec((8, TD), lambda i, *_: (0, i))],
        scratch_shapes=[pltpu.VMEM((2, RL // ALIGN, ALIGN, D), BF16), pltpu.VMEM((ZROWS // ALIGN, ALIGN, D), BF16),
                        pltpu.VMEM((TD, TD), BF16), pltpu.SemaphoreType.DMA((2,))])
    return pl.pallas_call(
        _dispatch_kernel, grid_spec=grid_spec,
        out_shape=[jax.ShapeDtypeStruct((rtot // ALIGN, ALIGN, D), BF16), jax.ShapeDtypeStruct((8, n), F32)],
        compiler_params=pltpu.CompilerParams(dimension_semantics=("arbitrary",), vmem_limit_bytes=VMEM_LIMIT),
        name="dispatch",
    )(plan["pc"], plan["lo"], plan["dst"], plan["tsum"], plan["zdst"], plan["zlen"], mask_t, gate_t, h2)


def _expert_kernel(texp_ref, nact_ref, x_ref, w1_ref, w3_ref, w2_ref, y_ref, w1_s, w3_s, w2_s):
    m = pl.program_id(0)
    new_expert = (m == 0) | (texp_ref[m] != texp_ref[jnp.maximum(m - 1, 0)])

    @pl.when(new_expert)
    def _():
        w1_s[...] = w1_ref[...].astype(BF16)
        w3_s[...] = w3_ref[...].astype(BF16)
        w2_s[...] = w2_ref[...].astype(BF16)

    @pl.when(m < nact_ref[0])
    def _():
        x = x_ref[...]
        a = jnp.dot(x, w1_s[...], preferred_element_type=F32)
        a = a * _sigmoid(a) * jnp.dot(x, w3_s[...], preferred_element_type=F32)
        y_ref[...] = jnp.dot(a.astype(BF16), w2_s[...], preferred_element_type=F32).astype(BF16)

    @pl.when(m >= nact_ref[0])
    def _():
        y_ref[...] = jnp.zeros_like(y_ref)


def _experts(plan, xs, layer, w1, w3, w2):
    rtot = xs.shape[0]
    n_mt = rtot // TMX
    grid_spec = pltpu.PrefetchScalarGridSpec(
        num_scalar_prefetch=2, grid=(n_mt,),
        in_specs=[pl.BlockSpec((TMX, D), lambda m, te, na: (jnp.minimum(m, na[0] - 1), 0)),
                  pl.BlockSpec((None, None, D, FF), lambda m, te, na: (layer, te[m], 0, 0)),
                  pl.BlockSpec((None, None, D, FF), lambda m, te, na: (layer, te[m], 0, 0)),
                  pl.BlockSpec((None, None, FF, D), lambda m, te, na: (layer, te[m], 0, 0))],
        out_specs=pl.BlockSpec((TMX, D), lambda m, te, na: (m, 0)),
        scratch_shapes=[pltpu.VMEM((D, FF), BF16), pltpu.VMEM((D, FF), BF16), pltpu.VMEM((FF, D), BF16)])
    return pl.pallas_call(
        _expert_kernel, grid_spec=grid_spec,
        out_shape=jax.ShapeDtypeStruct((rtot, D), BF16),
        compiler_params=pltpu.CompilerParams(dimension_semantics=("arbitrary",), vmem_limit_bytes=VMEM_LIMIT),
        name="experts",
    )(plan["texp"], plan["nact"], xs, w1, w3, w2)


def _combine_kernel(split, bidx_ref, pc_ref, lo_ref, dst_ref, ts_ref, info_ref, ys_ref, x1_ref, mod_ref,
                    lng_ref, lnb_ref, *rest):
    del bidx_ref
    o_refs, (loc_s, sems) = rest[:-2], rest[-2:]
    i = pl.program_id(0)
    slot = i % 2

    def fetch(tile, buf):
        for e in range(NE):
            _run_copies(ys_ref, loc_s.at[buf], dst_ref[tile * NE + e], lo_ref[tile * NE + e],
                        pc_ref[tile * NE + e], sems.at[buf], RUN_BITS)

    @pl.when(i == 0)
    def _():
        loc_s[...] = jnp.zeros_like(loc_s)
        fetch(0, 0)

    @pl.when(i + 1 < pl.num_programs(0))
    def _():
        fetch(i + 1, 1 - slot)

    _wait_groups(loc_s.at[slot], ts_ref[i], sems.at[slot], TILE_BITS)
    info = info_ref[...]
    d0 = info[0:1].astype(jnp.int32)
    d1 = info[1:2].astype(jnp.int32)
    r_i = lax.broadcasted_iota(jnp.int32, (RL, TD), 0)
    wperm = (jnp.where(r_i == d0, info[2:3], 0.0) + jnp.where(r_i == d1, info[3:4], 0.0)).astype(BF16)
    moe = lax.dot_general(wperm, loc_s[slot].reshape(RL, D), (((0,), (0,)), ((), ())),
                          preferred_element_type=F32)
    gate = mod_ref[...][:, 2 * D:]
    out = _ln(ALPHA * x1_ref[...] + gate * moe) * lng_ref[...] + lnb_ref[...]
    if len(o_refs) == 1:
        o_refs[0][...] = out
    else:
        @pl.when(i < split)
        def _():
            o_refs[0][...] = out

        @pl.when(i >= split)
        def _():
            o_refs[1][...] = out


def _combine(bidx_d, plan, info, ys, x1, mod2, lng, lnb, out_rows):
    n = x1.shape[0]
    ntd = n // TD
    tok = pl.BlockSpec((TD, D), lambda i, *_: (i, 0))
    full = lambda a: pl.BlockSpec(a.shape, lambda i, *_: (0,) * a.ndim)
    if len(out_rows) == 1:
        split, out_specs = 0, [tok]
    else:
        split = out_rows[0] // TD
        out_specs = [pl.BlockSpec((TD, D), lambda i, *_: (jnp.minimum(i, split - 1), 0)),
                     pl.BlockSpec((TD, D), lambda i, *_: (jnp.maximum(i - split, 0), 0))]
    grid_spec = pltpu.PrefetchScalarGridSpec(
        num_scalar_prefetch=5, grid=(ntd,),
        in_specs=[pl.BlockSpec((8, TD), lambda i, *_: (0, i)),
                  pl.BlockSpec(memory_space=pl.ANY),
                  tok,
                  pl.BlockSpec((None, 1, 3 * D), lambda i, b, *_: (b[i], 0, 0)),
                  full(lng), full(lnb)],
        out_specs=out_specs,
        scratch_shapes=[pltpu.VMEM((2, RL // ALIGN, ALIGN, D), BF16), pltpu.SemaphoreType.DMA((2,))])
    return pl.pallas_call(
        functools.partial(_combine_kernel, split), grid_spec=grid_spec,
        out_shape=[jax.ShapeDtypeStruct((r, D), F32) for r in out_rows],
        compiler_params=pltpu.CompilerParams(dimension_semantics=("arbitrary",), vmem_limit_bytes=VMEM_LIMIT),
        name="combine",
    )(bidx_d, plan["pc"], plan["lo"], plan["dst"], plan["tsum"], info, ys, x1, mod2, lng, lnb)


def _moe(meta, h2, mask_t, gate_t, cnt, layer, w1, w3, w2, x1, mod2, lng, lnb, out_rows):
    n = h2.shape[0]
    ntd = n // TD
    rtot = -(-(ntd * RL) // TMX) * TMX + NE * TMX
    cnt = cnt[:, :, :TM // TD].transpose(0, 2, 1).reshape(ntd, NE).astype(jnp.int32)
    plan = _plan(cnt, rtot // TMX)
    xs, info = _dispatch(plan, mask_t, gate_t, h2, rtot)
    ys = _experts(plan, xs.reshape(rtot, D), layer, w1, w3, w2).reshape(rtot // ALIGN, ALIGN, D)
    return _combine(meta["bidx"], plan, info, ys, x1, mod2, lng, lnb, out_rows)


def _tile_meta(groups, rows):
    bidx, pblk, first, last, cf, cb = [], [], [], [], [], []
    row = 0
    tile = 0
    for (b, s) in groups:
        assert s % rows == 0
        per = s // rows
        for bi in range(b):
            for j in range(per):
                bidx.append(row + bi)
                pblk.append(j)
                first.append(1 if j == 0 else 0)
                last.append(1 if j == per - 1 else 0)
                cf.append(tile + j)
                cb.append(tile + per - 1 - j)
            tile += per
        row += b
    as_i32 = lambda v: jnp.asarray(np.asarray(v, np.int32))
    return dict(bidx=as_i32(bidx), pblk=as_i32(pblk), first=as_i32(first), last=as_i32(last),
                cf=as_i32(cf), cb=as_i32(cb))


def _rope_tables(s_max):
    inv = jnp.power(ROPE_THETA, -jnp.arange(0, HD, 2, dtype=F32) / HD)
    ang = jnp.arange(s_max, dtype=F32)[:, None] * inv[None, :]
    cos, sin = jnp.cos(ang), jnp.sin(ang)
    cos128 = jnp.tile(cos, (1, 4))
    sin128 = jnp.tile(jnp.concatenate([-sin, sin], axis=1), (1, 2))
    return cos128, sin128


def _block_diag(w):
    d, nb, c, f = w.shape
    eye = jnp.eye(nb, dtype=w.dtype)
    return jnp.einsum('dncf,nm->dncmf', w, eye).reshape(d, nb * c, nb * f)


def _forward(xs, cs, w_mod, b_mod, w_in, attn_sink, conv_w, conv_b, lru_w_r, lru_b_r, lru_w_i, lru_b_i,
             lru_lambda, sg_norm_g, sg_w, sg_b, mix_norm_g, w_out, ln_g, ln_b, router_w, router_bias,
             exp_w1, exp_w3, exp_w2):
    groups = [(x.shape[0], x.shape[1]) for x in xs]
    assert len(groups) == 2
    meta_p, meta_a, meta_r = (_tile_meta(groups, rows) for rows in (TP, TA, TR))
    meta_m, meta_d = _tile_meta(groups, TM), _tile_meta(groups, TD)
    x_parts = [xx.reshape(-1, D) for xx in xs]
    group_rows = [p.shape[0] for p in x_parts]
    c_all = jnp.concatenate(cs, axis=0)
    bt = c_all.shape[0]
    mods = _modulation(c_all, w_mod, b_mod).reshape(2 * DEPTH, bt, 1, 3 * D)
    cos_t, sin_t = _rope_tables(max(s for _, s in groups))
    rwt = router_w.T.astype(BF16)
    rb = router_bias.reshape(NE, 1)
    for l in range(DEPTH):
        q, kk, vv, gy, xr, u, vn = _pre_mixer(meta_p, x_parts, mods[2 * l], w_in[l].astype(BF16), cos_t, sin_t,
                                              sg_norm_g[l].reshape(1, SGW))
        oa = _attention(meta_a, attn_sink[l], q, kk, vv)
        hf, hb = _recurrent(meta_r, xr, conv_w[l], conv_b[l].reshape(1, RW),
                            _block_diag(0.5 * lru_w_r[l]).astype(BF16), _block_diag(0.5 * lru_w_i[l]).astype(BF16),
                            0.5 * lru_b_r[l], 0.5 * lru_b_i[l], lru_lambda[l])
        sgb = jnp.repeat(sg_b[l].T, HD, axis=1)
        x1, h2, gate_t, mask_t, cnt = _merge(meta_m, x_parts, oa, gy, hf, hb, u, vn, mods[2 * l], mods[2 * l + 1],
                                             mix_norm_g[l].reshape(1, D), sg_w[l].astype(BF16), sgb,
                                             w_out[l].astype(BF16), ln_g[l, 0].reshape(1, D),
                                             ln_b[l, 0].reshape(1, D), rwt, rb)
        out_rows = group_rows if l == DEPTH - 1 else [sum(group_rows)]
        x_parts = _moe(meta_d, h2, mask_t, gate_t, cnt, l, exp_w1, exp_w3, exp_w2, x1, mods[2 * l + 1],
                       ln_g[l, 1].reshape(1, D), ln_b[l, 1].reshape(1, D), out_rows)
    return tuple(p.reshape(b, s, D) for p, (b, s) in zip(x_parts, groups))


def kernel(x_prompt, x_sample, c_prompt, c_sample, w_mod, b_mod, w_in, attn_sink, conv_w, conv_b, lru_w_r, lru_b_r,
           lru_w_i, lru_b_i, lru_lambda, sg_norm_g, sg_w, sg_b, mix_norm_g, w_out, ln_g, ln_b, router_w,
           router_bias, exp_w1, exp_w3, exp_w2):
    return _forward([x_prompt, x_sample], [c_prompt, c_sample], w_mod, b_mod, w_in, attn_sink, conv_w, conv_b,
                    lru_w_r, lru_b_r, lru_w_i, lru_b_i, lru_lambda, sg_norm_g, sg_w, sg_b, mix_norm_g, w_out,
                    ln_g, ln_b, router_w, router_bias, exp_w1, exp_w3, exp_w2)
```

```python
import functools

import numpy as np
import jax
import jax.numpy as jnp
from jax import lax
from jax.experimental import pallas as pl
from jax.experimental.pallas import tpu as pltpu

F32 = jnp.float32
BF16 = jnp.bfloat16

D = 1024
DEPTH = 2
HD = 64
AW = 384
KVW = 128
RW = 384
SGW = 256
WINDOW = 128
CONV_WIDTH = 4
LRU_C = 8.0
CHUNK = 128
NE = 16
NG = 4
EPG = NE // NG
FF = 512
ALPHA = (2 * DEPTH) ** 0.25
LN_EPS = 1e-5
RMS_EPS = 1e-6
ROPE_THETA = 10000.0

TM = 1024
TP = 1024
TA = 1024
TR = 1024
QB = 128
SEG = TR // 8
NLG = RW // 128
TD = 512
ALIGN = 16
RL = 2 * TD + NE * ALIGN
TMX = 1024
ZROWS = 256
NEG = -1e30
VMEM_LIMIT = 48 * 1024 * 1024

C_Q, C_K, C_V, C_Y, C_R, C_U, C_SV, C_END = 0, 384, 512, 640, 1024, 1408, 1664, 1920


def _ln(x):
    mu = jnp.mean(x, axis=-1, keepdims=True)
    xc = x - mu
    var = jnp.mean(xc * xc, axis=-1, keepdims=True)
    return xc * lax.rsqrt(var + LN_EPS)


def _rms(x):
    return x * lax.rsqrt(jnp.mean(x * x, axis=-1, keepdims=True) + RMS_EPS)


def _sigmoid(z):
    return 0.5 * (jnp.tanh(0.5 * z) + 1.0)


def _gelu(x):
    return 0.5 * x * (1.0 + jnp.tanh(0.7978845608028654 * (x + 0.044715 * (x * x * x))))


def _split_bf16(a):
    hi = a.astype(BF16)
    lo = (a - hi.astype(F32)).astype(BF16)
    return hi, lo


def _mod_kernel(c_ref, w_ref, b_ref, o_ref):
    c = c_ref[...]
    ca = c * _sigmoid(c)
    ch, cl = _split_bf16(ca)
    wh, wl = _split_bf16(w_ref[...])
    acc = jnp.dot(ch, wh, preferred_element_type=F32)
    acc += jnp.dot(ch, wl, preferred_element_type=F32)
    acc += jnp.dot(cl, wh, preferred_element_type=F32)
    o_ref[...] = acc + b_ref[...]


def _modulation(c_all, w_mod, b_mod):
    bt = c_all.shape[0]
    cb = 768
    w = w_mod.reshape(2 * DEPTH, D, 3 * D)
    b = b_mod.reshape(2 * DEPTH, 1, 3 * D)
    return pl.pallas_call(
        _mod_kernel,
        grid=(2 * DEPTH, 3 * D // cb),
        in_specs=[pl.BlockSpec((bt, D), lambda s, j: (0, 0)),
                  pl.BlockSpec((None, D, cb), lambda s, j: (s, 0, j)),
                  pl.BlockSpec((None, 1, cb), lambda s, j: (s, 0, j))],
        out_specs=pl.BlockSpec((None, bt, cb), lambda s, j: (s, 0, j)),
        out_shape=jax.ShapeDtypeStruct((2 * DEPTH, bt, 3 * D), F32),
        name="modulation",
    )(c_all, w, b)


def _load_tokens(i, x_refs, split):
    if len(x_refs) == 1:
        return x_refs[0][...]
    return jnp.where(i < split, x_refs[0][...], x_refs[1][...])


def _token_specs(x_parts, rows):
    if len(x_parts) == 1:
        return [pl.BlockSpec((rows, D), lambda i, *_: (i, 0))], 0
    split = x_parts[0].shape[0] // rows
    return [pl.BlockSpec((rows, D), lambda i, *_: (jnp.minimum(i, split - 1), 0)),
            pl.BlockSpec((rows, D), lambda i, *_: (jnp.maximum(i - split, 0), 0))], split


def _pre_kernel(split, bidx_ref, pblk_ref, *refs):
    del bidx_ref, pblk_ref
    nx = len(refs) - 12
    x_refs = refs[:nx]
    mod_ref, w_ref, cos_ref, sin_ref, sgg_ref, q_ref, kk_ref, vv_ref, gy_ref, xr_ref, u_ref, vn_ref = refs[nx:]
    mod = mod_ref[...]
    shift, scale = mod[:, :D], mod[:, D:2 * D]
    h = (_ln(_load_tokens(pl.program_id(0), x_refs, split)) * (1.0 + scale) + shift).astype(BF16)
    cos = cos_ref[...]
    sin = sin_ref[...]
    lane = lax.broadcasted_iota(jnp.int32, (1, 128), 1)
    first_half = (lane % HD) < (HD // 2)

    def rope128(xg):
        rot = jnp.where(first_half, pltpu.roll(xg, 128 - HD // 2, 1), pltpu.roll(xg, HD // 2, 1))
        return xg * cos + rot * sin

    q = jnp.dot(h, w_ref[:, C_Q:C_K], preferred_element_type=F32)
    for g in range(AW // 128):
        q_ref[:, g * 128:(g + 1) * 128] = (rope128(q[:, g * 128:(g + 1) * 128]) * (HD ** -0.5)).astype(BF16)
    kv = jnp.dot(h, w_ref[:, C_K:C_Y], preferred_element_type=F32)
    lo_half = lane < HD
    for src, dst_ref in ((rope128(kv[:, :KVW]), kk_ref), (kv[:, KVW:], vv_ref)):
        swapped = pltpu.roll(src, HD, 1)
        dst_ref[:, 0:128] = jnp.where(lo_half, src, swapped).astype(BF16)
        dst_ref[:, 128:256] = jnp.where(lo_half, swapped, src).astype(BF16)
    gy_ref[...] = _gelu(jnp.dot(h, w_ref[:, C_Y:C_R], preferred_element_type=F32)).astype(gy_ref.dtype)
    xr = jnp.dot(h, w_ref[:, C_R:C_U], preferred_element_type=F32)
    for g in range(NLG):
        xr_ref[g] = xr[:, g * 128:(g + 1) * 128]
    u_ref[...] = _gelu(jnp.dot(h, w_ref[:, C_U:C_SV], preferred_element_type=F32)).astype(u_ref.dtype)
    sv = _gelu(jnp.dot(h, w_ref[:, C_SV:C_END], preferred_element_type=F32))
    vn_ref[...] = (_ln(sv) * sgg_ref[...]).astype(BF16)


def _pre_mixer(meta, x_parts, mod_l, w_ext, cos_t, sin_t, sgg):
    n = sum(p.shape[0] for p in x_parts)
    nt = n // TP
    tok = lambda w: pl.BlockSpec((TP, w), lambda i, b, p: (i, 0))
    full = lambda a: pl.BlockSpec(a.shape, lambda i, b, p: (0,) * a.ndim)
    x_specs, split = _token_specs(x_parts, TP)
    grid_spec = pltpu.PrefetchScalarGridSpec(
        num_scalar_prefetch=2, grid=(nt,),
        in_specs=x_specs + [
            pl.BlockSpec((None, 1, 3 * D), lambda i, b, p: (b[i], 0, 0)),
            full(w_ext),
            pl.BlockSpec((TP, 128), lambda i, b, p: (p[i], 0)),
            pl.BlockSpec((TP, 128), lambda i, b, p: (p[i], 0)),
            full(sgg)],
        out_specs=[tok(AW), tok(256), tok(256), tok(RW),
                   pl.BlockSpec((NLG, TP, 128), lambda i, b, p: (0, i, 0)), tok(SGW), tok(SGW)])
    sds = lambda w, dt: jax.ShapeDtypeStruct((n, w), dt)
    return pl.pallas_call(
        functools.partial(_pre_kernel, split), grid_spec=grid_spec,
        out_shape=[sds(AW, BF16), sds(256, BF16), sds(256, BF16), sds(RW, BF16),
                   jax.ShapeDtypeStruct((NLG, n, 128), F32), sds(SGW, BF16), sds(SGW, BF16)],
        compiler_params=pltpu.CompilerParams(dimension_semantics=("parallel",), vmem_limit_bytes=VMEM_LIMIT),
        name="pre_mixer",
    )(meta["bidx"], meta["pblk"], *x_parts, mod_l, w_ext, cos_t, sin_t, sgg)


def _attn_kernel(first_ref, last_ref, sink_ref, q_ref, k_ref, kp_ref, kn_ref, v_ref, vp_ref, vn_ref,
                 o_ref, kw_s, vw_s):
    i = pl.program_id(0)
    is_first = first_ref[i] == 1
    is_last = last_ref[i] == 1
    kw_s[0:QB] = kp_ref[...]
    kw_s[QB:QB + TA] = k_ref[...]
    kw_s[QB + TA:] = kn_ref[...]
    vw_s[0:QB] = vp_ref[...]
    vw_s[QB:QB + TA] = v_ref[...]
    vw_s[QB + TA:] = vn_ref[...]
    iq = lax.broadcasted_iota(jnp.int32, (QB, 3 * QB), 0)
    ik = lax.broadcasted_iota(jnp.int32, (QB, 3 * QB), 1)
    rel = ik - iq
    band = (rel >= 0) & (rel <= 2 * WINDOW)
    lane = lax.broadcasted_iota(jnp.int32, (1, 128), 1)
    lo_half = lane < HD
    zero = jnp.zeros((), BF16)
    gsz = AW // HD // 2
    row_head = lax.broadcasted_iota(jnp.int32, (gsz * QB, 1), 0) // QB
    for j in range(TA // QB):
        ok = band
        if j == 0:
            ok = ok & ((ik >= QB) | jnp.logical_not(is_first))
        if j == TA // QB - 1:
            ok = ok & ((ik < 2 * QB) | jnp.logical_not(is_last))
        bias = jnp.where(ok, 0.0, NEG)
        bias = jnp.concatenate([bias] * gsz, axis=0)
        rows = slice(j * QB, (j + 1) * QB)
        keys = slice(j * QB, j * QB + 3 * QB)
        outs = [None] * (AW // HD)
        for g in range(2):
            heads = range(g * gsz, (g + 1) * gsz)
            qs = []
            for hh in heads:
                qg = q_ref[rows, (hh // 2) * 128:(hh // 2 + 1) * 128]
                qs.append(jnp.where(lo_half if hh % 2 == 0 else jnp.logical_not(lo_half), qg, zero))
            s = lax.dot_general(jnp.concatenate(qs, axis=0), kw_s[keys, g * 128:(g + 1) * 128],
                                (((1,), (1,)), ((), ())), preferred_element_type=F32) + bias
            sink = jnp.full((gsz * QB, 1), sink_ref[heads[-1]], F32)
            for t in range(gsz - 1):
                sink = jnp.where(row_head == t, sink_ref[heads[t]], sink)
            m = jnp.maximum(jnp.max(s, axis=-1, keepdims=True), sink)
            p = jnp.exp(s - m)
            denom = jnp.sum(p, axis=-1, keepdims=True) + jnp.exp(sink - m)
            pv = jnp.dot(p.astype(BF16), vw_s[keys, g * 128:(g + 1) * 128], preferred_element_type=F32)
            o = pv * (1.0 / denom)
            for t, hh in enumerate(heads):
                outs[hh] = o[t * QB:(t + 1) * QB]
        for jg in range(AW // 128):
            o_ref[rows, jg * 128:(jg + 1) * 128] = jnp.where(lo_half, outs[2 * jg], outs[2 * jg + 1]).astype(o_ref.dtype)


def _attention(meta, sink, q, kk, vv):
    n = q.shape[0]
    nt = n // TA
    nqb = n // QB
    r = TA // QB
    main = lambda w: pl.BlockSpec((TA, w), lambda i, f, l, s: (i, 0))
    prev = pl.BlockSpec((QB, 256), lambda i, f, l, s: (jnp.maximum(i * r - 1, 0), 0))
    nxt = pl.BlockSpec((QB, 256), lambda i, f, l, s: (jnp.minimum(i * r + r, nqb - 1), 0))
    grid_spec = pltpu.PrefetchScalarGridSpec(
        num_scalar_prefetch=3, grid=(nt,),
        in_specs=[main(AW), main(256), prev, nxt, main(256), prev, nxt],
        out_specs=main(AW),
        scratch_shapes=[pltpu.VMEM((TA + 2 * QB, 256), BF16), pltpu.VMEM((TA + 2 * QB, 256), BF16)])
    return pl.pallas_call(
        _attn_kernel, grid_spec=grid_spec,
        out_shape=jax.ShapeDtypeStruct((n, AW), BF16),
        compiler_params=pltpu.CompilerParams(dimension_semantics=("parallel",), vmem_limit_bytes=VMEM_LIMIT),
        name="attention",
    )(meta["first"], meta["last"], sink, q, kk, kk, kk, vv, vv, vv)


def _rec_kernel(cf_ref, cb_ref, first_ref, last_ref,
                x_hbm, xfp_ref, xfn_ref, xbp_ref, xbn_ref,
                cw_ref, cbias_ref, wr_ref, wi_ref, br_ref, bi_ref, lam_ref,
                hf_hbm, hb_hbm,
                ext_s, a_s, b_s, hs_s, ps_s, out_s, carry_s, gsem, ssem):
    i = pl.program_id(0)
    nsteps = pl.num_programs(0)
    slot = i % 2
    cw = cw_ref[...]
    cbias = cbias_ref[...]
    sub = lax.broadcasted_iota(jnp.int32, (8, 128), 0)
    unroll = 8
    chunk_refs = (cf_ref, cb_ref)
    out_hbm = (hf_hbm, hb_hbm)

    def gather(step, buf, d, start):
        c = chunk_refs[d][step]
        for g in range(NLG):
            for s in range(8):
                cp = pltpu.make_async_copy(x_hbm.at[g, pl.ds(pl.multiple_of(c * TR + s * SEG, SEG), SEG), :],
                                           ext_s.at[buf, d, g, pl.ds(2, SEG), s, :], gsem.at[buf, d])
                if start:
                    cp.start()
                else:
                    cp.wait()

    def scatter(d, c, start):
        for g in range(NLG):
            for s in range(8):
                cp = pltpu.make_async_copy(out_s.at[d, g, :, s, :],
                                           out_hbm[d].at[g, pl.ds(pl.multiple_of(c * TR + s * SEG, SEG), SEG), :],
                                           ssem.at[d])
                if start:
                    cp.start()
                else:
                    cp.wait()

    @pl.when(i == 0)
    def _():
        gather(0, 0, 0, True)
        gather(0, 0, 1, True)

    @pl.when(i + 1 < nsteps)
    def _():
        gather(i + 1, 1 - slot, 0, True)
        gather(i + 1, 1 - slot, 1, True)

    def run(d, xp_ref, xn_ref):
        c = chunk_refs[d][i]
        first = first_ref[c] == 1
        last = last_ref[c] == 1
        gather(i, slot, d, False)
        ext = ext_s.at[slot, d]
        planes = []
        for g in range(NLG):
            prev = jnp.where(first, 0.0, xp_ref[g])
            nxt = jnp.where(last, 0.0, xn_ref[g])
            ext[g, 0] = jnp.where(sub == 0, prev[6:7], pltpu.roll(ext[g, SEG], 1, 0))
            ext[g, 1] = jnp.where(sub == 0, prev[7:8], pltpu.roll(ext[g, SEG + 1], 1, 0))
            ext[g, SEG + 2] = jnp.where(sub == 7, nxt[0:1], pltpu.roll(ext[g, 2], 7, 0))
            lanes = slice(g * 128, (g + 1) * 128)
            acc = ext[g, 0:SEG] * cw[0:1, lanes]
            for t in range(1, CONV_WIDTH):
                acc = acc + ext[g, t:t + SEG] * cw[t:t + 1, lanes]
            planes.append(acc.reshape(TR, 128))
        xc = jnp.concatenate(planes, axis=1) + cbias
        xb16 = xc.astype(BF16)
        tr = jnp.tanh(jnp.dot(xb16, wr_ref[d], preferred_element_type=F32) + br_ref[d:d + 1])
        ti = jnp.tanh(jnp.dot(xb16, wi_ref[d], preferred_element_type=F32) + bi_ref[d:d + 1])
        nlam = -lam_ref[d:d + 1]
        softplus = jnp.maximum(nlam, 0.0) + jnp.log1p(jnp.exp(-jnp.abs(nlam)))
        half_c = (-0.5 * LRU_C) * softplus
        log_a = half_c * tr + half_c
        a = jnp.exp(log_a)
        v = (1.0 + a * a) * jnp.tanh(-log_a)
        b = jnp.where(v > 0.0, v * lax.rsqrt(v), 0.0) * (ti + 1.0) * (0.5 * xc)
        for g in range(NLG):
            a_s[g] = a[:, g * 128:(g + 1) * 128].reshape(SEG, 8, 128)
            b_s[g] = b[:, g * 128:(g + 1) * 128].reshape(SEG, 8, 128)

        reverse = d == 1
        reset = last if reverse else first

        @pl.when(reset)
        def _():
            carry_s[d] = jnp.zeros((NLG, 128), F32)

        def step(jo, hp):
            hp = list(hp)
            for ji in range(unroll):
                jj = jo * unroll + ji
                j = SEG - 1 - jj if reverse else jj
                for g in range(NLG):
                    h, p = hp[g]
                    ag = a_s[g, j]
                    h = ag * h + b_s[g, j]
                    p = p * ag
                    hs_s[g, j] = h
                    ps_s[g, j] = p
                    hp[g] = (h, p)
            return tuple(hp)

        init = tuple((jnp.zeros((8, 128), F32), jnp.ones((8, 128), F32)) for _ in range(NLG))
        ends = lax.fori_loop(0, SEG // unroll, step, init)
        cmats = []
        for g in range(NLG):
            e, pe = ends[g]
            c_in = carry_s[d, g:g + 1]
            rows = [None] * 8
            for s in (range(7, -1, -1) if reverse else range(8)):
                rows[s] = c_in
                c_in = e[s:s + 1] + pe[s:s + 1] * c_in
            carry_s[d, g:g + 1] = c_in
            cmats.append(jnp.concatenate(rows, axis=0))

        @pl.when(i > 0)
        def _():
            scatter(d, 0, False)

        def fix(jo, carry):
            for ji in range(unroll):
                j = jo * unroll + ji
                for g in range(NLG):
                    out_s[d, g, j] = hs_s[g, j] + ps_s[g, j] * cmats[g]
            return carry

        lax.fori_loop(0, SEG // unroll, fix, 0)
        scatter(d, c, True)

    run(0, xfp_ref, xfn_ref)
    run(1, xbp_ref, xbn_ref)

    @pl.when(i == nsteps - 1)
    def _():
        scatter(0, 0, False)
        scatter(1, 0, False)


def _recurrent(meta, xr, cw, cbias, wr, wi, br, bi, lam):
    n = xr.shape[1]
    nt = n // TR
    n8 = n // 8
    r8 = TR // 8

    def halos(which):
        sel = (lambda cf, cb: cf) if which == 0 else (lambda cf, cb: cb)
        prev = pl.BlockSpec((NLG, 8, 128),
                            lambda i, cf, cb, f, l: (0, jnp.maximum(sel(cf, cb)[i] * r8 - 1, 0), 0))
        nxt = pl.BlockSpec((NLG, 8, 128),
                           lambda i, cf, cb, f, l: (0, jnp.minimum(sel(cf, cb)[i] * r8 + r8, n8 - 1), 0))
        return prev, nxt

    full = lambda a: pl.BlockSpec(a.shape, lambda i, cf, cb, f, l: (0,) * a.ndim)
    anywhere = pl.BlockSpec(memory_space=pl.ANY)
    plane = lambda lead: pltpu.VMEM(lead + (NLG, SEG, 8, 128), F32)
    grid_spec = pltpu.PrefetchScalarGridSpec(
        num_scalar_prefetch=4, grid=(nt,),
        in_specs=[anywhere, *halos(0), *halos(1),
                  full(cw), full(cbias), full(wr), full(wi), full(br), full(bi), full(lam)],
        out_specs=[anywhere, anywhere],
        scratch_shapes=[pltpu.VMEM((2, 2, NLG, SEG + 3, 8, 128), F32),
                        plane(()), plane(()), plane(()), plane(()), plane((2,)),
                        pltpu.VMEM((2, NLG, 128), F32),
                        pltpu.SemaphoreType.DMA((2, 2)), pltpu.SemaphoreType.DMA((2,))])
    return pl.pallas_call(
        _rec_kernel, grid_spec=grid_spec,
        out_shape=[jax.ShapeDtypeStruct((NLG, n, 128), F32), jax.ShapeDtypeStruct((NLG, n, 128), F32)],
        compiler_params=pltpu.CompilerParams(dimension_semantics=("arbitrary",), vmem_limit_bytes=VMEM_LIMIT),
        name="recurrent",
    )(meta["cf"], meta["cb"], meta["first"], meta["last"], xr, xr, xr, xr, xr, cw, cbias, wr, wi, br, bi, lam)


def _route(sel, score):
    def before(vk, vj, k, j):
        return (vk > vj) | ((vk == vj) & (k < j)) if k < j else (vk > vj)

    def count(flags):
        return sum(jnp.where(f, 1.0, 0.0) for f in flags)

    slot = lambda a, j: a[j * NG:(j + 1) * NG]
    cand = [slot(sel, j) for j in range(EPG)]
    in_top = [jnp.where(count(before(cand[k], cand[j], k, j) for k in range(EPG) if k != j) < 2.0, 1.0, 0.0)
              for j in range(EPG)]
    gscore = sum(in_top[j] * cand[j] for j in range(EPG))
    grow = [gscore[g:g + 1] for g in range(NG)]
    best = jnp.concatenate(
        [jnp.where(count(before(grow[k], grow[g], k, g) for k in range(NG) if k != g) < 1.0, 1.0, 0.0)
         for g in range(NG)], axis=0)
    picked = [in_top[j] * best * slot(score, j) for j in range(EPG)]
    inv = 1.0 / sum(jnp.sum(p, axis=0, keepdims=True) for p in picked)
    mask = jnp.concatenate([in_top[j] * best for j in range(EPG)], axis=0)
    gate = jnp.concatenate([p * inv for p in picked], axis=0)
    return mask, gate


def _slot_order():
    return np.asarray([g * EPG + j for j in range(EPG) for g in range(NG)], np.int32)


def _merge_kernel(split, bidx_ref, *refs):
    del bidx_ref
    nx = len(refs) - 22
    x_refs = refs[:nx]
    (oa_ref, gy_ref, hf_ref, hb_ref, u_ref, vn_ref, mod1_ref, mod2_ref, gmix_ref, sgw_ref, sgb_ref, wout_ref,
     lng_ref, lnb_ref, rwt_ref, rb_ref, x1_ref, h2_ref, gate_ref, mask_ref, cnt_ref, mrg_s) = refs[nx:]
    gmix = gmix_ref[...]
    mrg_s[:, 0:AW] = (_rms(oa_ref[...].astype(F32)) * gmix[:, 0:AW]).astype(BF16)
    hsum = jnp.concatenate([hf_ref[g] + hb_ref[g] for g in range(NLG)], axis=1)
    o_rec = gy_ref[...].astype(F32) * hsum
    mrg_s[:, AW:AW + RW] = (_rms(o_rec) * gmix[:, AW:AW + RW]).astype(BF16)
    lane = lax.broadcasted_iota(jnp.int32, (1, 128), 1)
    lo_half = lane < HD
    zero = jnp.zeros((), BF16)
    pieces = []
    for c in range(TM // CHUNK):
        rows = slice(c * CHUNK, (c + 1) * CHUNK)
        grp = []
        for g in range(SGW // 128):
            vg = vn_ref[rows, g * 128:(g + 1) * 128]
            mixed = jnp.dot(sgw_ref[2 * g], jnp.where(lo_half, vg, zero), preferred_element_type=F32)
            mixed += jnp.dot(sgw_ref[2 * g + 1], jnp.where(lo_half, zero, vg), preferred_element_type=F32)
            grp.append(mixed)
        mixed = jnp.concatenate(grp, axis=1) + sgb_ref[...]
        pieces.append(u_ref[rows, :].astype(F32) * mixed)
    o_sg = jnp.concatenate(pieces, axis=0)
    mrg_s[:, AW + RW:] = (_rms(o_sg) * gmix[:, AW + RW:]).astype(BF16)
    o = jnp.dot(mrg_s[...], wout_ref[...], preferred_element_type=F32)
    gate1 = mod1_ref[...][:, 2 * D:]
    x_in = _load_tokens(pl.program_id(0), x_refs, split)
    x1 = _ln(ALPHA * x_in + gate1 * o) * lng_ref[...] + lnb_ref[...]
    x1_ref[...] = x1
    mod2 = mod2_ref[...]
    h2 = (_ln(x1) * (1.0 + mod2[:, D:2 * D]) + mod2[:, :D]).astype(BF16)
    h2_ref[...] = h2
    logits = lax.dot_general(rwt_ref[...], h2, (((1,), (1,)), ((), ())), preferred_element_type=F32)
    score = _sigmoid(logits)
    sel = score + rb_ref[...]
    mask_t, gate_t = _route(sel, score)
    gate_ref[...] = gate_t
    mask_ref[...] = mask_t
    t_i = lax.broadcasted_iota(jnp.int32, (TM, 128), 0)
    j_i = lax.broadcasted_iota(jnp.int32, (TM, 128), 1)
    in_tile = jnp.where((t_i >= j_i * TD) & (t_i < (j_i + 1) * TD), 1.0, 0.0).astype(BF16)
    cnt_ref[...] = jnp.dot(mask_t.astype(BF16), in_tile, preferred_element_type=F32)


def _merge(meta, x_parts, oa, gy, hf, hb, u, vn, mod1, mod2, gmix, sgw, sgb, wout, lng, lnb, rwt, rb):
    n = oa.shape[0]
    nt = n // TM
    x_specs, split = _token_specs(x_parts, TM)
    tok = lambda w: pl.BlockSpec((TM, w), lambda i, b: (i, 0))
    full = lambda a: pl.BlockSpec(a.shape, lambda i, b: (0,) * a.ndim)
    modspec = pl.BlockSpec((None, 1, 3 * D), lambda i, b: (b[i], 0, 0))
    tspec = pl.BlockSpec((NE, TM), lambda i, b: (0, i))
    rec = pl.BlockSpec((NLG, TM, 128), lambda i, b: (0, i, 0))
    grid_spec = pltpu.PrefetchScalarGridSpec(
        num_scalar_prefetch=1, grid=(nt,),
        in_specs=x_specs + [
            tok(AW), tok(RW), rec, rec, tok(SGW), tok(SGW), modspec, modspec,
            full(gmix), full(sgw), full(sgb), full(wout), full(lng), full(lnb), full(rwt), full(rb)],
        out_specs=[tok(D), tok(D), tspec, tspec, pl.BlockSpec((None, NE, 128), lambda i, b: (i, 0, 0))],
        scratch_shapes=[pltpu.VMEM((TM, D), BF16)])
    return pl.pallas_call(
        functools.partial(_merge_kernel, split), grid_spec=grid_spec,
        out_shape=[jax.ShapeDtypeStruct((n, D), F32), jax.ShapeDtypeStruct((n, D), BF16),
                   jax.ShapeDtypeStruct((NE, n), F32), jax.ShapeDtypeStruct((NE, n), F32),
                   jax.ShapeDtypeStruct((nt, NE, 128), F32)],
        compiler_params=pltpu.CompilerParams(dimension_semantics=("parallel",), vmem_limit_bytes=VMEM_LIMIT),
        name="merge_route",
    )(meta["bidx"], *x_parts, oa, gy, hf, hb, u, vn, mod1, mod2, gmix, sgw, sgb, wout, lng, lnb, rwt, rb)


def _plan(cnt, n_mt):
    pc = (cnt + (ALIGN - 1)) // ALIGN * ALIGN
    lo = jnp.cumsum(pc, axis=1) - pc
    tot = jnp.sum(pc, axis=0)
    seg = (tot + (TMX - 1)) // TMX * TMX
    gend = jnp.cumsum(seg)
    gstart = gend - seg
    dst = gstart[None, :] + jnp.cumsum(pc, axis=0) - pc
    tile_row = jnp.arange(n_mt, dtype=jnp.int32) * TMX
    texp = jnp.minimum(jnp.sum((gend[None, :] <= tile_row[:, None]).astype(jnp.int32), axis=1), NE - 1)
    texp = jnp.asarray(_slot_order())[texp]
    nact = (gend[-1] // TMX).reshape(1)
    grp = lambda a: (a // ALIGN).astype(jnp.int32)
    return dict(pc=grp(pc).reshape(-1), lo=grp(lo).reshape(-1), dst=grp(dst).reshape(-1),
                tsum=grp(jnp.sum(pc, axis=1)),
                zdst=grp(jnp.concatenate([gstart + tot, gend[-1:]])), zlen=grp(seg - tot),
                texp=texp.astype(jnp.int32), nact=nact.astype(jnp.int32))


def _run_copies(src, dst, src_off, dst_off, groups, sem, max_bits):
    def arm(b):
        @pl.when(((groups >> b) & 1) == 1)
        def _():
            off = (groups >> (b + 1)) << (b + 1)
            pltpu.make_async_copy(src.at[pl.ds(src_off + off, 1 << b)], dst.at[pl.ds(dst_off + off, 1 << b)],
                                  sem).start()

    for b in range(min(COMMON_BITS, max_bits)):
        arm(b)
    if max_bits > COMMON_BITS:
        @pl.when(groups >= (1 << COMMON_BITS))
        def _():
            for b in range(COMMON_BITS, max_bits):
                arm(b)


def _wait_groups(buf, groups, sem, max_bits):
    for b in range(max_bits):
        @pl.when(((groups >> b) & 1) == 1)
        def _():
            pltpu.make_async_copy(buf.at[pl.ds(0, 1 << b)], buf.at[pl.ds(0, 1 << b)], sem).wait()


COMMON_BITS = 3
RUN_BITS = (TD // ALIGN).bit_length()
TILE_BITS = (RL // ALIGN).bit_length()


def _dispatch_kernel(pc_ref, lo_ref, dst_ref, ts_ref, zd_ref, zl_ref, mask_ref, gate_ref, h_ref,
                     xs_ref, info_ref, loc_s, zero_s, earlier_s, sems):
    i = pl.program_id(0)
    last_step = pl.num_programs(0) - 1
    slot = i % 2
    mask = mask_ref[...]
    mb = mask.astype(BF16)
    @pl.when(i == 0)
    def _():
        s_i = lax.broadcasted_iota(jnp.int32, (TD, TD), 0)
        t_i = lax.broadcasted_iota(jnp.int32, (TD, TD), 1)
        earlier_s[...] = jnp.where(s_i < t_i, 1.0, 0.0).astype(BF16)

    rank = jnp.dot(mb, earlier_s[...], preferred_element_type=F32)
    e_i = lax.broadcasted_iota(jnp.int32, (NE, NE), 0)
    f_i = lax.broadcasted_iota(jnp.int32, (NE, NE), 1)
    below = jnp.where(f_i < e_i, 1.0, 0.0).astype(BF16)
    lower = jnp.dot(below, mb, preferred_element_type=F32)
    row_e = lax.broadcasted_iota(jnp.int32, (NE, 1), 0)
    lo_vec = jnp.zeros((NE, 1), F32)
    for e in range(NE):
        lo_vec = jnp.where(row_e == e, (lo_ref[i * NE + e] * ALIGN).astype(F32), lo_vec)
    row = lo_vec + rank
    is0 = mask * jnp.where(lower == 0.0, 1.0, 0.0)
    is1 = mask - is0
    d0 = jnp.sum(is0 * row, axis=0, keepdims=True)
    d1 = jnp.sum(is1 * row, axis=0, keepdims=True)
    gate = gate_ref[...]
    w0 = jnp.sum(is0 * gate, axis=0, keepdims=True)
    w1 = jnp.sum(is1 * gate, axis=0, keepdims=True)
    info_ref[...] = jnp.concatenate([d0, d1, w0, w1, jnp.zeros((4, TD), F32)], axis=0)
    r_i = lax.broadcasted_iota(jnp.int32, (RL, TD), 0)
    perm = jnp.where((r_i == d0.astype(jnp.int32)) | (r_i == d1.astype(jnp.int32)), 1.0, 0.0).astype(BF16)
    loc = loc_s.at[slot]
    loc[...] = jnp.dot(perm, h_ref[...], preferred_element_type=F32).astype(BF16).reshape(RL // ALIGN, ALIGN, D)
    for e in range(NE):
        _run_copies(loc, xs_ref, lo_ref[i * NE + e], dst_ref[i * NE + e], pc_ref[i * NE + e], sems.at[slot],
                    RUN_BITS)

    @pl.when(i > 0)
    def _():
        _wait_groups(loc_s.at[1 - slot], ts_ref[jnp.maximum(i - 1, 0)], sems.at[1 - slot], TILE_BITS)

    @pl.when(i == last_step)
    def _():
        _wait_groups(loc, ts_ref[i], sems.at[slot], TILE_BITS)
        sem = sems.at[0]
        zero_s[...] = jnp.zeros_like(zero_s)
        zg = ZROWS // ALIGN
        for wait in (False, True):
            for e in range(NE):
                for part in range(TMX // ZROWS):
                    groups = jnp.clip(zl_ref[e] - part * zg, 0, zg)
                    if wait:
                        _wait_groups(zero_s, groups, sem, zg.bit_length())
                    else:
                        _run_copies(zero_s, xs_ref, 0, zd_ref[e] + part * zg, groups, sem, zg.bit_length())
        tail = zd_ref[NE]
        chunks = (xs_ref.shape[0] - tail) // zg

        def fill(c, carry):
            pltpu.make_async_copy(zero_s, xs_ref.at[pl.ds(tail + c * zg, zg)], sem).start()
            return carry

        def drain(c, carry):
            pltpu.make_async_copy(zero_s, zero_s, sem).wait()
            return carry

        lax.fori_loop(0, chunks, fill, 0)
        lax.fori_loop(0, chunks, drain, 0)


def _dispatch(plan, mask_t, gate_t, h2, rtot):
    n = h2.shape[0]
    ntd = n // TD
    tspec = pl.BlockSpec((NE, TD), lambda i, *_: (0, i))
    grid_spec = pltpu.PrefetchScalarGridSpec(
        num_scalar_prefetch=6, grid=(ntd,),
        in_specs=[tspec, tspec, pl.BlockSpec((TD, D), lambda i, *_: (i, 0))],
        out_specs=[pl.BlockSpec(memory_space=pl.ANY), pl.BlockSpec((8, TD), lambda i, *_: (0, i))],
        scratch_shapes=[pltpu.VMEM((2, RL // ALIGN, ALIGN, D), BF16), pltpu.VMEM((ZROWS // ALIGN, ALIGN, D), BF16),
                        pltpu.VMEM((TD, TD), BF16), pltpu.SemaphoreType.DMA((2,))])
    return pl.pallas_call(
        _dispatch_kernel, grid_spec=grid_spec,
        out_shape=[jax.ShapeDtypeStruct((rtot // ALIGN, ALIGN, D), BF16), jax.ShapeDtypeStruct((8, n), F32)],
        compiler_params=pltpu.CompilerParams(dimension_semantics=("arbitrary",), vmem_limit_bytes=VMEM_LIMIT),
        name="dispatch",
    )(plan["pc"], plan["lo"], plan["dst"], plan["tsum"], plan["zdst"], plan["zlen"], mask_t, gate_t, h2)


def _expert_kernel(texp_ref, nact_ref, x_ref, w1_ref, w3_ref, w2_ref, y_ref, w1_s, w3_s, w2_s):
    m = pl.program_id(0)
    new_expert = (m == 0) | (texp_ref[m] != texp_ref[jnp.maximum(m - 1, 0)])

    @pl.when(new_expert)
    def _():
        w1_s[...] = w1_ref[...].astype(BF16)
        w3_s[...] = w3_ref[...].astype(BF16)
        w2_s[...] = w2_ref[...].astype(BF16)

    @pl.when(m < nact_ref[0])
    def _():
        x = x_ref[...]
        a = jnp.dot(x, w1_s[...], preferred_element_type=F32)
        a = a * _sigmoid(a) * jnp.dot(x, w3_s[...], preferred_element_type=F32)
        y_ref[...] = jnp.dot(a.astype(BF16), w2_s[...], preferred_element_type=F32).astype(BF16)

    @pl.when(m >= nact_ref[0])
    def _():
        y_ref[...] = jnp.zeros_like(y_ref)


def _experts(plan, xs, layer, w1, w3, w2):
    rtot = xs.shape[0]
    n_mt = rtot // TMX
    grid_spec = pltpu.PrefetchScalarGridSpec(
        num_scalar_prefetch=2, grid=(n_mt,),
        in_specs=[pl.BlockSpec((TMX, D), lambda m, te, na: (jnp.minimum(m, na[0] - 1), 0)),
                  pl.BlockSpec((None, None, D, FF), lambda m, te, na: (layer, te[m], 0, 0)),
                  pl.BlockSpec((None, None, D, FF), lambda m, te, na: (layer, te[m], 0, 0)),
                  pl.BlockSpec((None, None, FF, D), lambda m, te, na: (layer, te[m], 0, 0))],
        out_specs=pl.BlockSpec((TMX, D), lambda m, te, na: (m, 0)),
        scratch_shapes=[pltpu.VMEM((D, FF), BF16), pltpu.VMEM((D, FF), BF16), pltpu.VMEM((FF, D), BF16)])
    return pl.pallas_call(
        _expert_kernel, grid_spec=grid_spec,
        out_shape=jax.ShapeDtypeStruct((rtot, D), BF16),
        compiler_params=pltpu.CompilerParams(dimension_semantics=("arbitrary",), vmem_limit_bytes=VMEM_LIMIT),
        name="experts",
    )(plan["texp"], plan["nact"], xs, w1, w3, w2)


def _combine_kernel(split, bidx_ref, pc_ref, lo_ref, dst_ref, ts_ref, info_ref, ys_ref, x1_ref, mod_ref,
                    lng_ref, lnb_ref, *rest):
    del bidx_ref
    o_refs, (loc_s, sems) = rest[:-2], rest[-2:]
    i = pl.program_id(0)
    slot = i % 2

    def fetch(tile, buf):
        for e in range(NE):
            _run_copies(ys_ref, loc_s.at[buf], dst_ref[tile * NE + e], lo_ref[tile * NE + e],
                        pc_ref[tile * NE + e], sems.at[buf], RUN_BITS)

    @pl.when(i == 0)
    def _():
        loc_s[...] = jnp.zeros_like(loc_s)
        fetch(0, 0)

    @pl.when(i + 1 < pl.num_programs(0))
    def _():
        fetch(i + 1, 1 - slot)

    _wait_groups(loc_s.at[slot], ts_ref[i], sems.at[slot], TILE_BITS)
    info = info_ref[...]
    d0 = info[0:1].astype(jnp.int32)
    d1 = info[1:2].astype(jnp.int32)
    r_i = lax.broadcasted_iota(jnp.int32, (RL, TD), 0)
    wperm = (jnp.where(r_i == d0, info[2:3], 0.0) + jnp.where(r_i == d1, info[3:4], 0.0)).astype(BF16)
    moe = lax.dot_general(wperm, loc_s[slot].reshape(RL, D), (((0,), (0,)), ((), ())),
                          preferred_element_type=F32)
    gate = mod_ref[...][:, 2 * D:]
    out = _ln(ALPHA * x1_ref[...] + gate * moe) * lng_ref[...] + lnb_ref[...]
    if len(o_refs) == 1:
        o_refs[0][...] = out
    else:
        @pl.when(i < split)
        def _():
            o_refs[0][...] = out

        @pl.when(i >= split)
        def _():
            o_refs[1][...] = out


def _combine(bidx_d, plan, info, ys, x1, mod2, lng, lnb, out_rows):
    n = x1.shape[0]
    ntd = n // TD
    tok = pl.BlockSpec((TD, D), lambda i, *_: (i, 0))
    full = lambda a: pl.BlockSpec(a.shape, lambda i, *_: (0,) * a.ndim)
    if len(out_rows) == 1:
        split, out_specs = 0, [tok]
    else:
        split = out_rows[0] // TD
        out_specs = [pl.BlockSpec((TD, D), lambda i, *_: (jnp.minimum(i, split - 1), 0)),
                     pl.BlockSpec((TD, D), lambda i, *_: (jnp.maximum(i - split, 0), 0))]
    grid_spec = pltpu.PrefetchScalarGridSpec(
        num_scalar_prefetch=5, grid=(ntd,),
        in_specs=[pl.BlockSpec((8, TD), lambda i, *_: (0, i)),
                  pl.BlockSpec(memory_space=pl.ANY),
                  tok,
                  pl.BlockSpec((None, 1, 3 * D), lambda i, b, *_: (b[i], 0, 0)),
                  full(lng), full(lnb)],
        out_specs=out_specs,
        scratch_shapes=[pltpu.VMEM((2, RL // ALIGN, ALIGN, D), BF16), pltpu.SemaphoreType.DMA((2,))])
    return pl.pallas_call(
        functools.partial(_combine_kernel, split), grid_spec=grid_spec,
        out_shape=[jax.ShapeDtypeStruct((r, D), F32) for r in out_rows],
        compiler_params=pltpu.CompilerParams(dimension_semantics=("arbitrary",), vmem_limit_bytes=VMEM_LIMIT),
        name="combine",
    )(bidx_d, plan["pc"], plan["lo"], plan["dst"], plan["tsum"], info, ys, x1, mod2, lng, lnb)


def _moe(meta, h2, mask_t, gate_t, cnt, layer, w1, w3, w2, x1, mod2, lng, lnb, out_rows):
    n = h2.shape[0]
    ntd = n // TD
    rtot = -(-(ntd * RL) // TMX) * TMX + NE * TMX
    cnt = cnt[:, :, :TM // TD].transpose(0, 2, 1).reshape(ntd, NE).astype(jnp.int32)
    plan = _plan(cnt, rtot // TMX)
    xs, info = _dispatch(plan, mask_t, gate_t, h2, rtot)
    ys = _experts(plan, xs.reshape(rtot, D), layer, w1, w3, w2).reshape(rtot // ALIGN, ALIGN, D)
    return _combine(meta["bidx"], plan, info, ys, x1, mod2, lng, lnb, out_rows)


def _tile_meta(groups, rows):
    bidx, pblk, first, last, cf, cb = [], [], [], [], [], []
    row = 0
    tile = 0
    for (b, s) in groups:
        assert s % rows == 0
        per = s // rows
        for bi in range(b):
            for j in range(per):
                bidx.append(row + bi)
                pblk.append(j)
                first.append(1 if j == 0 else 0)
                last.append(1 if j == per - 1 else 0)
                cf.append(tile + j)
                cb.append(tile + per - 1 - j)
            tile += per
        row += b
    as_i32 = lambda v: jnp.asarray(np.asarray(v, np.int32))
    return dict(bidx=as_i32(bidx), pblk=as_i32(pblk), first=as_i32(first), last=as_i32(last),
                cf=as_i32(cf), cb=as_i32(cb))


def _rope_tables(s_max):
    inv = jnp.power(ROPE_THETA, -jnp.arange(0, HD, 2, dtype=F32) / HD)
    ang = jnp.arange(s_max, dtype=F32)[:, None] * inv[None, :]
    cos, sin = jnp.cos(ang), jnp.sin(ang)
    cos128 = jnp.tile(cos, (1, 4))
    sin128 = jnp.tile(jnp.concatenate([-sin, sin], axis=1), (1, 2))
    return cos128, sin128


def _block_diag(w):
    d, nb, c, f = w.shape
    eye = jnp.eye(nb, dtype=w.dtype)
    return jnp.einsum('dncf,nm->dncmf', w, eye).reshape(d, nb * c, nb * f)


def _forward(xs, cs, w_mod, b_mod, w_in, attn_sink, conv_w, conv_b, lru_w_r, lru_b_r, lru_w_i, lru_b_i,
             lru_lambda, sg_norm_g, sg_w, sg_b, mix_norm_g, w_out, ln_g, ln_b, router_w, router_bias,
             exp_w1, exp_w3, exp_w2):
    groups = [(x.shape[0], x.shape[1]) for x in xs]
    assert len(groups) == 2
    meta_p, meta_a, meta_r = (_tile_meta(groups, rows) for rows in (TP, TA, TR))
    meta_m, meta_d = _tile_meta(groups, TM), _tile_meta(groups, TD)
    x_parts = [xx.reshape(-1, D) for xx in xs]
    group_rows = [p.shape[0] for p in x_parts]
    c_all = jnp.concatenate(cs, axis=0)
    bt = c_all.shape[0]
    mods = _modulation(c_all, w_mod, b_mod).reshape(2 * DEPTH, bt, 1, 3 * D)
    cos_t, sin_t = _rope_tables(max(s for _, s in groups))
    rwt = router_w.T[_slot_order()].astype(BF16)
    rb = router_bias[_slot_order()].reshape(NE, 1)
    for l in range(DEPTH):
        q, kk, vv, gy, xr, u, vn = _pre_mixer(meta_p, x_parts, mods[2 * l], w_in[l].astype(BF16), cos_t, sin_t,
                                              sg_norm_g[l].reshape(1, SGW))
        oa = _attention(meta_a, attn_sink[l], q, kk, vv)
        hf, hb = _recurrent(meta_r, xr, conv_w[l], conv_b[l].reshape(1, RW),
                            _block_diag(0.5 * lru_w_r[l]).astype(BF16), _block_diag(0.5 * lru_w_i[l]).astype(BF16),
                            0.5 * lru_b_r[l], 0.5 * lru_b_i[l], lru_lambda[l])
        sgb = jnp.repeat(sg_b[l].T, HD, axis=1)
        x1, h2, gate_t, mask_t, cnt = _merge(meta_m, x_parts, oa, gy, hf, hb, u, vn, mods[2 * l], mods[2 * l + 1],
                                             mix_norm_g[l].reshape(1, D), sg_w[l].astype(BF16), sgb,
                                             w_out[l].astype(BF16), ln_g[l, 0].reshape(1, D),
                                             ln_b[l, 0].reshape(1, D), rwt, rb)
        out_rows = group_rows if l == DEPTH - 1 else [sum(group_rows)]
        x_parts = _moe(meta_d, h2, mask_t, gate_t, cnt, l, exp_w1, exp_w3, exp_w2, x1, mods[2 * l + 1],
                       ln_g[l, 1].reshape(1, D), ln_b[l, 1].reshape(1, D), out_rows)
    return tuple(p.reshape(b, s, D) for p, (b, s) in zip(x_parts, groups))


def kernel(x_prompt, x_sample, c_prompt, c_sample, w_mod, b_mod, w_in, attn_sink, conv_w, conv_b, lru_w_r, lru_b_r,
           lru_w_i, lru_b_i, lru_lambda, sg_norm_g, sg_w, sg_b, mix_norm_g, w_out, ln_g, ln_b, router_w,
           router_bias, exp_w1, exp_w3, exp_w2):
    return _forward([x_prompt, x_sample], [c_prompt, c_sample], w_mod, b_mod, w_in, attn_sink, conv_w, conv_b,
                    lru_w_r, lru_b_r, lru_w_i, lru_b_i, lru_lambda, sg_norm_g, sg_w, sg_b, mix_norm_g, w_out,
                    ln_g, ln_b, router_w, router_bias, exp_w1, exp_w3, exp_w2)
```

```python
import functools

import numpy as np
import jax
import jax.numpy as jnp
from jax import lax
from jax.experimental import pallas as pl
from jax.experimental.pallas import tpu as pltpu

F32 = jnp.float32
BF16 = jnp.bfloat16

D = 1024
DEPTH = 2
HD = 64
AW = 384
KVW = 128
RW = 384
SGW = 256
WINDOW = 128
CONV_WIDTH = 4
LRU_C = 8.0
CHUNK = 128
NE = 16
NG = 4
EPG = NE // NG
FF = 512
ALPHA = (2 * DEPTH) ** 0.25
LN_EPS = 1e-5
RMS_EPS = 1e-6
ROPE_THETA = 10000.0

TM = 1024
TP = 1024
TA = 2048
TR = 2048
QB = 128
SEG = TR // 8
NLG = RW // 128
TD = 512
ALIGN = 16
RL = 2 * TD + NE * ALIGN
TMX = 1024
ZROWS = 256
NEG = -1e30
VMEM_LIMIT = 48 * 1024 * 1024

C_Q, C_K, C_V, C_Y, C_R, C_U, C_SV, C_END = 0, 384, 512, 640, 1024, 1408, 1664, 1920


def _ln(x):
    mu = jnp.mean(x, axis=-1, keepdims=True)
    xc = x - mu
    var = jnp.mean(xc * xc, axis=-1, keepdims=True)
    return xc * lax.rsqrt(var + LN_EPS)


def _rms(x):
    return x * lax.rsqrt(jnp.mean(x * x, axis=-1, keepdims=True) + RMS_EPS)


def _sigmoid(z):
    return 0.5 * (jnp.tanh(0.5 * z) + 1.0)


def _gelu(x):
    return 0.5 * x * (1.0 + jnp.tanh(0.7978845608028654 * (x + 0.044715 * (x * x * x))))


def _split_bf16(a):
    hi = a.astype(BF16)
    lo = (a - hi.astype(F32)).astype(BF16)
    return hi, lo


def _mod_kernel(c_ref, w_ref, b_ref, o_ref):
    c = c_ref[...]
    ca = c * _sigmoid(c)
    ch, cl = _split_bf16(ca)
    wh, wl = _split_bf16(w_ref[...])
    acc = jnp.dot(ch, wh, preferred_element_type=F32)
    acc += jnp.dot(ch, wl, preferred_element_type=F32)
    acc += jnp.dot(cl, wh, preferred_element_type=F32)
    o_ref[...] = acc + b_ref[...]


def _modulation(c_all, w_mod, b_mod):
    bt = c_all.shape[0]
    cb = 768
    w = w_mod.reshape(2 * DEPTH, D, 3 * D)
    b = b_mod.reshape(2 * DEPTH, 1, 3 * D)
    return pl.pallas_call(
        _mod_kernel,
        grid=(2 * DEPTH, 3 * D // cb),
        in_specs=[pl.BlockSpec((bt, D), lambda s, j: (0, 0)),
                  pl.BlockSpec((None, D, cb), lambda s, j: (s, 0, j)),
                  pl.BlockSpec((None, 1, cb), lambda s, j: (s, 0, j))],
        out_specs=pl.BlockSpec((None, bt, cb), lambda s, j: (s, 0, j)),
        out_shape=jax.ShapeDtypeStruct((2 * DEPTH, bt, 3 * D), F32),
        name="modulation",
    )(c_all, w, b)


def _load_tokens(i, x_refs, split):
    if len(x_refs) == 1:
        return x_refs[0][...]
    return jnp.where(i < split, x_refs[0][...], x_refs[1][...])


def _token_specs(x_parts, rows):
    if len(x_parts) == 1:
        return [pl.BlockSpec((rows, D), lambda i, *_: (i, 0))], 0
    split = x_parts[0].shape[0] // rows
    return [pl.BlockSpec((rows, D), lambda i, *_: (jnp.minimum(i, split - 1), 0)),
            pl.BlockSpec((rows, D), lambda i, *_: (jnp.maximum(i - split, 0), 0))], split


def _pre_kernel(split, bidx_ref, pblk_ref, *refs):
    del bidx_ref, pblk_ref
    nx = len(refs) - 12
    x_refs = refs[:nx]
    mod_ref, w_ref, cos_ref, sin_ref, sgg_ref, q_ref, kk_ref, vv_ref, gy_ref, xr_ref, u_ref, vn_ref = refs[nx:]
    mod = mod_ref[...]
    shift, scale = mod[:, :D], mod[:, D:2 * D]
    h = (_ln(_load_tokens(pl.program_id(0), x_refs, split)) * (1.0 + scale) + shift).astype(BF16)
    cos = cos_ref[...]
    sin = sin_ref[...]
    lane = lax.broadcasted_iota(jnp.int32, (1, 128), 1)
    first_half = (lane % HD) < (HD // 2)

    def rope128(xg):
        rot = jnp.where(first_half, pltpu.roll(xg, 128 - HD // 2, 1), pltpu.roll(xg, HD // 2, 1))
        return xg * cos + rot * sin

    q = jnp.dot(h, w_ref[:, C_Q:C_K], preferred_element_type=F32)
    for g in range(AW // 128):
        q_ref[:, g * 128:(g + 1) * 128] = (rope128(q[:, g * 128:(g + 1) * 128]) * (HD ** -0.5)).astype(BF16)
    kv = jnp.dot(h, w_ref[:, C_K:C_Y], preferred_element_type=F32)
    lo_half = lane < HD
    for src, dst_ref in ((rope128(kv[:, :KVW]), kk_ref), (kv[:, KVW:], vv_ref)):
        swapped = pltpu.roll(src, HD, 1)
        dst_ref[:, 0:128] = jnp.where(lo_half, src, swapped).astype(BF16)
        dst_ref[:, 128:256] = jnp.where(lo_half, swapped, src).astype(BF16)
    gy_ref[...] = _gelu(jnp.dot(h, w_ref[:, C_Y:C_R], preferred_element_type=F32)).astype(gy_ref.dtype)
    xr = jnp.dot(h, w_ref[:, C_R:C_U], preferred_element_type=F32)
    for g in range(NLG):
        xr_ref[g] = xr[:, g * 128:(g + 1) * 128]
    u_ref[...] = _gelu(jnp.dot(h, w_ref[:, C_U:C_SV], preferred_element_type=F32)).astype(u_ref.dtype)
    sv = _gelu(jnp.dot(h, w_ref[:, C_SV:C_END], preferred_element_type=F32))
    vn_ref[...] = (_ln(sv) * sgg_ref[...]).astype(BF16)


def _pre_mixer(meta, x_parts, mod_l, w_ext, cos_t, sin_t, sgg):
    n = sum(p.shape[0] for p in x_parts)
    nt = n // TP
    tok = lambda w: pl.BlockSpec((TP, w), lambda i, b, p: (i, 0))
    full = lambda a: pl.BlockSpec(a.shape, lambda i, b, p: (0,) * a.ndim)
    x_specs, split = _token_specs(x_parts, TP)
    grid_spec = pltpu.PrefetchScalarGridSpec(
        num_scalar_prefetch=2, grid=(nt,),
        in_specs=x_specs + [
            pl.BlockSpec((None, 1, 3 * D), lambda i, b, p: (b[i], 0, 0)),
            full(w_ext),
            pl.BlockSpec((TP, 128), lambda i, b, p: (p[i], 0)),
            pl.BlockSpec((TP, 128), lambda i, b, p: (p[i], 0)),
            full(sgg)],
        out_specs=[tok(AW), tok(256), tok(256), tok(RW),
                   pl.BlockSpec((NLG, TP, 128), lambda i, b, p: (0, i, 0)), tok(SGW), tok(SGW)])
    sds = lambda w, dt: jax.ShapeDtypeStruct((n, w), dt)
    return pl.pallas_call(
        functools.partial(_pre_kernel, split), grid_spec=grid_spec,
        out_shape=[sds(AW, BF16), sds(256, BF16), sds(256, BF16), sds(RW, BF16),
                   jax.ShapeDtypeStruct((NLG, n, 128), F32), sds(SGW, BF16), sds(SGW, BF16)],
        compiler_params=pltpu.CompilerParams(dimension_semantics=("parallel",), vmem_limit_bytes=VMEM_LIMIT),
        name="pre_mixer",
    )(meta["bidx"], meta["pblk"], *x_parts, mod_l, w_ext, cos_t, sin_t, sgg)


def _attn_kernel(first_ref, last_ref, sink_ref, q_ref, k_ref, kp_ref, kn_ref, v_ref, vp_ref, vn_ref,
                 o_ref, kw_s, vw_s):
    i = pl.program_id(0)
    is_first = first_ref[i] == 1
    is_last = last_ref[i] == 1
    kw_s[0:QB] = kp_ref[...]
    kw_s[QB:QB + TA] = k_ref[...]
    kw_s[QB + TA:] = kn_ref[...]
    vw_s[0:QB] = vp_ref[...]
    vw_s[QB:QB + TA] = v_ref[...]
    vw_s[QB + TA:] = vn_ref[...]
    iq = lax.broadcasted_iota(jnp.int32, (QB, 3 * QB), 0)
    ik = lax.broadcasted_iota(jnp.int32, (QB, 3 * QB), 1)
    rel = ik - iq
    band = (rel >= 0) & (rel <= 2 * WINDOW)
    lane = lax.broadcasted_iota(jnp.int32, (1, 128), 1)
    lo_half = lane < HD
    zero = jnp.zeros((), BF16)
    gsz = AW // HD // 2
    row_head = lax.broadcasted_iota(jnp.int32, (gsz * QB, 1), 0) // QB
    for j in range(TA // QB):
        ok = band
        if j == 0:
            ok = ok & ((ik >= QB) | jnp.logical_not(is_first))
        if j == TA // QB - 1:
            ok = ok & ((ik < 2 * QB) | jnp.logical_not(is_last))
        bias = jnp.where(ok, 0.0, NEG)
        bias = jnp.concatenate([bias] * gsz, axis=0)
        rows = slice(j * QB, (j + 1) * QB)
        keys = slice(j * QB, j * QB + 3 * QB)
        outs = [None] * (AW // HD)
        for g in range(2):
            heads = range(g * gsz, (g + 1) * gsz)
            qs = []
            for hh in heads:
                qg = q_ref[rows, (hh // 2) * 128:(hh // 2 + 1) * 128]
                qs.append(jnp.where(lo_half if hh % 2 == 0 else jnp.logical_not(lo_half), qg, zero))
            s = lax.dot_general(jnp.concatenate(qs, axis=0), kw_s[keys, g * 128:(g + 1) * 128],
                                (((1,), (1,)), ((), ())), preferred_element_type=F32) + bias
            sink = jnp.full((gsz * QB, 1), sink_ref[heads[-1]], F32)
            for t in range(gsz - 1):
                sink = jnp.where(row_head == t, sink_ref[heads[t]], sink)
            m = jnp.maximum(jnp.max(s, axis=-1, keepdims=True), sink)
            p = jnp.exp(s - m)
            denom = jnp.sum(p, axis=-1, keepdims=True) + jnp.exp(sink - m)
            pv = jnp.dot(p.astype(BF16), vw_s[keys, g * 128:(g + 1) * 128], preferred_element_type=F32)
            o = pv * (1.0 / denom)
            for t, hh in enumerate(heads):
                outs[hh] = o[t * QB:(t + 1) * QB]
        for jg in range(AW // 128):
            o_ref[rows, jg * 128:(jg + 1) * 128] = jnp.where(lo_half, outs[2 * jg], outs[2 * jg + 1]).astype(o_ref.dtype)


def _attention(meta, sink, q, kk, vv):
    n = q.shape[0]
    nt = n // TA
    nqb = n // QB
    r = TA // QB
    main = lambda w: pl.BlockSpec((TA, w), lambda i, f, l, s: (i, 0))
    prev = pl.BlockSpec((QB, 256), lambda i, f, l, s: (jnp.maximum(i * r - 1, 0), 0))
    nxt = pl.BlockSpec((QB, 256), lambda i, f, l, s: (jnp.minimum(i * r + r, nqb - 1), 0))
    grid_spec = pltpu.PrefetchScalarGridSpec(
        num_scalar_prefetch=3, grid=(nt,),
        in_specs=[main(AW), main(256), prev, nxt, main(256), prev, nxt],
        out_specs=main(AW),
        scratch_shapes=[pltpu.VMEM((TA + 2 * QB, 256), BF16), pltpu.VMEM((TA + 2 * QB, 256), BF16)])
    return pl.pallas_call(
        _attn_kernel, grid_spec=grid_spec,
        out_shape=jax.ShapeDtypeStruct((n, AW), BF16),
        compiler_params=pltpu.CompilerParams(dimension_semantics=("parallel",), vmem_limit_bytes=VMEM_LIMIT),
        name="attention",
    )(meta["first"], meta["last"], sink, q, kk, kk, kk, vv, vv, vv)


def _rec_kernel(cf_ref, cb_ref, first_ref, last_ref,
                x_hbm, xfp_ref, xfn_ref, xbp_ref, xbn_ref,
                cw_ref, cbias_ref, wr_ref, wi_ref, br_ref, bi_ref, lam_ref,
                hf_hbm, hb_hbm,
                ext_s, a_s, b_s, hs_s, ps_s, out_s, carry_s, gsem, ssem):
    i = pl.program_id(0)
    nsteps = pl.num_programs(0)
    slot = i % 2
    cw = cw_ref[...]
    cbias = cbias_ref[...]
    sub = lax.broadcasted_iota(jnp.int32, (8, 128), 0)
    unroll = 8
    chunk_refs = (cf_ref, cb_ref)
    out_hbm = (hf_hbm, hb_hbm)

    def gather(step, buf, d, start):
        c = chunk_refs[d][step]
        for g in range(NLG):
            for s in range(8):
                cp = pltpu.make_async_copy(x_hbm.at[g, pl.ds(pl.multiple_of(c * TR + s * SEG, SEG), SEG), :],
                                           ext_s.at[buf, d, g, pl.ds(2, SEG), s, :], gsem.at[buf, d])
                if start:
                    cp.start()
                else:
                    cp.wait()

    def scatter(d, c, start):
        for g in range(NLG):
            for s in range(8):
                cp = pltpu.make_async_copy(out_s.at[d, g, :, s, :],
                                           out_hbm[d].at[g, pl.ds(pl.multiple_of(c * TR + s * SEG, SEG), SEG), :],
                                           ssem.at[d])
                if start:
                    cp.start()
                else:
                    cp.wait()

    @pl.when(i == 0)
    def _():
        gather(0, 0, 0, True)
        gather(0, 0, 1, True)

    @pl.when(i + 1 < nsteps)
    def _():
        gather(i + 1, 1 - slot, 0, True)
        gather(i + 1, 1 - slot, 1, True)

    def run(d, xp_ref, xn_ref):
        c = chunk_refs[d][i]
        first = first_ref[c] == 1
        last = last_ref[c] == 1
        gather(i, slot, d, False)
        ext = ext_s.at[slot, d]
        planes = []
        for g in range(NLG):
            prev = jnp.where(first, 0.0, xp_ref[g])
            nxt = jnp.where(last, 0.0, xn_ref[g])
            ext[g, 0] = jnp.where(sub == 0, prev[6:7], pltpu.roll(ext[g, SEG], 1, 0))
            ext[g, 1] = jnp.where(sub == 0, prev[7:8], pltpu.roll(ext[g, SEG + 1], 1, 0))
            ext[g, SEG + 2] = jnp.where(sub == 7, nxt[0:1], pltpu.roll(ext[g, 2], 7, 0))
            lanes = slice(g * 128, (g + 1) * 128)
            acc = ext[g, 0:SEG] * cw[0:1, lanes]
            for t in range(1, CONV_WIDTH):
                acc = acc + ext[g, t:t + SEG] * cw[t:t + 1, lanes]
            planes.append(acc.reshape(TR, 128))
        xc = jnp.concatenate(planes, axis=1) + cbias
        xb16 = xc.astype(BF16)
        tr = jnp.tanh(jnp.dot(xb16, wr_ref[d], preferred_element_type=F32) + br_ref[d:d + 1])
        ti = jnp.tanh(jnp.dot(xb16, wi_ref[d], preferred_element_type=F32) + bi_ref[d:d + 1])
        nlam = -lam_ref[d:d + 1]
        softplus = jnp.maximum(nlam, 0.0) + jnp.log1p(jnp.exp(-jnp.abs(nlam)))
        half_c = (-0.5 * LRU_C) * softplus
        log_a = half_c * tr + half_c
        a = jnp.exp(log_a)
        v = (1.0 + a * a) * jnp.tanh(-log_a)
        b = jnp.where(v > 0.0, v * lax.rsqrt(v), 0.0) * (ti + 1.0) * (0.5 * xc)
        for g in range(NLG):
            a_s[g] = a[:, g * 128:(g + 1) * 128].reshape(SEG, 8, 128)
            b_s[g] = b[:, g * 128:(g + 1) * 128].reshape(SEG, 8, 128)

        reverse = d == 1
        reset = last if reverse else first

        @pl.when(reset)
        def _():
            carry_s[d] = jnp.zeros((NLG, 128), F32)

        def step(jo, hp):
            hp = list(hp)
            for ji in range(unroll):
                jj = jo * unroll + ji
                j = SEG - 1 - jj if reverse else jj
                for g in range(NLG):
                    h, p = hp[g]
                    ag = a_s[g, j]
                    h = ag * h + b_s[g, j]
                    p = p * ag
                    hs_s[g, j] = h
                    ps_s[g, j] = p
                    hp[g] = (h, p)
            return tuple(hp)

        init = tuple((jnp.zeros((8, 128), F32), jnp.ones((8, 128), F32)) for _ in range(NLG))
        ends = lax.fori_loop(0, SEG // unroll, step, init)
        cmats = []
        for g in range(NLG):
            e, pe = ends[g]
            c_in = carry_s[d, g:g + 1]
            rows = [None] * 8
            for s in (range(7, -1, -1) if reverse else range(8)):
                rows[s] = c_in
                c_in = e[s:s + 1] + pe[s:s + 1] * c_in
            carry_s[d, g:g + 1] = c_in
            cmats.append(jnp.concatenate(rows, axis=0))

        @pl.when(i > 0)
        def _():
            scatter(d, 0, False)

        def fix(jo, carry):
            for ji in range(unroll):
                j = jo * unroll + ji
                for g in range(NLG):
                    out_s[d, g, j] = hs_s[g, j] + ps_s[g, j] * cmats[g]
            return carry

        lax.fori_loop(0, SEG // unroll, fix, 0)
        scatter(d, c, True)

    run(0, xfp_ref, xfn_ref)
    run(1, xbp_ref, xbn_ref)

    @pl.when(i == nsteps - 1)
    def _():
        scatter(0, 0, False)
        scatter(1, 0, False)


def _recurrent(meta, xr, cw, cbias, wr, wi, br, bi, lam):
    n = xr.shape[1]
    nt = n // TR
    n8 = n // 8
    r8 = TR // 8

    def halos(which):
        sel = (lambda cf, cb: cf) if which == 0 else (lambda cf, cb: cb)
        prev = pl.BlockSpec((NLG, 8, 128),
                            lambda i, cf, cb, f, l: (0, jnp.maximum(sel(cf, cb)[i] * r8 - 1, 0), 0))
        nxt = pl.BlockSpec((NLG, 8, 128),
                           lambda i, cf, cb, f, l: (0, jnp.minimum(sel(cf, cb)[i] * r8 + r8, n8 - 1), 0))
        return prev, nxt

    full = lambda a: pl.BlockSpec(a.shape, lambda i, cf, cb, f, l: (0,) * a.ndim)
    anywhere = pl.BlockSpec(memory_space=pl.ANY)
    plane = lambda lead: pltpu.VMEM(lead + (NLG, SEG, 8, 128), F32)
    grid_spec = pltpu.PrefetchScalarGridSpec(
        num_scalar_prefetch=4, grid=(nt,),
        in_specs=[anywhere, *halos(0), *halos(1),
                  full(cw), full(cbias), full(wr), full(wi), full(br), full(bi), full(lam)],
        out_specs=[anywhere, anywhere],
        scratch_shapes=[pltpu.VMEM((2, 2, NLG, SEG + 3, 8, 128), F32),
                        plane(()), plane(()), plane(()), plane(()), plane((2,)),
                        pltpu.VMEM((2, NLG, 128), F32),
                        pltpu.SemaphoreType.DMA((2, 2)), pltpu.SemaphoreType.DMA((2,))])
    return pl.pallas_call(
        _rec_kernel, grid_spec=grid_spec,
        out_shape=[jax.ShapeDtypeStruct((NLG, n, 128), F32), jax.ShapeDtypeStruct((NLG, n, 128), F32)],
        compiler_params=pltpu.CompilerParams(dimension_semantics=("arbitrary",), vmem_limit_bytes=VMEM_LIMIT),
        name="recurrent",
    )(meta["cf"], meta["cb"], meta["first"], meta["last"], xr, xr, xr, xr, xr, cw, cbias, wr, wi, br, bi, lam)


def _route(sel, score):
    def before(vk, vj, k, j):
        return (vk > vj) | ((vk == vj) & (k < j)) if k < j else (vk > vj)

    def count(flags):
        return sum(jnp.where(f, 1.0, 0.0) for f in flags)

    slot = lambda a, j: a[j * NG:(j + 1) * NG]
    cand = [slot(sel, j) for j in range(EPG)]
    in_top = [jnp.where(count(before(cand[k], cand[j], k, j) for k in range(EPG) if k != j) < 2.0, 1.0, 0.0)
              for j in range(EPG)]
    gscore = sum(in_top[j] * cand[j] for j in range(EPG))
    grow = [gscore[g:g + 1] for g in range(NG)]
    best = jnp.concatenate(
        [jnp.where(count(before(grow[k], grow[g], k, g) for k in range(NG) if k != g) < 1.0, 1.0, 0.0)
         for g in range(NG)], axis=0)
    picked = [in_top[j] * best * slot(score, j) for j in range(EPG)]
    inv = 1.0 / sum(jnp.sum(p, axis=0, keepdims=True) for p in picked)
    mask = jnp.concatenate([in_top[j] * best for j in range(EPG)], axis=0)
    gate = jnp.concatenate([p * inv for p in picked], axis=0)
    return mask, gate


def _slot_order():
    return np.asarray([g * EPG + j for j in range(EPG) for g in range(NG)], np.int32)


def _merge_kernel(split, bidx_ref, *refs):
    del bidx_ref
    nx = len(refs) - 22
    x_refs = refs[:nx]
    (oa_ref, gy_ref, hf_ref, hb_ref, u_ref, vn_ref, mod1_ref, mod2_ref, gmix_ref, sgw_ref, sgb_ref, wout_ref,
     lng_ref, lnb_ref, rwt_ref, rb_ref, x1_ref, h2_ref, gate_ref, mask_ref, cnt_ref, mrg_s) = refs[nx:]
    gmix = gmix_ref[...]
    mrg_s[:, 0:AW] = (_rms(oa_ref[...].astype(F32)) * gmix[:, 0:AW]).astype(BF16)
    hsum = jnp.concatenate([hf_ref[g] + hb_ref[g] for g in range(NLG)], axis=1)
    o_rec = gy_ref[...].astype(F32) * hsum
    mrg_s[:, AW:AW + RW] = (_rms(o_rec) * gmix[:, AW:AW + RW]).astype(BF16)
    lane = lax.broadcasted_iota(jnp.int32, (1, 128), 1)
    lo_half = lane < HD
    zero = jnp.zeros((), BF16)
    pieces = []
    for c in range(TM // CHUNK):
        rows = slice(c * CHUNK, (c + 1) * CHUNK)
        grp = []
        for g in range(SGW // 128):
            vg = vn_ref[rows, g * 128:(g + 1) * 128]
            mixed = jnp.dot(sgw_ref[2 * g], jnp.where(lo_half, vg, zero), preferred_element_type=F32)
            mixed += jnp.dot(sgw_ref[2 * g + 1], jnp.where(lo_half, zero, vg), preferred_element_type=F32)
            grp.append(mixed)
        mixed = jnp.concatenate(grp, axis=1) + sgb_ref[...]
        pieces.append(u_ref[rows, :].astype(F32) * mixed)
    o_sg = jnp.concatenate(pieces, axis=0)
    mrg_s[:, AW + RW:] = (_rms(o_sg) * gmix[:, AW + RW:]).astype(BF16)
    o = jnp.dot(mrg_s[...], wout_ref[...], preferred_element_type=F32)
    gate1 = mod1_ref[...][:, 2 * D:]
    x_in = _load_tokens(pl.program_id(0), x_refs, split)
    x1 = _ln(ALPHA * x_in + gate1 * o) * lng_ref[...] + lnb_ref[...]
    x1_ref[...] = x1
    mod2 = mod2_ref[...]
    h2 = (_ln(x1) * (1.0 + mod2[:, D:2 * D]) + mod2[:, :D]).astype(BF16)
    h2_ref[...] = h2
    logits = lax.dot_general(rwt_ref[...], h2, (((1,), (1,)), ((), ())), preferred_element_type=F32)
    score = _sigmoid(logits)
    sel = score + rb_ref[...]
    mask_t, gate_t = _route(sel, score)
    gate_ref[...] = gate_t
    mask_ref[...] = mask_t
    t_i = lax.broadcasted_iota(jnp.int32, (TM, 128), 0)
    j_i = lax.broadcasted_iota(jnp.int32, (TM, 128), 1)
    in_tile = jnp.where((t_i >= j_i * TD) & (t_i < (j_i + 1) * TD), 1.0, 0.0).astype(BF16)
    cnt_ref[...] = jnp.dot(mask_t.astype(BF16), in_tile, preferred_element_type=F32)


def _merge(meta, x_parts, oa, gy, hf, hb, u, vn, mod1, mod2, gmix, sgw, sgb, wout, lng, lnb, rwt, rb):
    n = oa.shape[0]
    nt = n // TM
    x_specs, split = _token_specs(x_parts, TM)
    tok = lambda w: pl.BlockSpec((TM, w), lambda i, b: (i, 0))
    full = lambda a: pl.BlockSpec(a.shape, lambda i, b: (0,) * a.ndim)
    modspec = pl.BlockSpec((None, 1, 3 * D), lambda i, b: (b[i], 0, 0))
    tspec = pl.BlockSpec((NE, TM), lambda i, b: (0, i))
    rec = pl.BlockSpec((NLG, TM, 128), lambda i, b: (0, i, 0))
    grid_spec = pltpu.PrefetchScalarGridSpec(
        num_scalar_prefetch=1, grid=(nt,),
        in_specs=x_specs + [
            tok(AW), tok(RW), rec, rec, tok(SGW), tok(SGW), modspec, modspec,
            full(gmix), full(sgw), full(sgb), full(wout), full(lng), full(lnb), full(rwt), full(rb)],
        out_specs=[tok(D), tok(D), tspec, tspec, pl.BlockSpec((None, NE, 128), lambda i, b: (i, 0, 0))],
        scratch_shapes=[pltpu.VMEM((TM, D), BF16)])
    return pl.pallas_call(
        functools.partial(_merge_kernel, split), grid_spec=grid_spec,
        out_shape=[jax.ShapeDtypeStruct((n, D), F32), jax.ShapeDtypeStruct((n, D), BF16),
                   jax.ShapeDtypeStruct((NE, n), F32), jax.ShapeDtypeStruct((NE, n), F32),
                   jax.ShapeDtypeStruct((nt, NE, 128), F32)],
        compiler_params=pltpu.CompilerParams(dimension_semantics=("parallel",), vmem_limit_bytes=VMEM_LIMIT),
        name="merge_route",
    )(meta["bidx"], *x_parts, oa, gy, hf, hb, u, vn, mod1, mod2, gmix, sgw, sgb, wout, lng, lnb, rwt, rb)


def _plan(cnt, n_mt):
    pc = (cnt + (ALIGN - 1)) // ALIGN * ALIGN
    lo = jnp.cumsum(pc, axis=1) - pc
    tot = jnp.sum(pc, axis=0)
    seg = (tot + (TMX - 1)) // TMX * TMX
    gend = jnp.cumsum(seg)
    gstart = gend - seg
    dst = gstart[None, :] + jnp.cumsum(pc, axis=0) - pc
    tile_row = jnp.arange(n_mt, dtype=jnp.int32) * TMX
    texp = jnp.minimum(jnp.sum((gend[None, :] <= tile_row[:, None]).astype(jnp.int32), axis=1), NE - 1)
    texp = jnp.asarray(_slot_order())[texp]
    nact = (gend[-1] // TMX).reshape(1)
    grp = lambda a: (a // ALIGN).astype(jnp.int32)
    return dict(pc=grp(pc).reshape(-1), lo=grp(lo).reshape(-1), dst=grp(dst).reshape(-1),
                tsum=grp(jnp.sum(pc, axis=1)),
                zdst=grp(jnp.concatenate([gstart + tot, gend[-1:]])), zlen=grp(seg - tot),
                texp=texp.astype(jnp.int32), nact=nact.astype(jnp.int32))


def _run_copies(src, dst, src_off, dst_off, groups, sem, max_bits):
    def arm(b):
        @pl.when(((groups >> b) & 1) == 1)
        def _():
            off = (groups >> (b + 1)) << (b + 1)
            pltpu.make_async_copy(src.at[pl.ds(src_off + off, 1 << b)], dst.at[pl.ds(dst_off + off, 1 << b)],
                                  sem).start()

    for b in range(min(COMMON_BITS, max_bits)):
        arm(b)
    if max_bits > COMMON_BITS:
        @pl.when(groups >= (1 << COMMON_BITS))
        def _():
            for b in range(COMMON_BITS, max_bits):
                arm(b)


def _wait_groups(buf, groups, sem, max_bits):
    for b in range(max_bits):
        @pl.when(((groups >> b) & 1) == 1)
        def _():
            pltpu.make_async_copy(buf.at[pl.ds(0, 1 << b)], buf.at[pl.ds(0, 1 << b)], sem).wait()


COMMON_BITS = 3
RUN_BITS = (TD // ALIGN).bit_length()
TILE_BITS = (RL // ALIGN).bit_length()


def _dispatch_kernel(pc_ref, lo_ref, dst_ref, ts_ref, zd_ref, zl_ref, mask_ref, gate_ref, h_ref,
                     xs_ref, info_ref, loc_s, zero_s, earlier_s, sems):
    i = pl.program_id(0)
    last_step = pl.num_programs(0) - 1
    slot = i % 2
    mask = mask_ref[...]
    mb = mask.astype(BF16)
    @pl.when(i == 0)
    def _():
        s_i = lax.broadcasted_iota(jnp.int32, (TD, TD), 0)
        t_i = lax.broadcasted_iota(jnp.int32, (TD, TD), 1)
        earlier_s[...] = jnp.where(s_i < t_i, 1.0, 0.0).astype(BF16)

    rank = jnp.dot(mb, earlier_s[...], preferred_element_type=F32)
    e_i = lax.broadcasted_iota(jnp.int32, (NE, NE), 0)
    f_i = lax.broadcasted_iota(jnp.int32, (NE, NE), 1)
    below = jnp.where(f_i < e_i, 1.0, 0.0).astype(BF16)
    lower = jnp.dot(below, mb, preferred_element_type=F32)
    row_e = lax.broadcasted_iota(jnp.int32, (NE, 1), 0)
    lo_vec = jnp.zeros((NE, 1), F32)
    for e in range(NE):
        lo_vec = jnp.where(row_e == e, (lo_ref[i * NE + e] * ALIGN).astype(F32), lo_vec)
    row = lo_vec + rank
    is0 = mask * jnp.where(lower == 0.0, 1.0, 0.0)
    is1 = mask - is0
    d0 = jnp.sum(is0 * row, axis=0, keepdims=True)
    d1 = jnp.sum(is1 * row, axis=0, keepdims=True)
    gate = gate_ref[...]
    w0 = jnp.sum(is0 * gate, axis=0, keepdims=True)
    w1 = jnp.sum(is1 * gate, axis=0, keepdims=True)
    info_ref[...] = jnp.concatenate([d0, d1, w0, w1, jnp.zeros((4, TD), F32)], axis=0)
    r_i = lax.broadcasted_iota(jnp.int32, (RL, TD), 0)
    perm = jnp.where((r_i == d0.astype(jnp.int32)) | (r_i == d1.astype(jnp.int32)), 1.0, 0.0).astype(BF16)
    loc = loc_s.at[slot]
    loc[...] = jnp.dot(perm, h_ref[...], preferred_element_type=F32).astype(BF16).reshape(RL // ALIGN, ALIGN, D)
    for e in range(NE):
        _run_copies(loc, xs_ref, lo_ref[i * NE + e], dst_ref[i * NE + e], pc_ref[i * NE + e], sems.at[slot],
                    RUN_BITS)

    @pl.when(i > 0)
    def _():
        _wait_groups(loc_s.at[1 - slot], ts_ref[jnp.maximum(i - 1, 0)], sems.at[1 - slot], TILE_BITS)

    @pl.when(i == last_step)
    def _():
        _wait_groups(loc, ts_ref[i], sems.at[slot], TILE_BITS)
        sem = sems.at[0]
        zero_s[...] = jnp.zeros_like(zero_s)
        zg = ZROWS // ALIGN
        for wait in (False, True):
            for e in range(NE):
                for part in range(TMX // ZROWS):
                    groups = jnp.clip(zl_ref[e] - part * zg, 0, zg)
                    if wait:
                        _wait_groups(zero_s, groups, sem, zg.bit_length())
                    else:
                        _run_copies(zero_s, xs_ref, 0, zd_ref[e] + part * zg, groups, sem, zg.bit_length())
        tail = zd_ref[NE]
        chunks = (xs_ref.shape[0] - tail) // zg

        def fill(c, carry):
            pltpu.make_async_copy(zero_s, xs_ref.at[pl.ds(tail + c * zg, zg)], sem).start()
            return carry

        def drain(c, carry):
            pltpu.make_async_copy(zero_s, zero_s, sem).wait()
            return carry

        lax.fori_loop(0, chunks, fill, 0)
        lax.fori_loop(0, chunks, drain, 0)


def _dispatch(plan, mask_t, gate_t, h2, rtot):
    n = h2.shape[0]
    ntd = n // TD
    tspec = pl.BlockSpec((NE, TD), lambda i, *_: (0, i))
    grid_spec = pltpu.PrefetchScalarGridSpec(
        num_scalar_prefetch=6, grid=(ntd,),
        in_specs=[tspec, tspec, pl.BlockSpec((TD, D), lambda i, *_: (i, 0))],
        out_specs=[pl.BlockSpec(memory_space=pl.ANY), pl.BlockSpec((8, TD), lambda i, *_: (0, i))],
        scratch_shapes=[pltpu.VMEM((2, RL // ALIGN, ALIGN, D), BF16), pltpu.VMEM((ZROWS // ALIGN, ALIGN, D), BF16),
                        pltpu.VMEM((TD, TD), BF16), pltpu.SemaphoreType.DMA((2,))])
    return pl.pallas_call(
        _dispatch_kernel, grid_spec=grid_spec,
        out_shape=[jax.ShapeDtypeStruct((rtot // ALIGN, ALIGN, D), BF16), jax.ShapeDtypeStruct((8, n), F32)],
        compiler_params=pltpu.CompilerParams(dimension_semantics=("arbitrary",), vmem_limit_bytes=VMEM_LIMIT),
        name="dispatch",
    )(plan["pc"], plan["lo"], plan["dst"], plan["tsum"], plan["zdst"], plan["zlen"], mask_t, gate_t, h2)


def _expert_kernel(texp_ref, nact_ref, x_ref, w1_ref, w3_ref, w2_ref, y_ref, w1_s, w3_s, w2_s):
    m = pl.program_id(0)
    new_expert = (m == 0) | (texp_ref[m] != texp_ref[jnp.maximum(m - 1, 0)])

    @pl.when(new_expert)
    def _():
        w1_s[...] = w1_ref[...].astype(BF16)
        w3_s[...] = w3_ref[...].astype(BF16)
        w2_s[...] = w2_ref[...].astype(BF16)

    @pl.when(m < nact_ref[0])
    def _():
        x = x_ref[...]
        a = jnp.dot(x, w1_s[...], preferred_element_type=F32)
        a = a * _sigmoid(a) * jnp.dot(x, w3_s[...], preferred_element_type=F32)
        y_ref[...] = jnp.dot(a.astype(BF16), w2_s[...], preferred_element_type=F32).astype(BF16)

    @pl.when(m >= nact_ref[0])
    def _():
        y_ref[...] = jnp.zeros_like(y_ref)


def _experts(plan, xs, layer, w1, w3, w2):
    rtot = xs.shape[0]
    n_mt = rtot // TMX
    grid_spec = pltpu.PrefetchScalarGridSpec(
        num_scalar_prefetch=2, grid=(n_mt,),
        in_specs=[pl.BlockSpec((TMX, D), lambda m, te, na: (jnp.minimum(m, na[0] - 1), 0)),
                  pl.BlockSpec((None, None, D, FF), lambda m, te, na: (layer, te[m], 0, 0)),
                  pl.BlockSpec((None, None, D, FF), lambda m, te, na: (layer, te[m], 0, 0)),
                  pl.BlockSpec((None, None, FF, D), lambda m, te, na: (layer, te[m], 0, 0))],
        out_specs=pl.BlockSpec((TMX, D), lambda m, te, na: (m, 0)),
        scratch_shapes=[pltpu.VMEM((D, FF), BF16), pltpu.VMEM((D, FF), BF16), pltpu.VMEM((FF, D), BF16)])
    return pl.pallas_call(
        _expert_kernel, grid_spec=grid_spec,
        out_shape=jax.ShapeDtypeStruct((rtot, D), BF16),
        compiler_params=pltpu.CompilerParams(dimension_semantics=("arbitrary",), vmem_limit_bytes=VMEM_LIMIT),
        name="experts",
    )(plan["texp"], plan["nact"], xs, w1, w3, w2)


def _combine_kernel(split, bidx_ref, pc_ref, lo_ref, dst_ref, ts_ref, info_ref, ys_ref, x1_ref, mod_ref,
                    lng_ref, lnb_ref, *rest):
    del bidx_ref
    o_refs, (loc_s, sems) = rest[:-2], rest[-2:]
    i = pl.program_id(0)
    slot = i % 2

    def fetch(tile, buf):
        for e in range(NE):
            _run_copies(ys_ref, loc_s.at[buf], dst_ref[tile * NE + e], lo_ref[tile * NE + e],
                        pc_ref[tile * NE + e], sems.at[buf], RUN_BITS)

    @pl.when(i == 0)
    def _():
        loc_s[...] = jnp.zeros_like(loc_s)
        fetch(0, 0)

    @pl.when(i + 1 < pl.num_programs(0))
    def _():
        fetch(i + 1, 1 - slot)

    _wait_groups(loc_s.at[slot], ts_ref[i], sems.at[slot], TILE_BITS)
    info = info_ref[...]
    d0 = info[0:1].astype(jnp.int32)
    d1 = info[1:2].astype(jnp.int32)
    r_i = lax.broadcasted_iota(jnp.int32, (RL, TD), 0)
    wperm = (jnp.where(r_i == d0, info[2:3], 0.0) + jnp.where(r_i == d1, info[3:4], 0.0)).astype(BF16)
    moe = lax.dot_general(wperm, loc_s[slot].reshape(RL, D), (((0,), (0,)), ((), ())),
                          preferred_element_type=F32)
    gate = mod_ref[...][:, 2 * D:]
    out = _ln(ALPHA * x1_ref[...] + gate * moe) * lng_ref[...] + lnb_ref[...]
    if len(o_refs) == 1:
        o_refs[0][...] = out
    else:
        @pl.when(i < split)
        def _():
            o_refs[0][...] = out

        @pl.when(i >= split)
        def _():
            o_refs[1][...] = out


def _combine(bidx_d, plan, info, ys, x1, mod2, lng, lnb, out_rows):
    n = x1.shape[0]
    ntd = n // TD
    tok = pl.BlockSpec((TD, D), lambda i, *_: (i, 0))
    full = lambda a: pl.BlockSpec(a.shape, lambda i, *_: (0,) * a.ndim)
    if len(out_rows) == 1:
        split, out_specs = 0, [tok]
    else:
        split = out_rows[0] // TD
        out_specs = [pl.BlockSpec((TD, D), lambda i, *_: (jnp.minimum(i, split - 1), 0)),
                     pl.BlockSpec((TD, D), lambda i, *_: (jnp.maximum(i - split, 0), 0))]
    grid_spec = pltpu.PrefetchScalarGridSpec(
        num_scalar_prefetch=5, grid=(ntd,),
        in_specs=[pl.BlockSpec((8, TD), lambda i, *_: (0, i)),
                  pl.BlockSpec(memory_space=pl.ANY),
                  tok,
                  pl.BlockSpec((None, 1, 3 * D), lambda i, b, *_: (b[i], 0, 0)),
                  full(lng), full(lnb)],
        out_specs=out_specs,
        scratch_shapes=[pltpu.VMEM((2, RL // ALIGN, ALIGN, D), BF16), pltpu.SemaphoreType.DMA((2,))])
    return pl.pallas_call(
        functools.partial(_combine_kernel, split), grid_spec=grid_spec,
        out_shape=[jax.ShapeDtypeStruct((r, D), F32) for r in out_rows],
        compiler_params=pltpu.CompilerParams(dimension_semantics=("arbitrary",), vmem_limit_bytes=VMEM_LIMIT),
        name="combine",
    )(bidx_d, plan["pc"], plan["lo"], plan["dst"], plan["tsum"], info, ys, x1, mod2, lng, lnb)


def _moe(meta, h2, mask_t, gate_t, cnt, layer, w1, w3, w2, x1, mod2, lng, lnb, out_rows):
    n = h2.shape[0]
    ntd = n // TD
    rtot = -(-(ntd * RL) // TMX) * TMX + NE * TMX
    cnt = cnt[:, :, :TM // TD].transpose(0, 2, 1).reshape(ntd, NE).astype(jnp.int32)
    plan = _plan(cnt, rtot // TMX)
    xs, info = _dispatch(plan, mask_t, gate_t, h2, rtot)
    ys = _experts(plan, xs.reshape(rtot, D), layer, w1, w3, w2).reshape(rtot // ALIGN, ALIGN, D)
    return _combine(meta["bidx"], plan, info, ys, x1, mod2, lng, lnb, out_rows)


def _tile_meta(groups, rows):
    bidx, pblk, first, last, cf, cb = [], [], [], [], [], []
    row = 0
    tile = 0
    for (b, s) in groups:
        assert s % rows == 0
        per = s // rows
        for bi in range(b):
            for j in range(per):
                bidx.append(row + bi)
                pblk.append(j)
                first.append(1 if j == 0 else 0)
                last.append(1 if j == per - 1 else 0)
                cf.append(tile + j)
                cb.append(tile + per - 1 - j)
            tile += per
        row += b
    as_i32 = lambda v: jnp.asarray(np.asarray(v, np.int32))
    return dict(bidx=as_i32(bidx), pblk=as_i32(pblk), first=as_i32(first), last=as_i32(last),
                cf=as_i32(cf), cb=as_i32(cb))


def _rope_tables(s_max):
    inv = jnp.power(ROPE_THETA, -jnp.arange(0, HD, 2, dtype=F32) / HD)
    ang = jnp.arange(s_max, dtype=F32)[:, None] * inv[None, :]
    cos, sin = jnp.cos(ang), jnp.sin(ang)
    cos128 = jnp.tile(cos, (1, 4))
    sin128 = jnp.tile(jnp.concatenate([-sin, sin], axis=1), (1, 2))
    return cos128, sin128


def _block_diag(w):
    d, nb, c, f = w.shape
    eye = jnp.eye(nb, dtype=w.dtype)
    return jnp.einsum('dncf,nm->dncmf', w, eye).reshape(d, nb * c, nb * f)


def _forward(xs, cs, w_mod, b_mod, w_in, attn_sink, conv_w, conv_b, lru_w_r, lru_b_r, lru_w_i, lru_b_i,
             lru_lambda, sg_norm_g, sg_w, sg_b, mix_norm_g, w_out, ln_g, ln_b, router_w, router_bias,
             exp_w1, exp_w3, exp_w2):
    groups = [(x.shape[0], x.shape[1]) for x in xs]
    assert len(groups) == 2
    meta_p, meta_a, meta_r = (_tile_meta(groups, rows) for rows in (TP, TA, TR))
    meta_m, meta_d = _tile_meta(groups, TM), _tile_meta(groups, TD)
    x_parts = [xx.reshape(-1, D) for xx in xs]
    group_rows = [p.shape[0] for p in x_parts]
    c_all = jnp.concatenate(cs, axis=0)
    bt = c_all.shape[0]
    mods = _modulation(c_all, w_mod, b_mod).reshape(2 * DEPTH, bt, 1, 3 * D)
    cos_t, sin_t = _rope_tables(max(s for _, s in groups))
    rwt = router_w.T[_slot_order()].astype(BF16)
    rb = router_bias[_slot_order()].reshape(NE, 1)
    for l in range(DEPTH):
        q, kk, vv, gy, xr, u, vn = _pre_mixer(meta_p, x_parts, mods[2 * l], w_in[l].astype(BF16), cos_t, sin_t,
                                              sg_norm_g[l].reshape(1, SGW))
        oa = _attention(meta_a, attn_sink[l], q, kk, vv)
        hf, hb = _recurrent(meta_r, xr, conv_w[l], conv_b[l].reshape(1, RW),
                            _block_diag(0.5 * lru_w_r[l]).astype(BF16), _block_diag(0.5 * lru_w_i[l]).astype(BF16),
                            0.5 * lru_b_r[l], 0.5 * lru_b_i[l], lru_lambda[l])
        sgb = jnp.repeat(sg_b[l].T, HD, axis=1)
        x1, h2, gate_t, mask_t, cnt = _merge(meta_m, x_parts, oa, gy, hf, hb, u, vn, mods[2 * l], mods[2 * l + 1],
                                             mix_norm_g[l].reshape(1, D), sg_w[l].astype(BF16), sgb,
                                             w_out[l].astype(BF16), ln_g[l, 0].reshape(1, D),
                                             ln_b[l, 0].reshape(1, D), rwt, rb)
        out_rows = group_rows if l == DEPTH - 1 else [sum(group_rows)]
        x_parts = _moe(meta_d, h2, mask_t, gate_t, cnt, l, exp_w1, exp_w3, exp_w2, x1, mods[2 * l + 1],
                       ln_g[l, 1].reshape(1, D), ln_b[l, 1].reshape(1, D), out_rows)
    return tuple(p.reshape(b, s, D) for p, (b, s) in zip(x_parts, groups))


def kernel(x_prompt, x_sample, c_prompt, c_sample, w_mod, b_mod, w_in, attn_sink, conv_w, conv_b, lru_w_r, lru_b_r,
           lru_w_i, lru_b_i, lru_lambda, sg_norm_g, sg_w, sg_b, mix_norm_g, w_out, ln_g, ln_b, router_w,
           router_bias, exp_w1, exp_w3, exp_w2):
    return _forward([x_prompt, x_sample], [c_prompt, c_sample], w_mod, b_mod, w_in, attn_sink, conv_w, conv_b,
                    lru_w_r, lru_b_r, lru_w_i, lru_b_i, lru_lambda, sg_norm_g, sg_w, sg_b, mix_norm_g, w_out,
                    ln_g, ln_b, router_w, router_bias, exp_w1, exp_w3, exp_w2)
```

```python
import functools

import numpy as np
import jax
import jax.numpy as jnp
from jax import lax
from jax.experimental import pallas as pl
from jax.experimental.pallas import tpu as pltpu

F32 = jnp.float32
BF16 = jnp.bfloat16

D = 1024
DEPTH = 2
HD = 64
AW = 384
KVW = 128
RW = 384
SGW = 256
WINDOW = 128
CONV_WIDTH = 4
LRU_C = 8.0
CHUNK = 128
NE = 16
NG = 4
EPG = NE // NG
FF = 512
ALPHA = (2 * DEPTH) ** 0.25
LN_EPS = 1e-5
RMS_EPS = 1e-6
ROPE_THETA = 10000.0

TM = 1024
TP = 1024
TA = 2048
TR = 2048
QB = 128
SEG = TR // 8
NLG = RW // 128
TD = 512
ALIGN = 16
RL = 2 * TD + NE * ALIGN
TMX = 1024
ZROWS = 256
NEG = -1e30
VMEM_LIMIT = 48 * 1024 * 1024

C_Q, C_K, C_V, C_Y, C_R, C_U, C_SV, C_END = 0, 384, 512, 640, 1024, 1408, 1664, 1920


def _ln(x):
    mu = jnp.mean(x, axis=-1, keepdims=True)
    xc = x - mu
    var = jnp.mean(xc * xc, axis=-1, keepdims=True)
    return xc * lax.rsqrt(var + LN_EPS)


def _rms(x):
    return x * lax.rsqrt(jnp.mean(x * x, axis=-1, keepdims=True) + RMS_EPS)


def _sigmoid(z):
    return 0.5 * (jnp.tanh(0.5 * z) + 1.0)


def _gelu(x):
    return 0.5 * x * (1.0 + jnp.tanh(0.7978845608028654 * (x + 0.044715 * (x * x * x))))


def _split_bf16(a):
    hi = a.astype(BF16)
    lo = (a - hi.astype(F32)).astype(BF16)
    return hi, lo


def _mod_kernel(c_ref, w_ref, b_ref, o_ref):
    c = c_ref[...]
    ca = c * _sigmoid(c)
    ch, cl = _split_bf16(ca)
    wh, wl = _split_bf16(w_ref[...])
    acc = jnp.dot(ch, wh, preferred_element_type=F32)
    acc += jnp.dot(ch, wl, preferred_element_type=F32)
    acc += jnp.dot(cl, wh, preferred_element_type=F32)
    o_ref[...] = acc + b_ref[...]


def _modulation(c_all, w_mod, b_mod):
    bt = c_all.shape[0]
    cb = 768
    w = w_mod.reshape(2 * DEPTH, D, 3 * D)
    b = b_mod.reshape(2 * DEPTH, 1, 3 * D)
    return pl.pallas_call(
        _mod_kernel,
        grid=(2 * DEPTH, 3 * D // cb),
        in_specs=[pl.BlockSpec((bt, D), lambda s, j: (0, 0)),
                  pl.BlockSpec((None, D, cb), lambda s, j: (s, 0, j)),
                  pl.BlockSpec((None, 1, cb), lambda s, j: (s, 0, j))],
        out_specs=pl.BlockSpec((None, bt, cb), lambda s, j: (s, 0, j)),
        out_shape=jax.ShapeDtypeStruct((2 * DEPTH, bt, 3 * D), F32),
        name="modulation",
    )(c_all, w, b)


def _load_tokens(i, x_refs, split):
    if len(x_refs) == 1:
        return x_refs[0][...]
    return jnp.where(i < split, x_refs[0][...], x_refs[1][...])


def _token_specs(x_parts, rows):
    if len(x_parts) == 1:
        return [pl.BlockSpec((rows, D), lambda i, *_: (i, 0))], 0
    split = x_parts[0].shape[0] // rows
    return [pl.BlockSpec((rows, D), lambda i, *_: (jnp.minimum(i, split - 1), 0)),
            pl.BlockSpec((rows, D), lambda i, *_: (jnp.maximum(i - split, 0), 0))], split


def _pre_kernel(split, bidx_ref, pblk_ref, *refs):
    del bidx_ref, pblk_ref
    nx = len(refs) - 12
    x_refs = refs[:nx]
    mod_ref, w_ref, cos_ref, sin_ref, sgg_ref, q_ref, kk_ref, vv_ref, gy_ref, xr_ref, u_ref, vn_ref = refs[nx:]
    mod = mod_ref[...]
    shift, scale = mod[:, :D], mod[:, D:2 * D]
    h = (_ln(_load_tokens(pl.program_id(0), x_refs, split)) * (1.0 + scale) + shift).astype(BF16)
    cos = cos_ref[...]
    sin = sin_ref[...]
    lane = lax.broadcasted_iota(jnp.int32, (1, 128), 1)
    first_half = (lane % HD) < (HD // 2)

    def rope128(xg):
        rot = jnp.where(first_half, pltpu.roll(xg, 128 - HD // 2, 1), pltpu.roll(xg, HD // 2, 1))
        return xg * cos + rot * sin

    qk = jnp.dot(h, w_ref[:, C_Q:C_V], preferred_element_type=F32)
    for g in range(AW // 128):
        q_ref[:, g * 128:(g + 1) * 128] = (rope128(qk[:, g * 128:(g + 1) * 128]) * (HD ** -0.5)).astype(BF16)
    vg = jnp.dot(h, w_ref[:, C_V:C_R], preferred_element_type=F32)
    lo_half = lane < HD
    for src, dst_ref in ((rope128(qk[:, C_K:C_V]), kk_ref), (vg[:, :KVW], vv_ref)):
        swapped = pltpu.roll(src, HD, 1)
        dst_ref[:, 0:128] = jnp.where(lo_half, src, swapped).astype(BF16)
        dst_ref[:, 128:256] = jnp.where(lo_half, swapped, src).astype(BF16)
    gy_ref[...] = _gelu(vg[:, KVW:]).astype(gy_ref.dtype)
    rest = jnp.dot(h, w_ref[:, C_R:C_END], preferred_element_type=F32)
    for g in range(NLG):
        xr_ref[g] = rest[:, g * 128:(g + 1) * 128]
    u_ref[...] = _gelu(rest[:, C_U - C_R:C_SV - C_R]).astype(u_ref.dtype)
    sv = _gelu(rest[:, C_SV - C_R:])
    vn_ref[...] = (_ln(sv) * sgg_ref[...]).astype(BF16)


def _pre_mixer(meta, x_parts, mod_l, w_ext, cos_t, sin_t, sgg):
    n = sum(p.shape[0] for p in x_parts)
    nt = n // TP
    tok = lambda w: pl.BlockSpec((TP, w), lambda i, b, p: (i, 0))
    full = lambda a: pl.BlockSpec(a.shape, lambda i, b, p: (0,) * a.ndim)
    x_specs, split = _token_specs(x_parts, TP)
    grid_spec = pltpu.PrefetchScalarGridSpec(
        num_scalar_prefetch=2, grid=(nt,),
        in_specs=x_specs + [
            pl.BlockSpec((None, 1, 3 * D), lambda i, b, p: (b[i], 0, 0)),
            full(w_ext),
            pl.BlockSpec((TP, 128), lambda i, b, p: (p[i], 0)),
            pl.BlockSpec((TP, 128), lambda i, b, p: (p[i], 0)),
            full(sgg)],
        out_specs=[tok(AW), tok(256), tok(256), tok(RW),
                   pl.BlockSpec((NLG, TP, 128), lambda i, b, p: (0, i, 0)), tok(SGW), tok(SGW)])
    sds = lambda w, dt: jax.ShapeDtypeStruct((n, w), dt)
    return pl.pallas_call(
        functools.partial(_pre_kernel, split), grid_spec=grid_spec,
        out_shape=[sds(AW, BF16), sds(256, BF16), sds(256, BF16), sds(RW, BF16),
                   jax.ShapeDtypeStruct((NLG, n, 128), F32), sds(SGW, BF16), sds(SGW, BF16)],
        compiler_params=pltpu.CompilerParams(dimension_semantics=("parallel",), vmem_limit_bytes=VMEM_LIMIT),
        name="pre_mixer",
    )(meta["bidx"], meta["pblk"], *x_parts, mod_l, w_ext, cos_t, sin_t, sgg)


def _attn_kernel(first_ref, last_ref, sink_ref, q_ref, k_ref, kp_ref, kn_ref, v_ref, vp_ref, vn_ref,
                 o_ref, kw_s, vw_s):
    i = pl.program_id(0)
    is_first = first_ref[i] == 1
    is_last = last_ref[i] == 1
    kw_s[0:QB] = kp_ref[...]
    kw_s[QB:QB + TA] = k_ref[...]
    kw_s[QB + TA:] = kn_ref[...]
    vw_s[0:QB] = vp_ref[...]
    vw_s[QB:QB + TA] = v_ref[...]
    vw_s[QB + TA:] = vn_ref[...]
    iq = lax.broadcasted_iota(jnp.int32, (QB, 3 * QB), 0)
    ik = lax.broadcasted_iota(jnp.int32, (QB, 3 * QB), 1)
    rel = ik - iq
    band = (rel >= 0) & (rel <= 2 * WINDOW)
    lane = lax.broadcasted_iota(jnp.int32, (1, 128), 1)
    lo_half = lane < HD
    zero = jnp.zeros((), BF16)
    gsz = AW // HD // 2
    row_head = lax.broadcasted_iota(jnp.int32, (gsz * QB, 1), 0) // QB
    for j in range(TA // QB):
        ok = band
        if j == 0:
            ok = ok & ((ik >= QB) | jnp.logical_not(is_first))
        if j == TA // QB - 1:
            ok = ok & ((ik < 2 * QB) | jnp.logical_not(is_last))
        bias = jnp.where(ok, 0.0, NEG)
        bias = jnp.concatenate([bias] * gsz, axis=0)
        rows = slice(j * QB, (j + 1) * QB)
        keys = slice(j * QB, j * QB + 3 * QB)
        outs = [None] * (AW // HD)
        for g in range(2):
            heads = range(g * gsz, (g + 1) * gsz)
            qs = []
            for hh in heads:
                qg = q_ref[rows, (hh // 2) * 128:(hh // 2 + 1) * 128]
                qs.append(jnp.where(lo_half if hh % 2 == 0 else jnp.logical_not(lo_half), qg, zero))
            s = lax.dot_general(jnp.concatenate(qs, axis=0), kw_s[keys, g * 128:(g + 1) * 128],
                                (((1,), (1,)), ((), ())), preferred_element_type=F32) + bias
            sink = jnp.full((gsz * QB, 1), sink_ref[heads[-1]], F32)
            for t in range(gsz - 1):
                sink = jnp.where(row_head == t, sink_ref[heads[t]], sink)
            m = jnp.maximum(jnp.max(s, axis=-1, keepdims=True), sink)
            p = jnp.exp(s - m)
            denom = jnp.sum(p, axis=-1, keepdims=True) + jnp.exp(sink - m)
            pv = jnp.dot(p.astype(BF16), vw_s[keys, g * 128:(g + 1) * 128], preferred_element_type=F32)
            o = pv * (1.0 / denom)
            for t, hh in enumerate(heads):
                outs[hh] = o[t * QB:(t + 1) * QB]
        for jg in range(AW // 128):
            o_ref[rows, jg * 128:(jg + 1) * 128] = jnp.where(lo_half, outs[2 * jg], outs[2 * jg + 1]).astype(o_ref.dtype)


def _attention(meta, sink, q, kk, vv):
    n = q.shape[0]
    nt = n // TA
    nqb = n // QB
    r = TA // QB
    main = lambda w: pl.BlockSpec((TA, w), lambda i, f, l, s: (i, 0))
    prev = pl.BlockSpec((QB, 256), lambda i, f, l, s: (jnp.maximum(i * r - 1, 0), 0))
    nxt = pl.BlockSpec((QB, 256), lambda i, f, l, s: (jnp.minimum(i * r + r, nqb - 1), 0))
    grid_spec = pltpu.PrefetchScalarGridSpec(
        num_scalar_prefetch=3, grid=(nt,),
        in_specs=[main(AW), main(256), prev, nxt, main(256), prev, nxt],
        out_specs=main(AW),
        scratch_shapes=[pltpu.VMEM((TA + 2 * QB, 256), BF16), pltpu.VMEM((TA + 2 * QB, 256), BF16)])
    return pl.pallas_call(
        _attn_kernel, grid_spec=grid_spec,
        out_shape=jax.ShapeDtypeStruct((n, AW), BF16),
        compiler_params=pltpu.CompilerParams(dimension_semantics=("parallel",), vmem_limit_bytes=VMEM_LIMIT),
        name="attention",
    )(meta["first"], meta["last"], sink, q, kk, kk, kk, vv, vv, vv)


def _rec_kernel(cf_ref, cb_ref, first_ref, last_ref,
                x_hbm, xfp_ref, xfn_ref, xbp_ref, xbn_ref,
                cw_ref, cbias_ref, wg_ref, bg_ref, lam_ref,
                hf_hbm, hb_hbm,
                ext_s, a_s, b_s, hs_s, ps_s, out_s, carry_s, gsem, ssem):
    i = pl.program_id(0)
    nsteps = pl.num_programs(0)
    slot = i % 2
    cw = cw_ref[...]
    cbias = cbias_ref[...]
    sub = lax.broadcasted_iota(jnp.int32, (8, 128), 0)
    unroll = 8
    chunk_refs = (cf_ref, cb_ref)
    out_hbm = (hf_hbm, hb_hbm)

    def gather(step, buf, d, start):
        c = chunk_refs[d][step]
        for g in range(NLG):
            for s in range(8):
                cp = pltpu.make_async_copy(x_hbm.at[g, pl.ds(pl.multiple_of(c * TR + s * SEG, SEG), SEG), :],
                                           ext_s.at[buf, d, g, pl.ds(2, SEG), s, :], gsem.at[buf, d])
                if start:
                    cp.start()
                else:
                    cp.wait()

    def scatter(d, c, start):
        for g in range(NLG):
            for s in range(8):
                cp = pltpu.make_async_copy(out_s.at[d, g, :, s, :],
                                           out_hbm[d].at[g, pl.ds(pl.multiple_of(c * TR + s * SEG, SEG), SEG), :],
                                           ssem.at[d])
                if start:
                    cp.start()
                else:
                    cp.wait()

    @pl.when(i == 0)
    def _():
        gather(0, 0, 0, True)
        gather(0, 0, 1, True)

    @pl.when(i + 1 < nsteps)
    def _():
        gather(i + 1, 1 - slot, 0, True)
        gather(i + 1, 1 - slot, 1, True)

    def run(d, xp_ref, xn_ref):
        c = chunk_refs[d][i]
        first = first_ref[c] == 1
        last = last_ref[c] == 1
        gather(i, slot, d, False)
        ext = ext_s.at[slot, d]
        planes = []
        for g in range(NLG):
            prev = jnp.where(first, 0.0, xp_ref[g])
            nxt = jnp.where(last, 0.0, xn_ref[g])
            ext[g, 0] = jnp.where(sub == 0, prev[6:7], pltpu.roll(ext[g, SEG], 1, 0))
            ext[g, 1] = jnp.where(sub == 0, prev[7:8], pltpu.roll(ext[g, SEG + 1], 1, 0))
            ext[g, SEG + 2] = jnp.where(sub == 7, nxt[0:1], pltpu.roll(ext[g, 2], 7, 0))
            lanes = slice(g * 128, (g + 1) * 128)
            acc = ext[g, 0:SEG] * cw[0:1, lanes]
            for t in range(1, CONV_WIDTH):
                acc = acc + ext[g, t:t + SEG] * cw[t:t + 1, lanes]
            planes.append(acc.reshape(TR, 128))
        xc = jnp.concatenate(planes, axis=1) + cbias
        xb16 = xc.astype(BF16)
        tg = jnp.tanh(jnp.dot(xb16, wg_ref[d], preferred_element_type=F32) + bg_ref[d:d + 1])
        tr, ti = tg[:, :RW], tg[:, RW:]
        nlam = -lam_ref[d:d + 1]
        softplus = jnp.maximum(nlam, 0.0) + jnp.log1p(jnp.exp(-jnp.abs(nlam)))
        half_c = (-0.5 * LRU_C) * softplus
        log_a = half_c * tr + half_c
        a = jnp.exp(log_a)
        v = (1.0 + a * a) * jnp.tanh(-log_a)
        b = jnp.where(v > 0.0, v * lax.rsqrt(v), 0.0) * (ti + 1.0) * (0.5 * xc)
        for g in range(NLG):
            a_s[g] = a[:, g * 128:(g + 1) * 128].reshape(SEG, 8, 128)
            b_s[g] = b[:, g * 128:(g + 1) * 128].reshape(SEG, 8, 128)

        reverse = d == 1
        reset = last if reverse else first

        @pl.when(reset)
        def _():
            carry_s[d] = jnp.zeros((NLG, 128), F32)

        def step(jo, hp):
            hp = list(hp)
            for ji in range(unroll):
                jj = jo * unroll + ji
                j = SEG - 1 - jj if reverse else jj
                for g in range(NLG):
                    h, p = hp[g]
                    ag = a_s[g, j]
                    h = ag * h + b_s[g, j]
                    p = p * ag
                    hs_s[g, j] = h
                    ps_s[g, j] = p
                    hp[g] = (h, p)
            return tuple(hp)

        init = tuple((jnp.zeros((8, 128), F32), jnp.ones((8, 128), F32)) for _ in range(NLG))
        ends = lax.fori_loop(0, SEG // unroll, step, init)
        cmats = []
        for g in range(NLG):
            e, pe = ends[g]
            c_in = carry_s[d, g:g + 1]
            rows = [None] * 8
            for s in (range(7, -1, -1) if reverse else range(8)):
                rows[s] = c_in
                c_in = e[s:s + 1] + pe[s:s + 1] * c_in
            carry_s[d, g:g + 1] = c_in
            cmats.append(jnp.concatenate(rows, axis=0))

        @pl.when(i > 0)
        def _():
            scatter(d, 0, False)

        def fix(jo, carry):
            for ji in range(unroll):
                j = jo * unroll + ji
                for g in range(NLG):
                    out_s[d, g, j] = hs_s[g, j] + ps_s[g, j] * cmats[g]
            return carry

        lax.fori_loop(0, SEG // unroll, fix, 0)
        scatter(d, c, True)

    run(0, xfp_ref, xfn_ref)
    run(1, xbp_ref, xbn_ref)

    @pl.when(i == nsteps - 1)
    def _():
        scatter(0, 0, False)
        scatter(1, 0, False)


def _recurrent(meta, xr, cw, cbias, wg, bg, lam):
    n = xr.shape[1]
    nt = n // TR
    n8 = n // 8
    r8 = TR // 8

    def halos(which):
        sel = (lambda cf, cb: cf) if which == 0 else (lambda cf, cb: cb)
        prev = pl.BlockSpec((NLG, 8, 128),
                            lambda i, cf, cb, f, l: (0, jnp.maximum(sel(cf, cb)[i] * r8 - 1, 0), 0))
        nxt = pl.BlockSpec((NLG, 8, 128),
                           lambda i, cf, cb, f, l: (0, jnp.minimum(sel(cf, cb)[i] * r8 + r8, n8 - 1), 0))
        return prev, nxt

    full = lambda a: pl.BlockSpec(a.shape, lambda i, cf, cb, f, l: (0,) * a.ndim)
    anywhere = pl.BlockSpec(memory_space=pl.ANY)
    plane = lambda lead: pltpu.VMEM(lead + (NLG, SEG, 8, 128), F32)
    grid_spec = pltpu.PrefetchScalarGridSpec(
        num_scalar_prefetch=4, grid=(nt,),
        in_specs=[anywhere, *halos(0), *halos(1),
                  full(cw), full(cbias), full(wg), full(bg), full(lam)],
        out_specs=[anywhere, anywhere],
        scratch_shapes=[pltpu.VMEM((2, 2, NLG, SEG + 3, 8, 128), F32),
                        plane(()), plane(()), plane(()), plane(()), plane((2,)),
                        pltpu.VMEM((2, NLG, 128), F32),
                        pltpu.SemaphoreType.DMA((2, 2)), pltpu.SemaphoreType.DMA((2,))])
    return pl.pallas_call(
        _rec_kernel, grid_spec=grid_spec,
        out_shape=[jax.ShapeDtypeStruct((NLG, n, 128), F32), jax.ShapeDtypeStruct((NLG, n, 128), F32)],
        compiler_params=pltpu.CompilerParams(dimension_semantics=("arbitrary",), vmem_limit_bytes=VMEM_LIMIT),
        name="recurrent",
    )(meta["cf"], meta["cb"], meta["first"], meta["last"], xr, xr, xr, xr, xr, cw, cbias, wg, bg, lam)


def _route(sel, score):
    def before(vk, vj, k, j):
        return (vk > vj) | ((vk == vj) & (k < j)) if k < j else (vk > vj)

    def count(flags):
        return sum(jnp.where(f, 1.0, 0.0) for f in flags)

    slot = lambda a, j: a[j * NG:(j + 1) * NG]
    cand = [slot(sel, j) for j in range(EPG)]
    in_top = [jnp.where(count(before(cand[k], cand[j], k, j) for k in range(EPG) if k != j) < 2.0, 1.0, 0.0)
              for j in range(EPG)]
    gscore = sum(in_top[j] * cand[j] for j in range(EPG))
    grow = [gscore[g:g + 1] for g in range(NG)]
    best = jnp.concatenate(
        [jnp.where(count(before(grow[k], grow[g], k, g) for k in range(NG) if k != g) < 1.0, 1.0, 0.0)
         for g in range(NG)], axis=0)
    picked = [in_top[j] * best * slot(score, j) for j in range(EPG)]
    inv = 1.0 / sum(jnp.sum(p, axis=0, keepdims=True) for p in picked)
    mask = jnp.concatenate([in_top[j] * best for j in range(EPG)], axis=0)
    gate = jnp.concatenate([p * inv for p in picked], axis=0)
    return mask, gate


def _slot_order():
    return np.asarray([g * EPG + j for j in range(EPG) for g in range(NG)], np.int32)


def _merge_kernel(split, bidx_ref, *refs):
    del bidx_ref
    nx = len(refs) - 22
    x_refs = refs[:nx]
    (oa_ref, gy_ref, hf_ref, hb_ref, u_ref, vn_ref, mod1_ref, mod2_ref, gmix_ref, sgw_ref, sgb_ref, wout_ref,
     lng_ref, lnb_ref, rwt_ref, rb_ref, x1_ref, h2_ref, gate_ref, mask_ref, cnt_ref, mrg_s) = refs[nx:]
    gmix = gmix_ref[...]
    mrg_s[:, 0:AW] = (_rms(oa_ref[...].astype(F32)) * gmix[:, 0:AW]).astype(BF16)
    hsum = jnp.concatenate([hf_ref[g] + hb_ref[g] for g in range(NLG)], axis=1)
    o_rec = gy_ref[...].astype(F32) * hsum
    mrg_s[:, AW:AW + RW] = (_rms(o_rec) * gmix[:, AW:AW + RW]).astype(BF16)
    lane = lax.broadcasted_iota(jnp.int32, (1, 128), 1)
    lo_half = lane < HD
    zero = jnp.zeros((), BF16)
    pieces = []
    for c in range(TM // CHUNK):
        rows = slice(c * CHUNK, (c + 1) * CHUNK)
        grp = []
        for g in range(SGW // 128):
            vg = vn_ref[rows, g * 128:(g + 1) * 128]
            mixed = jnp.dot(sgw_ref[2 * g], jnp.where(lo_half, vg, zero), preferred_element_type=F32)
            mixed += jnp.dot(sgw_ref[2 * g + 1], jnp.where(lo_half, zero, vg), preferred_element_type=F32)
            grp.append(mixed)
        mixed = jnp.concatenate(grp, axis=1) + sgb_ref[...]
        pieces.append(u_ref[rows, :].astype(F32) * mixed)
    o_sg = jnp.concatenate(pieces, axis=0)
    mrg_s[:, AW + RW:] = (_rms(o_sg) * gmix[:, AW + RW:]).astype(BF16)
    o = jnp.dot(mrg_s[...], wout_ref[...], preferred_element_type=F32)
    gate1 = mod1_ref[...][:, 2 * D:]
    x_in = _load_tokens(pl.program_id(0), x_refs, split)
    x1 = _ln(ALPHA * x_in + gate1 * o) * lng_ref[...] + lnb_ref[...]
    x1_ref[...] = x1
    mod2 = mod2_ref[...]
    h2 = (_ln(x1) * (1.0 + mod2[:, D:2 * D]) + mod2[:, :D]).astype(BF16)
    h2_ref[...] = h2
    logits = lax.dot_general(rwt_ref[...], h2, (((1,), (1,)), ((), ())), preferred_element_type=F32)
    score = _sigmoid(logits)
    sel = score + rb_ref[...]
    mask_t, gate_t = _route(sel, score)
    gate_ref[...] = gate_t
    mask_ref[...] = mask_t
    t_i = lax.broadcasted_iota(jnp.int32, (TM, 128), 0)
    j_i = lax.broadcasted_iota(jnp.int32, (TM, 128), 1)
    in_tile = jnp.where((t_i >= j_i * TD) & (t_i < (j_i + 1) * TD), 1.0, 0.0).astype(BF16)
    cnt_ref[...] = jnp.dot(mask_t.astype(BF16), in_tile, preferred_element_type=F32)


def _merge(meta, x_parts, oa, gy, hf, hb, u, vn, mod1, mod2, gmix, sgw, sgb, wout, lng, lnb, rwt, rb):
    n = oa.shape[0]
    nt = n // TM
    x_specs, split = _token_specs(x_parts, TM)
    tok = lambda w: pl.BlockSpec((TM, w), lambda i, b: (i, 0))
    full = lambda a: pl.BlockSpec(a.shape, lambda i, b: (0,) * a.ndim)
    modspec = pl.BlockSpec((None, 1, 3 * D), lambda i, b: (b[i], 0, 0))
    tspec = pl.BlockSpec((NE, TM), lambda i, b: (0, i))
    rec = pl.BlockSpec((NLG, TM, 128), lambda i, b: (0, i, 0))
    grid_spec = pltpu.PrefetchScalarGridSpec(
        num_scalar_prefetch=1, grid=(nt,),
        in_specs=x_specs + [
            tok(AW), tok(RW), rec, rec, tok(SGW), tok(SGW), modspec, modspec,
            full(gmix), full(sgw), full(sgb), full(wout), full(lng), full(lnb), full(rwt), full(rb)],
        out_specs=[tok(D), tok(D), tspec, tspec, pl.BlockSpec((None, NE, 128), lambda i, b: (i, 0, 0))],
        scratch_shapes=[pltpu.VMEM((TM, D), BF16)])
    return pl.pallas_call(
        functools.partial(_merge_kernel, split), grid_spec=grid_spec,
        out_shape=[jax.ShapeDtypeStruct((n, D), F32), jax.ShapeDtypeStruct((n, D), BF16),
                   jax.ShapeDtypeStruct((NE, n), F32), jax.ShapeDtypeStruct((NE, n), F32),
                   jax.ShapeDtypeStruct((nt, NE, 128), F32)],
        compiler_params=pltpu.CompilerParams(dimension_semantics=("parallel",), vmem_limit_bytes=VMEM_LIMIT),
        name="merge_route",
    )(meta["bidx"], *x_parts, oa, gy, hf, hb, u, vn, mod1, mod2, gmix, sgw, sgb, wout, lng, lnb, rwt, rb)


def _plan(cnt, n_mt):
    pc = (cnt + (ALIGN - 1)) // ALIGN * ALIGN
    lo = jnp.cumsum(pc, axis=1) - pc
    tot = jnp.sum(pc, axis=0)
    seg = (tot + (TMX - 1)) // TMX * TMX
    gend = jnp.cumsum(seg)
    gstart = gend - seg
    dst = gstart[None, :] + jnp.cumsum(pc, axis=0) - pc
    tile_row = jnp.arange(n_mt, dtype=jnp.int32) * TMX
    texp = jnp.minimum(jnp.sum((gend[None, :] <= tile_row[:, None]).astype(jnp.int32), axis=1), NE - 1)
    texp = jnp.asarray(_slot_order())[texp]
    nact = (gend[-1] // TMX).reshape(1)
    grp = lambda a: (a // ALIGN).astype(jnp.int32)
    return dict(pc=grp(pc).reshape(-1), lo=grp(lo).reshape(-1), dst=grp(dst).reshape(-1),
                tsum=grp(jnp.sum(pc, axis=1)),
                zdst=grp(jnp.concatenate([gstart + tot, gend[-1:]])), zlen=grp(seg - tot),
                texp=texp.astype(jnp.int32), nact=nact.astype(jnp.int32))


def _run_copies(src, dst, src_off, dst_off, groups, sem, max_bits):
    def arm(b):
        @pl.when(((groups >> b) & 1) == 1)
        def _():
            off = (groups >> (b + 1)) << (b + 1)
            pltpu.make_async_copy(src.at[pl.ds(src_off + off, 1 << b)], dst.at[pl.ds(dst_off + off, 1 << b)],
                                  sem).start()

    for b in range(min(COMMON_BITS, max_bits)):
        arm(b)
    if max_bits > COMMON_BITS:
        @pl.when(groups >= (1 << COMMON_BITS))
        def _():
            for b in range(COMMON_BITS, max_bits):
                arm(b)


def _wait_groups(buf, groups, sem, max_bits):
    for b in range(max_bits):
        @pl.when(((groups >> b) & 1) == 1)
        def _():
            pltpu.make_async_copy(buf.at[pl.ds(0, 1 << b)], buf.at[pl.ds(0, 1 << b)], sem).wait()


COMMON_BITS = 3
RUN_BITS = (TD // ALIGN).bit_length()
TILE_BITS = (RL // ALIGN).bit_length()


def _dispatch_kernel(pc_ref, lo_ref, dst_ref, ts_ref, zd_ref, zl_ref, mask_ref, gate_ref, h_ref,
                     xs_ref, info_ref, loc_s, zero_s, earlier_s, sems):
    i = pl.program_id(0)
    last_step = pl.num_programs(0) - 1
    slot = i % 2
    mask = mask_ref[...]
    mb = mask.astype(BF16)
    @pl.when(i == 0)
    def _():
        s_i = lax.broadcasted_iota(jnp.int32, (TD, TD), 0)
        t_i = lax.broadcasted_iota(jnp.int32, (TD, TD), 1)
        earlier_s[...] = jnp.where(s_i < t_i, 1.0, 0.0).astype(BF16)

    rank = jnp.dot(mb, earlier_s[...], preferred_element_type=F32)
    e_i = lax.broadcasted_iota(jnp.int32, (NE, NE), 0)
    f_i = lax.broadcasted_iota(jnp.int32, (NE, NE), 1)
    below = jnp.where(f_i < e_i, 1.0, 0.0).astype(BF16)
    lower = jnp.dot(below, mb, preferred_element_type=F32)
    row_e = lax.broadcasted_iota(jnp.int32, (NE, 1), 0)
    lo_vec = jnp.zeros((NE, 1), F32)
    for e in range(NE):
        lo_vec = jnp.where(row_e == e, (lo_ref[i * NE + e] * ALIGN).astype(F32), lo_vec)
    row = lo_vec + rank
    is0 = mask * jnp.where(lower == 0.0, 1.0, 0.0)
    is1 = mask - is0
    d0 = jnp.sum(is0 * row, axis=0, keepdims=True)
    d1 = jnp.sum(is1 * row, axis=0, keepdims=True)
    gate = gate_ref[...]
    w0 = jnp.sum(is0 * gate, axis=0, keepdims=True)
    w1 = jnp.sum(is1 * gate, axis=0, keepdims=True)
    info_ref[...] = jnp.concatenate([d0, d1, w0, w1, jnp.zeros((4, TD), F32)], axis=0)
    r_i = lax.broadcasted_iota(jnp.int32, (RL, TD), 0)
    perm = jnp.where((r_i == d0.astype(jnp.int32)) | (r_i == d1.astype(jnp.int32)), 1.0, 0.0).astype(BF16)
    loc = loc_s.at[slot]
    loc[...] = jnp.dot(perm, h_ref[...], preferred_element_type=F32).astype(BF16).reshape(RL // ALIGN, ALIGN, D)
    for e in range(NE):
        _run_copies(loc, xs_ref, lo_ref[i * NE + e], dst_ref[i * NE + e], pc_ref[i * NE + e], sems.at[slot],
                    RUN_BITS)

    @pl.when(i > 0)
    def _():
        _wait_groups(loc_s.at[1 - slot], ts_ref[jnp.maximum(i - 1, 0)], sems.at[1 - slot], TILE_BITS)

    @pl.when(i == last_step)
    def _():
        _wait_groups(loc, ts_ref[i], sems.at[slot], TILE_BITS)
        sem = sems.at[0]
        zero_s[...] = jnp.zeros_like(zero_s)
        zg = ZROWS // ALIGN
        for wait in (False, True):
            for e in range(NE):
                for part in range(TMX // ZROWS):
                    groups = jnp.clip(zl_ref[e] - part * zg, 0, zg)
                    if wait:
                        _wait_groups(zero_s, groups, sem, zg.bit_length())
                    else:
                        _run_copies(zero_s, xs_ref, 0, zd_ref[e] + part * zg, groups, sem, zg.bit_length())
        tail = zd_ref[NE]
        chunks = (xs_ref.shape[0] - tail) // zg

        def fill(c, carry):
            pltpu.make_async_copy(zero_s, xs_ref.at[pl.ds(tail + c * zg, zg)], sem).start()
            return carry

        def drain(c, carry):
            pltpu.make_async_copy(zero_s, zero_s, sem).wait()
            return carry

        lax.fori_loop(0, chunks, fill, 0)
        lax.fori_loop(0, chunks, drain, 0)


def _dispatch(plan, mask_t, gate_t, h2, rtot):
    n = h2.shape[0]
    ntd = n // TD
    tspec = pl.BlockSpec((NE, TD), lambda i, *_: (0, i))
    grid_spec = pltpu.PrefetchScalarGridSpec(
        num_scalar_prefetch=6, grid=(ntd,),
        in_specs=[tspec, tspec, pl.BlockSpec((TD, D), lambda i, *_: (i, 0))],
        out_specs=[pl.BlockSpec(memory_space=pl.ANY), pl.BlockSpec((8, TD), lambda i, *_: (0, i))],
        scratch_shapes=[pltpu.VMEM((2, RL // ALIGN, ALIGN, D), BF16), pltpu.VMEM((ZROWS // ALIGN, ALIGN, D), BF16),
                        pltpu.VMEM((TD, TD), BF16), pltpu.SemaphoreType.DMA((2,))])
    return pl.pallas_call(
        _dispatch_kernel, grid_spec=grid_spec,
        out_shape=[jax.ShapeDtypeStruct((rtot // ALIGN, ALIGN, D), BF16), jax.ShapeDtypeStruct((8, n), F32)],
        compiler_params=pltpu.CompilerParams(dimension_semantics=("arbitrary",), vmem_limit_bytes=VMEM_LIMIT),
        name="dispatch",
    )(plan["pc"], plan["lo"], plan["dst"], plan["tsum"], plan["zdst"], plan["zlen"], mask_t, gate_t, h2)


def _expert_kernel(texp_ref, nact_ref, x_ref, w1_ref, w3_ref, w2_ref, y_ref, w1_s, w3_s, w2_s):
    m = pl.program_id(0)
    new_expert = (m == 0) | (texp_ref[m] != texp_ref[jnp.maximum(m - 1, 0)])

    @pl.when(new_expert)
    def _():
        w1_s[...] = w1_ref[...].astype(BF16)
        w3_s[...] = w3_ref[...].astype(BF16)
        w2_s[...] = w2_ref[...].astype(BF16)

    @pl.when(m < nact_ref[0])
    def _():
        x = x_ref[...]
        a = jnp.dot(x, w1_s[...], preferred_element_type=F32)
        a = a * _sigmoid(a) * jnp.dot(x, w3_s[...], preferred_element_type=F32)
        y_ref[...] = jnp.dot(a.astype(BF16), w2_s[...], preferred_element_type=F32).astype(BF16)

    @pl.when(m >= nact_ref[0])
    def _():
        y_ref[...] = jnp.zeros_like(y_ref)


def _experts(plan, xs, layer, w1, w3, w2):
    rtot = xs.shape[0]
    n_mt = rtot // TMX
    grid_spec = pltpu.PrefetchScalarGridSpec(
        num_scalar_prefetch=2, grid=(n_mt,),
        in_specs=[pl.BlockSpec((TMX, D), lambda m, te, na: (jnp.minimum(m, na[0] - 1), 0)),
                  pl.BlockSpec((None, None, D, FF), lambda m, te, na: (layer, te[m], 0, 0)),
                  pl.BlockSpec((None, None, D, FF), lambda m, te, na: (layer, te[m], 0, 0)),
                  pl.BlockSpec((None, None, FF, D), lambda m, te, na: (layer, te[m], 0, 0))],
        out_specs=pl.BlockSpec((TMX, D), lambda m, te, na: (m, 0)),
        scratch_shapes=[pltpu.VMEM((D, FF), BF16), pltpu.VMEM((D, FF), BF16), pltpu.VMEM((FF, D), BF16)])
    return pl.pallas_call(
        _expert_kernel, grid_spec=grid_spec,
        out_shape=jax.ShapeDtypeStruct((rtot, D), BF16),
        compiler_params=pltpu.CompilerParams(dimension_semantics=("arbitrary",), vmem_limit_bytes=VMEM_LIMIT),
        name="experts",
    )(plan["texp"], plan["nact"], xs, w1, w3, w2)


def _combine_kernel(split, bidx_ref, pc_ref, lo_ref, dst_ref, ts_ref, info_ref, ys_ref, x1_ref, mod_ref,
                    lng_ref, lnb_ref, *rest):
    del bidx_ref
    o_refs, (loc_s, sems) = rest[:-2], rest[-2:]
    i = pl.program_id(0)
    slot = i % 2

    def fetch(tile, buf):
        for e in range(NE):
            _run_copies(ys_ref, loc_s.at[buf], dst_ref[tile * NE + e], lo_ref[tile * NE + e],
                        pc_ref[tile * NE + e], sems.at[buf], RUN_BITS)

    @pl.when(i == 0)
    def _():
        loc_s[...] = jnp.zeros_like(loc_s)
        fetch(0, 0)

    @pl.when(i + 1 < pl.num_programs(0))
    def _():
        fetch(i + 1, 1 - slot)

    _wait_groups(loc_s.at[slot], ts_ref[i], sems.at[slot], TILE_BITS)
    info = info_ref[...]
    d0 = info[0:1].astype(jnp.int32)
    d1 = info[1:2].astype(jnp.int32)
    r_i = lax.broadcasted_iota(jnp.int32, (RL, TD), 0)
    wperm = (jnp.where(r_i == d0, info[2:3], 0.0) + jnp.where(r_i == d1, info[3:4], 0.0)).astype(BF16)
    moe = lax.dot_general(wperm, loc_s[slot].reshape(RL, D), (((0,), (0,)), ((), ())),
                          preferred_element_type=F32)
    gate = mod_ref[...][:, 2 * D:]
    out = _ln(ALPHA * x1_ref[...] + gate * moe) * lng_ref[...] + lnb_ref[...]
    if len(o_refs) == 1:
        o_refs[0][...] = out
    else:
        @pl.when(i < split)
        def _():
            o_refs[0][...] = out

        @pl.when(i >= split)
        def _():
            o_refs[1][...] = out


def _combine(bidx_d, plan, info, ys, x1, mod2, lng, lnb, out_rows):
    n = x1.shape[0]
    ntd = n // TD
    tok = pl.BlockSpec((TD, D), lambda i, *_: (i, 0))
    full = lambda a: pl.BlockSpec(a.shape, lambda i, *_: (0,) * a.ndim)
    if len(out_rows) == 1:
        split, out_specs = 0, [tok]
    else:
        split = out_rows[0] // TD
        out_specs = [pl.BlockSpec((TD, D), lambda i, *_: (jnp.minimum(i, split - 1), 0)),
                     pl.BlockSpec((TD, D), lambda i, *_: (jnp.maximum(i - split, 0), 0))]
    grid_spec = pltpu.PrefetchScalarGridSpec(
        num_scalar_prefetch=5, grid=(ntd,),
        in_specs=[pl.BlockSpec((8, TD), lambda i, *_: (0, i)),
                  pl.BlockSpec(memory_space=pl.ANY),
                  tok,
                  pl.BlockSpec((None, 1, 3 * D), lambda i, b, *_: (b[i], 0, 0)),
                  full(lng), full(lnb)],
        out_specs=out_specs,
        scratch_shapes=[pltpu.VMEM((2, RL // ALIGN, ALIGN, D), BF16), pltpu.SemaphoreType.DMA((2,))])
    return pl.pallas_call(
        functools.partial(_combine_kernel, split), grid_spec=grid_spec,
        out_shape=[jax.ShapeDtypeStruct((r, D), F32) for r in out_rows],
        compiler_params=pltpu.CompilerParams(dimension_semantics=("arbitrary",), vmem_limit_bytes=VMEM_LIMIT),
        name="combine",
    )(bidx_d, plan["pc"], plan["lo"], plan["dst"], plan["tsum"], info, ys, x1, mod2, lng, lnb)


def _moe(meta, h2, mask_t, gate_t, cnt, layer, w1, w3, w2, x1, mod2, lng, lnb, out_rows):
    n = h2.shape[0]
    ntd = n // TD
    rtot = -(-(ntd * RL) // TMX) * TMX + NE * TMX
    cnt = cnt[:, :, :TM // TD].transpose(0, 2, 1).reshape(ntd, NE).astype(jnp.int32)
    plan = _plan(cnt, rtot // TMX)
    xs, info = _dispatch(plan, mask_t, gate_t, h2, rtot)
    ys = _experts(plan, xs.reshape(rtot, D), layer, w1, w3, w2).reshape(rtot // ALIGN, ALIGN, D)
    return _combine(meta["bidx"], plan, info, ys, x1, mod2, lng, lnb, out_rows)


def _tile_meta(groups, rows):
    bidx, pblk, first, last, cf, cb = [], [], [], [], [], []
    row = 0
    tile = 0
    for (b, s) in groups:
        assert s % rows == 0
        per = s // rows
        for bi in range(b):
            for j in range(per):
                bidx.append(row + bi)
                pblk.append(j)
                first.append(1 if j == 0 else 0)
                last.append(1 if j == per - 1 else 0)
                cf.append(tile + j)
                cb.append(tile + per - 1 - j)
            tile += per
        row += b
    as_i32 = lambda v: jnp.asarray(np.asarray(v, np.int32))
    return dict(bidx=as_i32(bidx), pblk=as_i32(pblk), first=as_i32(first), last=as_i32(last),
                cf=as_i32(cf), cb=as_i32(cb))


def _rope_tables(s_max):
    inv = jnp.power(ROPE_THETA, -jnp.arange(0, HD, 2, dtype=F32) / HD)
    ang = jnp.arange(s_max, dtype=F32)[:, None] * inv[None, :]
    cos, sin = jnp.cos(ang), jnp.sin(ang)
    cos128 = jnp.tile(cos, (1, 4))
    sin128 = jnp.tile(jnp.concatenate([-sin, sin], axis=1), (1, 2))
    return cos128, sin128


def _block_diag(w):
    d, nb, c, f = w.shape
    eye = jnp.eye(nb, dtype=w.dtype)
    return jnp.einsum('dncf,nm->dncmf', w, eye).reshape(d, nb * c, nb * f)


def _forward(xs, cs, w_mod, b_mod, w_in, attn_sink, conv_w, conv_b, lru_w_r, lru_b_r, lru_w_i, lru_b_i,
             lru_lambda, sg_norm_g, sg_w, sg_b, mix_norm_g, w_out, ln_g, ln_b, router_w, router_bias,
             exp_w1, exp_w3, exp_w2):
    groups = [(x.shape[0], x.shape[1]) for x in xs]
    assert len(groups) == 2
    meta_p, meta_a, meta_r = (_tile_meta(groups, rows) for rows in (TP, TA, TR))
    meta_m, meta_d = _tile_meta(groups, TM), _tile_meta(groups, TD)
    x_parts = [xx.reshape(-1, D) for xx in xs]
    group_rows = [p.shape[0] for p in x_parts]
    c_all = jnp.concatenate(cs, axis=0)
    bt = c_all.shape[0]
    mods = _modulation(c_all, w_mod, b_mod).reshape(2 * DEPTH, bt, 1, 3 * D)
    cos_t, sin_t = _rope_tables(max(s for _, s in groups))
    rwt = router_w.T[_slot_order()].astype(BF16)
    rb = router_bias[_slot_order()].reshape(NE, 1)
    for l in range(DEPTH):
        q, kk, vv, gy, xr, u, vn = _pre_mixer(meta_p, x_parts, mods[2 * l], w_in[l].astype(BF16), cos_t, sin_t,
                                              sg_norm_g[l].reshape(1, SGW))
        oa = _attention(meta_a, attn_sink[l], q, kk, vv)
        w_gates = jnp.concatenate([_block_diag(0.5 * lru_w_r[l]), _block_diag(0.5 * lru_w_i[l])], axis=2)
        b_gates = jnp.concatenate([0.5 * lru_b_r[l], 0.5 * lru_b_i[l]], axis=1)
        hf, hb = _recurrent(meta_r, xr, conv_w[l], conv_b[l].reshape(1, RW), w_gates.astype(BF16), b_gates,
                            lru_lambda[l])
        sgb = jnp.repeat(sg_b[l].T, HD, axis=1)
        x1, h2, gate_t, mask_t, cnt = _merge(meta_m, x_parts, oa, gy, hf, hb, u, vn, mods[2 * l], mods[2 * l + 1],
                                             mix_norm_g[l].reshape(1, D), sg_w[l].astype(BF16), sgb,
                                             w_out[l].astype(BF16), ln_g[l, 0].reshape(1, D),
                                             ln_b[l, 0].reshape(1, D), rwt, rb)
        out_rows = group_rows if l == DEPTH - 1 else [sum(group_rows)]
        x_parts = _moe(meta_d, h2, mask_t, gate_t, cnt, l, exp_w1, exp_w3, exp_w2, x1, mods[2 * l + 1],
                       ln_g[l, 1].reshape(1, D), ln_b[l, 1].reshape(1, D), out_rows)
    return tuple(p.reshape(b, s, D) for p, (b, s) in zip(x_parts, groups))


def kernel(x_prompt, x_sample, c_prompt, c_sample, w_mod, b_mod, w_in, attn_sink, conv_w, conv_b, lru_w_r, lru_b_r,
           lru_w_i, lru_b_i, lru_lambda, sg_norm_g, sg_w, sg_b, mix_norm_g, w_out, ln_g, ln_b, router_w,
           router_bias, exp_w1, exp_w3, exp_w2):
    return _forward([x_prompt, x_sample], [c_prompt, c_sample], w_mod, b_mod, w_in, attn_sink, conv_w, conv_b,
                    lru_w_r, lru_b_r, lru_w_i, lru_b_i, lru_lambda, sg_norm_g, sg_w, sg_b, mix_norm_g, w_out,
                    ln_g, ln_b, router_w, router_bias, exp_w1, exp_w3, exp_w2)
```

```python
import functools

import numpy as np
import jax
import jax.numpy as jnp
from jax import lax
from jax.experimental import pallas as pl
from jax.experimental.pallas import tpu as pltpu

F32 = jnp.float32
BF16 = jnp.bfloat16

D = 1024
DEPTH = 2
HD = 64
AW = 384
KVW = 128
RW = 384
SGW = 256
WINDOW = 128
CONV_WIDTH = 4
LRU_C = 8.0
CHUNK = 128
NE = 16
NG = 4
EPG = NE // NG
FF = 512
ALPHA = (2 * DEPTH) ** 0.25
LN_EPS = 1e-5
RMS_EPS = 1e-6
ROPE_THETA = 10000.0

TM = 1024
TP = 1024
TA = 2048
TR = 2048
QB = 128
SEG = TR // 8
NLG = RW // 128
TD = 512
ALIGN = 16
RL = 2 * TD + NE * ALIGN
TMX = 1024
ZROWS = 256
NEG = -1e30
VMEM_LIMIT = 48 * 1024 * 1024

C_Q, C_K, C_V, C_Y, C_R, C_U, C_SV, C_END = 0, 384, 512, 640, 1024, 1408, 1664, 1920


def _ln(x):
    mu = jnp.mean(x, axis=-1, keepdims=True)
    xc = x - mu
    var = jnp.mean(xc * xc, axis=-1, keepdims=True)
    return xc * lax.rsqrt(var + LN_EPS)


def _rms(x):
    return x * lax.rsqrt(jnp.mean(x * x, axis=-1, keepdims=True) + RMS_EPS)


def _sigmoid(z):
    return 0.5 * (jnp.tanh(0.5 * z) + 1.0)


def _gelu(x):
    return 0.5 * x * (1.0 + jnp.tanh(0.7978845608028654 * (x + 0.044715 * (x * x * x))))


def _split_bf16(a):
    hi = a.astype(BF16)
    lo = (a - hi.astype(F32)).astype(BF16)
    return hi, lo


def _mod_kernel(c_ref, w_ref, b_ref, o_ref):
    c = c_ref[...]
    ca = c * _sigmoid(c)
    ch, cl = _split_bf16(ca)
    wh, wl = _split_bf16(w_ref[...])
    acc = jnp.dot(ch, wh, preferred_element_type=F32)
    acc += jnp.dot(ch, wl, preferred_element_type=F32)
    acc += jnp.dot(cl, wh, preferred_element_type=F32)
    o_ref[...] = acc + b_ref[...]


def _modulation(c_all, w_mod, b_mod):
    bt = c_all.shape[0]
    cb = 768
    w = w_mod.reshape(2 * DEPTH, D, 3 * D)
    b = b_mod.reshape(2 * DEPTH, 1, 3 * D)
    return pl.pallas_call(
        _mod_kernel,
        grid=(2 * DEPTH, 3 * D // cb),
        in_specs=[pl.BlockSpec((bt, D), lambda s, j: (0, 0)),
                  pl.BlockSpec((None, D, cb), lambda s, j: (s, 0, j)),
                  pl.BlockSpec((None, 1, cb), lambda s, j: (s, 0, j))],
        out_specs=pl.BlockSpec((None, bt, cb), lambda s, j: (s, 0, j)),
        out_shape=jax.ShapeDtypeStruct((2 * DEPTH, bt, 3 * D), F32),
        name="modulation",
    )(c_all, w, b)


def _load_tokens(i, x_refs, split):
    if len(x_refs) == 1:
        return x_refs[0][...]
    return jnp.where(i < split, x_refs[0][...], x_refs[1][...])


def _token_specs(x_parts, rows):
    if len(x_parts) == 1:
        return [pl.BlockSpec((rows, D), lambda i, *_: (i, 0))], 0
    split = x_parts[0].shape[0] // rows
    return [pl.BlockSpec((rows, D), lambda i, *_: (jnp.minimum(i, split - 1), 0)),
            pl.BlockSpec((rows, D), lambda i, *_: (jnp.maximum(i - split, 0), 0))], split


def _pre_kernel(split, bidx_ref, pblk_ref, *refs):
    del bidx_ref, pblk_ref
    nx = len(refs) - 12
    x_refs = refs[:nx]
    mod_ref, w_ref, cos_ref, sin_ref, sgg_ref, q_ref, kk_ref, vv_ref, gy_ref, xr_ref, u_ref, vn_ref = refs[nx:]
    mod = mod_ref[...]
    shift, scale = mod[:, :D], mod[:, D:2 * D]
    h = (_ln(_load_tokens(pl.program_id(0), x_refs, split)) * (1.0 + scale) + shift).astype(BF16)
    cos = cos_ref[...]
    sin = sin_ref[...]
    lane = lax.broadcasted_iota(jnp.int32, (1, 128), 1)
    first_half = (lane % HD) < (HD // 2)

    def rope128(xg):
        rot = jnp.where(first_half, pltpu.roll(xg, 128 - HD // 2, 1), pltpu.roll(xg, HD // 2, 1))
        return xg * cos + rot * sin

    qk = jnp.dot(h, w_ref[:, C_Q:C_V], preferred_element_type=F32)
    for g in range(AW // 128):
        q_ref[:, g * 128:(g + 1) * 128] = (rope128(qk[:, g * 128:(g + 1) * 128]) * (HD ** -0.5)).astype(BF16)
    vg = jnp.dot(h, w_ref[:, C_V:C_R], preferred_element_type=F32)
    lo_half = lane < HD
    for src, dst_ref in ((rope128(qk[:, C_K:C_V]), kk_ref), (vg[:, :KVW], vv_ref)):
        swapped = pltpu.roll(src, HD, 1)
        dst_ref[:, 0:128] = jnp.where(lo_half, src, swapped).astype(BF16)
        dst_ref[:, 128:256] = jnp.where(lo_half, swapped, src).astype(BF16)
    gy_ref[...] = _gelu(vg[:, KVW:]).astype(gy_ref.dtype)
    rest = jnp.dot(h, w_ref[:, C_R:C_END], preferred_element_type=F32)
    for g in range(NLG):
        xr_ref[g] = rest[:, g * 128:(g + 1) * 128]
    u_ref[...] = _gelu(rest[:, C_U - C_R:C_SV - C_R]).astype(u_ref.dtype)
    sv = _gelu(rest[:, C_SV - C_R:])
    vn_ref[...] = (_ln(sv) * sgg_ref[...]).astype(BF16)


def _pre_mixer(meta, x_parts, mod_l, w_ext, cos_t, sin_t, sgg):
    n = sum(p.shape[0] for p in x_parts)
    nt = n // TP
    tok = lambda w: pl.BlockSpec((TP, w), lambda i, b, p: (i, 0))
    full = lambda a: pl.BlockSpec(a.shape, lambda i, b, p: (0,) * a.ndim)
    x_specs, split = _token_specs(x_parts, TP)
    grid_spec = pltpu.PrefetchScalarGridSpec(
        num_scalar_prefetch=2, grid=(nt,),
        in_specs=x_specs + [
            pl.BlockSpec((None, 1, 3 * D), lambda i, b, p: (b[i], 0, 0)),
            full(w_ext),
            pl.BlockSpec((TP, 128), lambda i, b, p: (p[i], 0)),
            pl.BlockSpec((TP, 128), lambda i, b, p: (p[i], 0)),
            full(sgg)],
        out_specs=[tok(AW), tok(256), tok(256), tok(RW),
                   pl.BlockSpec((NLG, TP, 128), lambda i, b, p: (0, i, 0)), tok(SGW), tok(SGW)])
    sds = lambda w, dt: jax.ShapeDtypeStruct((n, w), dt)
    return pl.pallas_call(
        functools.partial(_pre_kernel, split), grid_spec=grid_spec,
        out_shape=[sds(AW, BF16), sds(256, BF16), sds(256, BF16), sds(RW, BF16),
                   jax.ShapeDtypeStruct((NLG, n, 128), F32), sds(SGW, BF16), sds(SGW, BF16)],
        compiler_params=pltpu.CompilerParams(dimension_semantics=("parallel",), vmem_limit_bytes=VMEM_LIMIT),
        name="pre_mixer",
    )(meta["bidx"], meta["pblk"], *x_parts, mod_l, w_ext, cos_t, sin_t, sgg)


def _attn_kernel(first_ref, last_ref, sink_ref, q_ref, k_ref, kp_ref, kn_ref, v_ref, vp_ref, vn_ref,
                 o_ref, kw_s, vw_s):
    i = pl.program_id(0)
    is_first = first_ref[i] == 1
    is_last = last_ref[i] == 1
    kw_s[0:QB] = kp_ref[...]
    kw_s[QB:QB + TA] = k_ref[...]
    kw_s[QB + TA:] = kn_ref[...]
    vw_s[0:QB] = vp_ref[...]
    vw_s[QB:QB + TA] = v_ref[...]
    vw_s[QB + TA:] = vn_ref[...]
    iq = lax.broadcasted_iota(jnp.int32, (QB, 3 * QB), 0)
    ik = lax.broadcasted_iota(jnp.int32, (QB, 3 * QB), 1)
    rel = ik - iq
    band = (rel >= 0) & (rel <= 2 * WINDOW)
    lane = lax.broadcasted_iota(jnp.int32, (1, 128), 1)
    lo_half = lane < HD
    zero = jnp.zeros((), BF16)
    gsz = AW // HD // 2
    row_head = lax.broadcasted_iota(jnp.int32, (gsz * QB, 1), 0) // QB
    for j in range(TA // QB):
        ok = band
        if j == 0:
            ok = ok & ((ik >= QB) | jnp.logical_not(is_first))
        if j == TA // QB - 1:
            ok = ok & ((ik < 2 * QB) | jnp.logical_not(is_last))
        bias = jnp.where(ok, 0.0, NEG)
        bias = jnp.concatenate([bias] * gsz, axis=0)
        rows = slice(j * QB, (j + 1) * QB)
        keys = slice(j * QB, j * QB + 3 * QB)
        outs = [None] * (AW // HD)
        for g in range(2):
            heads = range(g * gsz, (g + 1) * gsz)
            qs = []
            for hh in heads:
                qg = q_ref[rows, (hh // 2) * 128:(hh // 2 + 1) * 128]
                qs.append(jnp.where(lo_half if hh % 2 == 0 else jnp.logical_not(lo_half), qg, zero))
            s = lax.dot_general(jnp.concatenate(qs, axis=0), kw_s[keys, g * 128:(g + 1) * 128],
                                (((1,), (1,)), ((), ())), preferred_element_type=F32) + bias
            sink = jnp.full((gsz * QB, 1), sink_ref[heads[-1]], F32)
            for t in range(gsz - 1):
                sink = jnp.where(row_head == t, sink_ref[heads[t]], sink)
            m = jnp.maximum(jnp.max(s, axis=-1, keepdims=True), sink)
            p = jnp.exp(s - m)
            vf = vw_s[keys, g * 128:(g + 1) * 128]
            pv = jnp.dot(p.astype(BF16), jnp.concatenate([vf, jnp.ones_like(vf)], axis=1),
                         preferred_element_type=F32)
            o = pv[:, :128] * (1.0 / (pv[:, 128:] + jnp.exp(sink - m)))
            for t, hh in enumerate(heads):
                outs[hh] = o[t * QB:(t + 1) * QB]
        for jg in range(AW // 128):
            o_ref[rows, jg * 128:(jg + 1) * 128] = jnp.where(lo_half, outs[2 * jg], outs[2 * jg + 1]).astype(o_ref.dtype)


def _attention(meta, sink, q, kk, vv):
    n = q.shape[0]
    nt = n // TA
    nqb = n // QB
    r = TA // QB
    main = lambda w: pl.BlockSpec((TA, w), lambda i, f, l, s: (i, 0))
    prev = pl.BlockSpec((QB, 256), lambda i, f, l, s: (jnp.maximum(i * r - 1, 0), 0))
    nxt = pl.BlockSpec((QB, 256), lambda i, f, l, s: (jnp.minimum(i * r + r, nqb - 1), 0))
    grid_spec = pltpu.PrefetchScalarGridSpec(
        num_scalar_prefetch=3, grid=(nt,),
        in_specs=[main(AW), main(256), prev, nxt, main(256), prev, nxt],
        out_specs=main(AW),
        scratch_shapes=[pltpu.VMEM((TA + 2 * QB, 256), BF16), pltpu.VMEM((TA + 2 * QB, 256), BF16)])
    return pl.pallas_call(
        _attn_kernel, grid_spec=grid_spec,
        out_shape=jax.ShapeDtypeStruct((n, AW), BF16),
        compiler_params=pltpu.CompilerParams(dimension_semantics=("parallel",), vmem_limit_bytes=VMEM_LIMIT),
        name="attention",
    )(meta["first"], meta["last"], sink, q, kk, kk, kk, vv, vv, vv)


def _rec_kernel(cf_ref, cb_ref, first_ref, last_ref,
                x_hbm, xfp_ref, xfn_ref, xbp_ref, xbn_ref,
                cw_ref, cbias_ref, wg_ref, bg_ref, lam_ref,
                hf_hbm, hb_hbm,
                ext_s, a_s, b_s, hs_s, ps_s, out_s, carry_s, gsem, ssem):
    i = pl.program_id(0)
    nsteps = pl.num_programs(0)
    slot = i % 2
    cw = cw_ref[...]
    cbias = cbias_ref[...]
    sub = lax.broadcasted_iota(jnp.int32, (8, 128), 0)
    unroll = 8
    chunk_refs = (cf_ref, cb_ref)
    out_hbm = (hf_hbm, hb_hbm)

    def gather(step, buf, d, start):
        c = chunk_refs[d][step]
        for g in range(NLG):
            for s in range(8):
                cp = pltpu.make_async_copy(x_hbm.at[g, pl.ds(pl.multiple_of(c * TR + s * SEG, SEG), SEG), :],
                                           ext_s.at[buf, d, g, pl.ds(2, SEG), s, :], gsem.at[buf, d])
                if start:
                    cp.start()
                else:
                    cp.wait()

    def scatter(d, c, start):
        for g in range(NLG):
            for s in range(8):
                cp = pltpu.make_async_copy(out_s.at[d, g, :, s, :],
                                           out_hbm[d].at[g, pl.ds(pl.multiple_of(c * TR + s * SEG, SEG), SEG), :],
                                           ssem.at[d])
                if start:
                    cp.start()
                else:
                    cp.wait()

    @pl.when(i == 0)
    def _():
        gather(0, 0, 0, True)
        gather(0, 0, 1, True)

    @pl.when(i + 1 < nsteps)
    def _():
        gather(i + 1, 1 - slot, 0, True)
        gather(i + 1, 1 - slot, 1, True)

    def run(d, xp_ref, xn_ref):
        c = chunk_refs[d][i]
        first = first_ref[c] == 1
        last = last_ref[c] == 1
        gather(i, slot, d, False)
        ext = ext_s.at[slot, d]
        planes = []
        for g in range(NLG):
            prev = jnp.where(first, 0.0, xp_ref[g])
            nxt = jnp.where(last, 0.0, xn_ref[g])
            ext[g, 0] = jnp.where(sub == 0, prev[6:7], pltpu.roll(ext[g, SEG], 1, 0))
            ext[g, 1] = jnp.where(sub == 0, prev[7:8], pltpu.roll(ext[g, SEG + 1], 1, 0))
            ext[g, SEG + 2] = jnp.where(sub == 7, nxt[0:1], pltpu.roll(ext[g, 2], 7, 0))
            lanes = slice(g * 128, (g + 1) * 128)
            acc = ext[g, 0:SEG] * cw[0:1, lanes]
            for t in range(1, CONV_WIDTH):
                acc = acc + ext[g, t:t + SEG] * cw[t:t + 1, lanes]
            planes.append(acc.reshape(TR, 128))
        xc = jnp.concatenate(planes, axis=1) + cbias
        xb16 = xc.astype(BF16)
        tg = jnp.tanh(jnp.dot(xb16, wg_ref[d], preferred_element_type=F32) + bg_ref[d:d + 1])
        tr, ti = tg[:, :RW], tg[:, RW:]
        nlam = -lam_ref[d:d + 1]
        softplus = jnp.maximum(nlam, 0.0) + jnp.log1p(jnp.exp(-jnp.abs(nlam)))
        half_c = (-0.5 * LRU_C) * softplus
        log_a = half_c * tr + half_c
        a = jnp.exp(log_a)
        v = (1.0 + a * a) * jnp.tanh(-log_a)
        b = jnp.where(v > 0.0, v * lax.rsqrt(v), 0.0) * (ti + 1.0) * (0.5 * xc)
        for g in range(NLG):
            a_s[g] = a[:, g * 128:(g + 1) * 128].reshape(SEG, 8, 128)
            b_s[g] = b[:, g * 128:(g + 1) * 128].reshape(SEG, 8, 128)

        reverse = d == 1
        reset = last if reverse else first

        @pl.when(reset)
        def _():
            carry_s[d] = jnp.zeros((NLG, 128), F32)

        def step(jo, hp):
            hp = list(hp)
            for ji in range(unroll):
                jj = jo * unroll + ji
                j = SEG - 1 - jj if reverse else jj
                for g in range(NLG):
                    h, p = hp[g]
                    ag = a_s[g, j]
                    h = ag * h + b_s[g, j]
                    p = p * ag
                    hs_s[g, j] = h
                    ps_s[g, j] = p
                    hp[g] = (h, p)
            return tuple(hp)

        init = tuple((jnp.zeros((8, 128), F32), jnp.ones((8, 128), F32)) for _ in range(NLG))
        ends = lax.fori_loop(0, SEG // unroll, step, init)
        cmats = []
        for g in range(NLG):
            e, pe = ends[g]
            c_in = carry_s[d, g:g + 1]
            rows = [None] * 8
            for s in (range(7, -1, -1) if reverse else range(8)):
                rows[s] = c_in
                c_in = e[s:s + 1] + pe[s:s + 1] * c_in
            carry_s[d, g:g + 1] = c_in
            cmats.append(jnp.concatenate(rows, axis=0))

        @pl.when(i > 0)
        def _():
            scatter(d, 0, False)

        def fix(jo, carry):
            for ji in range(unroll):
                j = jo * unroll + ji
                for g in range(NLG):
                    out_s[d, g, j] = hs_s[g, j] + ps_s[g, j] * cmats[g]
            return carry

        lax.fori_loop(0, SEG // unroll, fix, 0)
        scatter(d, c, True)

    run(0, xfp_ref, xfn_ref)
    run(1, xbp_ref, xbn_ref)

    @pl.when(i == nsteps - 1)
    def _():
        scatter(0, 0, False)
        scatter(1, 0, False)


def _recurrent(meta, xr, cw, cbias, wg, bg, lam):
    n = xr.shape[1]
    nt = n // TR
    n8 = n // 8
    r8 = TR // 8

    def halos(which):
        sel = (lambda cf, cb: cf) if which == 0 else (lambda cf, cb: cb)
        prev = pl.BlockSpec((NLG, 8, 128),
                            lambda i, cf, cb, f, l: (0, jnp.maximum(sel(cf, cb)[i] * r8 - 1, 0), 0))
        nxt = pl.BlockSpec((NLG, 8, 128),
                           lambda i, cf, cb, f, l: (0, jnp.minimum(sel(cf, cb)[i] * r8 + r8, n8 - 1), 0))
        return prev, nxt

    full = lambda a: pl.BlockSpec(a.shape, lambda i, cf, cb, f, l: (0,) * a.ndim)
    anywhere = pl.BlockSpec(memory_space=pl.ANY)
    plane = lambda lead: pltpu.VMEM(lead + (NLG, SEG, 8, 128), F32)
    grid_spec = pltpu.PrefetchScalarGridSpec(
        num_scalar_prefetch=4, grid=(nt,),
        in_specs=[anywhere, *halos(0), *halos(1),
                  full(cw), full(cbias), full(wg), full(bg), full(lam)],
        out_specs=[anywhere, anywhere],
        scratch_shapes=[pltpu.VMEM((2, 2, NLG, SEG + 3, 8, 128), F32),
                        plane(()), plane(()), plane(()), plane(()), plane((2,)),
                        pltpu.VMEM((2, NLG, 128), F32),
                        pltpu.SemaphoreType.DMA((2, 2)), pltpu.SemaphoreType.DMA((2,))])
    return pl.pallas_call(
        _rec_kernel, grid_spec=grid_spec,
        out_shape=[jax.ShapeDtypeStruct((NLG, n, 128), F32), jax.ShapeDtypeStruct((NLG, n, 128), F32)],
        compiler_params=pltpu.CompilerParams(dimension_semantics=("arbitrary",), vmem_limit_bytes=VMEM_LIMIT),
        name="recurrent",
    )(meta["cf"], meta["cb"], meta["first"], meta["last"], xr, xr, xr, xr, xr, cw, cbias, wg, bg, lam)


def _route(sel, score):
    def before(vk, vj, k, j):
        return (vk > vj) | ((vk == vj) & (k < j)) if k < j else (vk > vj)

    def count(flags):
        return sum(jnp.where(f, 1.0, 0.0) for f in flags)

    slot = lambda a, j: a[j * NG:(j + 1) * NG]
    cand = [slot(sel, j) for j in range(EPG)]
    in_top = [jnp.where(count(before(cand[k], cand[j], k, j) for k in range(EPG) if k != j) < 2.0, 1.0, 0.0)
              for j in range(EPG)]
    gscore = sum(in_top[j] * cand[j] for j in range(EPG))
    grow = [gscore[g:g + 1] for g in range(NG)]
    best = jnp.concatenate(
        [jnp.where(count(before(grow[k], grow[g], k, g) for k in range(NG) if k != g) < 1.0, 1.0, 0.0)
         for g in range(NG)], axis=0)
    picked = [in_top[j] * best * slot(score, j) for j in range(EPG)]
    inv = 1.0 / sum(jnp.sum(p, axis=0, keepdims=True) for p in picked)
    mask = jnp.concatenate([in_top[j] * best for j in range(EPG)], axis=0)
    gate = jnp.concatenate([p * inv for p in picked], axis=0)
    return mask, gate


def _slot_order():
    return np.asarray([g * EPG + j for j in range(EPG) for g in range(NG)], np.int32)


def _merge_kernel(split, bidx_ref, *refs):
    del bidx_ref
    nx = len(refs) - 22
    x_refs = refs[:nx]
    (oa_ref, gy_ref, hf_ref, hb_ref, u_ref, vn_ref, mod1_ref, mod2_ref, gmix_ref, sgw_ref, sgb_ref, wout_ref,
     lng_ref, lnb_ref, rwt_ref, rb_ref, x1_ref, h2_ref, gate_ref, mask_ref, cnt_ref, mrg_s) = refs[nx:]
    gmix = gmix_ref[...]
    mrg_s[:, 0:AW] = (_rms(oa_ref[...].astype(F32)) * gmix[:, 0:AW]).astype(BF16)
    hsum = jnp.concatenate([hf_ref[g] + hb_ref[g] for g in range(NLG)], axis=1)
    o_rec = gy_ref[...].astype(F32) * hsum
    mrg_s[:, AW:AW + RW] = (_rms(o_rec) * gmix[:, AW:AW + RW]).astype(BF16)
    lane = lax.broadcasted_iota(jnp.int32, (1, 128), 1)
    lo_half = lane < HD
    zero = jnp.zeros((), BF16)
    pieces = []
    for c in range(TM // CHUNK):
        rows = slice(c * CHUNK, (c + 1) * CHUNK)
        grp = []
        for g in range(SGW // 128):
            vg = vn_ref[rows, g * 128:(g + 1) * 128]
            mixed = jnp.dot(sgw_ref[2 * g], jnp.where(lo_half, vg, zero), preferred_element_type=F32)
            mixed += jnp.dot(sgw_ref[2 * g + 1], jnp.where(lo_half, zero, vg), preferred_element_type=F32)
            grp.append(mixed)
        mixed = jnp.concatenate(grp, axis=1) + sgb_ref[...]
        pieces.append(u_ref[rows, :].astype(F32) * mixed)
    o_sg = jnp.concatenate(pieces, axis=0)
    mrg_s[:, AW + RW:] = (_rms(o_sg) * gmix[:, AW + RW:]).astype(BF16)
    o = jnp.dot(mrg_s[...], wout_ref[...], preferred_element_type=F32)
    gate1 = mod1_ref[...][:, 2 * D:]
    x_in = _load_tokens(pl.program_id(0), x_refs, split)
    x1 = _ln(ALPHA * x_in + gate1 * o) * lng_ref[...] + lnb_ref[...]
    x1_ref[...] = x1
    mod2 = mod2_ref[...]
    h2 = (_ln(x1) * (1.0 + mod2[:, D:2 * D]) + mod2[:, :D]).astype(BF16)
    h2_ref[...] = h2
    logits = lax.dot_general(rwt_ref[...], h2, (((1,), (1,)), ((), ())), preferred_element_type=F32)
    score = _sigmoid(logits)
    sel = score + rb_ref[...]
    mask_t, gate_t = _route(sel, score)
    gate_ref[...] = gate_t
    mask_ref[...] = mask_t
    t_i = lax.broadcasted_iota(jnp.int32, (TM, 128), 0)
    j_i = lax.broadcasted_iota(jnp.int32, (TM, 128), 1)
    in_tile = jnp.where((t_i >= j_i * TD) & (t_i < (j_i + 1) * TD), 1.0, 0.0).astype(BF16)
    cnt_ref[...] = jnp.dot(mask_t.astype(BF16), in_tile, preferred_element_type=F32)


def _merge(meta, x_parts, oa, gy, hf, hb, u, vn, mod1, mod2, gmix, sgw, sgb, wout, lng, lnb, rwt, rb):
    n = oa.shape[0]
    nt = n // TM
    x_specs, split = _token_specs(x_parts, TM)
    tok = lambda w: pl.BlockSpec((TM, w), lambda i, b: (i, 0))
    full = lambda a: pl.BlockSpec(a.shape, lambda i, b: (0,) * a.ndim)
    modspec = pl.BlockSpec((None, 1, 3 * D), lambda i, b: (b[i], 0, 0))
    tspec = pl.BlockSpec((NE, TM), lambda i, b: (0, i))
    rec = pl.BlockSpec((NLG, TM, 128), lambda i, b: (0, i, 0))
    grid_spec = pltpu.PrefetchScalarGridSpec(
        num_scalar_prefetch=1, grid=(nt,),
        in_specs=x_specs + [
            tok(AW), tok(RW), rec, rec, tok(SGW), tok(SGW), modspec, modspec,
            full(gmix), full(sgw), full(sgb), full(wout), full(lng), full(lnb), full(rwt), full(rb)],
        out_specs=[tok(D), tok(D), tspec, tspec, pl.BlockSpec((None, NE, 128), lambda i, b: (i, 0, 0))],
        scratch_shapes=[pltpu.VMEM((TM, D), BF16)])
    return pl.pallas_call(
        functools.partial(_merge_kernel, split), grid_spec=grid_spec,
        out_shape=[jax.ShapeDtypeStruct((n, D), F32), jax.ShapeDtypeStruct((n, D), BF16),
                   jax.ShapeDtypeStruct((NE, n), F32), jax.ShapeDtypeStruct((NE, n), F32),
                   jax.ShapeDtypeStruct((nt, NE, 128), F32)],
        compiler_params=pltpu.CompilerParams(dimension_semantics=("parallel",), vmem_limit_bytes=VMEM_LIMIT),
        name="merge_route",
    )(meta["bidx"], *x_parts, oa, gy, hf, hb, u, vn, mod1, mod2, gmix, sgw, sgb, wout, lng, lnb, rwt, rb)


def _plan(cnt, n_mt):
    pc = (cnt + (ALIGN - 1)) // ALIGN * ALIGN
    lo = jnp.cumsum(pc, axis=1) - pc
    tot = jnp.sum(pc, axis=0)
    seg = (tot + (TMX - 1)) // TMX * TMX
    gend = jnp.cumsum(seg)
    gstart = gend - seg
    dst = gstart[None, :] + jnp.cumsum(pc, axis=0) - pc
    tile_row = jnp.arange(n_mt, dtype=jnp.int32) * TMX
    texp = jnp.minimum(jnp.sum((gend[None, :] <= tile_row[:, None]).astype(jnp.int32), axis=1), NE - 1)
    texp = jnp.asarray(_slot_order())[texp]
    nact = (gend[-1] // TMX).reshape(1)
    grp = lambda a: (a // ALIGN).astype(jnp.int32)
    return dict(pc=grp(pc).reshape(-1), lo=grp(lo).reshape(-1), dst=grp(dst).reshape(-1),
                tsum=grp(jnp.sum(pc, axis=1)),
                zdst=grp(jnp.concatenate([gstart + tot, gend[-1:]])), zlen=grp(seg - tot),
                texp=texp.astype(jnp.int32), nact=nact.astype(jnp.int32))


def _run_copies(src, dst, src_off, dst_off, groups, sem, max_bits):
    def arm(b):
        @pl.when(((groups >> b) & 1) == 1)
        def _():
            off = (groups >> (b + 1)) << (b + 1)
            pltpu.make_async_copy(src.at[pl.ds(src_off + off, 1 << b)], dst.at[pl.ds(dst_off + off, 1 << b)],
                                  sem).start()

    for b in range(min(COMMON_BITS, max_bits)):
        arm(b)
    if max_bits > COMMON_BITS:
        @pl.when(groups >= (1 << COMMON_BITS))
        def _():
            for b in range(COMMON_BITS, max_bits):
                arm(b)


def _wait_groups(buf, groups, sem, max_bits):
    for b in range(max_bits):
        @pl.when(((groups >> b) & 1) == 1)
        def _():
            pltpu.make_async_copy(buf.at[pl.ds(0, 1 << b)], buf.at[pl.ds(0, 1 << b)], sem).wait()


COMMON_BITS = 3
RUN_BITS = (TD // ALIGN).bit_length()
TILE_BITS = (RL // ALIGN).bit_length()


def _dispatch_kernel(pc_ref, lo_ref, dst_ref, ts_ref, zd_ref, zl_ref, mask_ref, gate_ref, h_ref,
                     xs_ref, info_ref, loc_s, zero_s, earlier_s, sems):
    i = pl.program_id(0)
    last_step = pl.num_programs(0) - 1
    slot = i % 2
    mask = mask_ref[...]
    mb = mask.astype(BF16)
    @pl.when(i == 0)
    def _():
        s_i = lax.broadcasted_iota(jnp.int32, (TD, TD), 0)
        t_i = lax.broadcasted_iota(jnp.int32, (TD, TD), 1)
        earlier_s[...] = jnp.where(s_i < t_i, 1.0, 0.0).astype(BF16)

    rank = jnp.dot(mb, earlier_s[...], preferred_element_type=F32)
    e_i = lax.broadcasted_iota(jnp.int32, (NE, NE), 0)
    f_i = lax.broadcasted_iota(jnp.int32, (NE, NE), 1)
    below = jnp.where(f_i < e_i, 1.0, 0.0).astype(BF16)
    lower = jnp.dot(below, mb, preferred_element_type=F32)
    row_e = lax.broadcasted_iota(jnp.int32, (NE, 1), 0)
    lo_vec = jnp.zeros((NE, 1), F32)
    for e in range(NE):
        lo_vec = jnp.where(row_e == e, (lo_ref[i * NE + e] * ALIGN).astype(F32), lo_vec)
    row = lo_vec + rank
    is0 = mask * jnp.where(lower == 0.0, 1.0, 0.0)
    is1 = mask - is0
    d0 = jnp.sum(is0 * row, axis=0, keepdims=True)
    d1 = jnp.sum(is1 * row, axis=0, keepdims=True)
    gate = gate_ref[...]
    w0 = jnp.sum(is0 * gate, axis=0, keepdims=True)
    w1 = jnp.sum(is1 * gate, axis=0, keepdims=True)
    info_ref[...] = jnp.concatenate([d0, d1, w0, w1, jnp.zeros((4, TD), F32)], axis=0)
    r_i = lax.broadcasted_iota(jnp.int32, (RL, TD), 0)
    perm = jnp.where((r_i == d0.astype(jnp.int32)) | (r_i == d1.astype(jnp.int32)), 1.0, 0.0).astype(BF16)
    loc = loc_s.at[slot]
    loc[...] = jnp.dot(perm, h_ref[...], preferred_element_type=F32).astype(BF16).reshape(RL // ALIGN, ALIGN, D)
    for e in range(NE):
        _run_copies(loc, xs_ref, lo_ref[i * NE + e], dst_ref[i * NE + e], pc_ref[i * NE + e], sems.at[slot],
                    RUN_BITS)

    @pl.when(i > 0)
    def _():
        _wait_groups(loc_s.at[1 - slot], ts_ref[jnp.maximum(i - 1, 0)], sems.at[1 - slot], TILE_BITS)

    @pl.when(i == last_step)
    def _():
        _wait_groups(loc, ts_ref[i], sems.at[slot], TILE_BITS)
        sem = sems.at[0]
        zero_s[...] = jnp.zeros_like(zero_s)
        zg = ZROWS // ALIGN
        for wait in (False, True):
            for e in range(NE):
                for part in range(TMX // ZROWS):
                    groups = jnp.clip(zl_ref[e] - part * zg, 0, zg)
                    if wait:
                        _wait_groups(zero_s, groups, sem, zg.bit_length())
                    else:
                        _run_copies(zero_s, xs_ref, 0, zd_ref[e] + part * zg, groups, sem, zg.bit_length())
        tail = zd_ref[NE]
        chunks = (xs_ref.shape[0] - tail) // zg

        def fill(c, carry):
            pltpu.make_async_copy(zero_s, xs_ref.at[pl.ds(tail + c * zg, zg)], sem).start()
            return carry

        def drain(c, carry):
            pltpu.make_async_copy(zero_s, zero_s, sem).wait()
            return carry

        lax.fori_loop(0, chunks, fill, 0)
        lax.fori_loop(0, chunks, drain, 0)


def _dispatch(plan, mask_t, gate_t, h2, rtot):
    n = h2.shape[0]
    ntd = n // TD
    tspec = pl.BlockSpec((NE, TD), lambda i, *_: (0, i))
    grid_spec = pltpu.PrefetchScalarGridSpec(
        num_scalar_prefetch=6, grid=(ntd,),
        in_specs=[tspec, tspec, pl.BlockSpec((TD, D), lambda i, *_: (i, 0))],
        out_specs=[pl.BlockSpec(memory_space=pl.ANY), pl.BlockSpec((8, TD), lambda i, *_: (0, i))],
        scratch_shapes=[pltpu.VMEM((2, RL // ALIGN, ALIGN, D), BF16), pltpu.VMEM((ZROWS // ALIGN, ALIGN, D), BF16),
                        pltpu.VMEM((TD, TD), BF16), pltpu.SemaphoreType.DMA((2,))])
    return pl.pallas_call(
        _dispatch_kernel, grid_spec=grid_spec,
        out_shape=[jax.ShapeDtypeStruct((rtot // ALIGN, ALIGN, D), BF16), jax.ShapeDtypeStruct((8, n), F32)],
        compiler_params=pltpu.CompilerParams(dimension_semantics=("arbitrary",), vmem_limit_bytes=VMEM_LIMIT),
        name="dispatch",
    )(plan["pc"], plan["lo"], plan["dst"], plan["tsum"], plan["zdst"], plan["zlen"], mask_t, gate_t, h2)


def _expert_kernel(texp_ref, nact_ref, x_ref, w1_ref, w3_ref, w2_ref, y_ref, w1_s, w3_s, w2_s):
    m = pl.program_id(0)
    new_expert = (m == 0) | (texp_ref[m] != texp_ref[jnp.maximum(m - 1, 0)])

    @pl.when(new_expert)
    def _():
        w1_s[...] = w1_ref[...].astype(BF16)
        w3_s[...] = w3_ref[...].astype(BF16)
        w2_s[...] = w2_ref[...].astype(BF16)

    @pl.when(m < nact_ref[0])
    def _():
        x = x_ref[...]
        a = jnp.dot(x, w1_s[...], preferred_element_type=F32)
        a = a * _sigmoid(a) * jnp.dot(x, w3_s[...], preferred_element_type=F32)
        y_ref[...] = jnp.dot(a.astype(BF16), w2_s[...], preferred_element_type=F32).astype(BF16)

    @pl.when(m >= nact_ref[0])
    def _():
        y_ref[...] = jnp.zeros_like(y_ref)


def _experts(plan, xs, layer, w1, w3, w2):
    rtot = xs.shape[0]
    n_mt = rtot // TMX
    grid_spec = pltpu.PrefetchScalarGridSpec(
        num_scalar_prefetch=2, grid=(n_mt,),
        in_specs=[pl.BlockSpec((TMX, D), lambda m, te, na: (jnp.minimum(m, na[0] - 1), 0)),
                  pl.BlockSpec((None, None, D, FF), lambda m, te, na: (layer, te[m], 0, 0)),
                  pl.BlockSpec((None, None, D, FF), lambda m, te, na: (layer, te[m], 0, 0)),
                  pl.BlockSpec((None, None, FF, D), lambda m, te, na: (layer, te[m], 0, 0))],
        out_specs=pl.BlockSpec((TMX, D), lambda m, te, na: (m, 0)),
        scratch_shapes=[pltpu.VMEM((D, FF), BF16), pltpu.VMEM((D, FF), BF16), pltpu.VMEM((FF, D), BF16)])
    return pl.pallas_call(
        _expert_kernel, grid_spec=grid_spec,
        out_shape=jax.ShapeDtypeStruct((rtot, D), BF16),
        compiler_params=pltpu.CompilerParams(dimension_semantics=("arbitrary",), vmem_limit_bytes=VMEM_LIMIT),
        name="experts",
    )(plan["texp"], plan["nact"], xs, w1, w3, w2)


def _combine_kernel(split, bidx_ref, pc_ref, lo_ref, dst_ref, ts_ref, info_ref, ys_ref, x1_ref, mod_ref,
                    lng_ref, lnb_ref, *rest):
    del bidx_ref
    o_refs, (loc_s, sems) = rest[:-2], rest[-2:]
    i = pl.program_id(0)
    slot = i % 2

    def fetch(tile, buf):
        for e in range(NE):
            _run_copies(ys_ref, loc_s.at[buf], dst_ref[tile * NE + e], lo_ref[tile * NE + e],
                        pc_ref[tile * NE + e], sems.at[buf], RUN_BITS)

    @pl.when(i == 0)
    def _():
        loc_s[...] = jnp.zeros_like(loc_s)
        fetch(0, 0)

    @pl.when(i + 1 < pl.num_programs(0))
    def _():
        fetch(i + 1, 1 - slot)

    _wait_groups(loc_s.at[slot], ts_ref[i], sems.at[slot], TILE_BITS)
    info = info_ref[...]
    d0 = info[0:1].astype(jnp.int32)
    d1 = info[1:2].astype(jnp.int32)
    r_i = lax.broadcasted_iota(jnp.int32, (RL, TD), 0)
    wperm = (jnp.where(r_i == d0, info[2:3], 0.0) + jnp.where(r_i == d1, info[3:4], 0.0)).astype(BF16)
    moe = lax.dot_general(wperm, loc_s[slot].reshape(RL, D), (((0,), (0,)), ((), ())),
                          preferred_element_type=F32)
    gate = mod_ref[...][:, 2 * D:]
    out = _ln(ALPHA * x1_ref[...] + gate * moe) * lng_ref[...] + lnb_ref[...]
    if len(o_refs) == 1:
        o_refs[0][...] = out
    else:
        @pl.when(i < split)
        def _():
            o_refs[0][...] = out

        @pl.when(i >= split)
        def _():
            o_refs[1][...] = out


def _combine(bidx_d, plan, info, ys, x1, mod2, lng, lnb, out_rows):
    n = x1.shape[0]
    ntd = n // TD
    tok = pl.BlockSpec((TD, D), lambda i, *_: (i, 0))
    full = lambda a: pl.BlockSpec(a.shape, lambda i, *_: (0,) * a.ndim)
    if len(out_rows) == 1:
        split, out_specs = 0, [tok]
    else:
        split = out_rows[0] // TD
        out_specs = [pl.BlockSpec((TD, D), lambda i, *_: (jnp.minimum(i, split - 1), 0)),
                     pl.BlockSpec((TD, D), lambda i, *_: (jnp.maximum(i - split, 0), 0))]
    grid_spec = pltpu.PrefetchScalarGridSpec(
        num_scalar_prefetch=5, grid=(ntd,),
        in_specs=[pl.BlockSpec((8, TD), lambda i, *_: (0, i)),
                  pl.BlockSpec(memory_space=pl.ANY),
                  tok,
                  pl.BlockSpec((None, 1, 3 * D), lambda i, b, *_: (b[i], 0, 0)),
                  full(lng), full(lnb)],
        out_specs=out_specs,
        scratch_shapes=[pltpu.VMEM((2, RL // ALIGN, ALIGN, D), BF16), pltpu.SemaphoreType.DMA((2,))])
    return pl.pallas_call(
        functools.partial(_combine_kernel, split), grid_spec=grid_spec,
        out_shape=[jax.ShapeDtypeStruct((r, D), F32) for r in out_rows],
        compiler_params=pltpu.CompilerParams(dimension_semantics=("arbitrary",), vmem_limit_bytes=VMEM_LIMIT),
        name="combine",
    )(bidx_d, plan["pc"], plan["lo"], plan["dst"], plan["tsum"], info, ys, x1, mod2, lng, lnb)


def _moe(meta, h2, mask_t, gate_t, cnt, layer, w1, w3, w2, x1, mod2, lng, lnb, out_rows):
    n = h2.shape[0]
    ntd = n // TD
    rtot = -(-(ntd * RL) // TMX) * TMX + NE * TMX
    cnt = cnt[:, :, :TM // TD].transpose(0, 2, 1).reshape(ntd, NE).astype(jnp.int32)
    plan = _plan(cnt, rtot // TMX)
    xs, info = _dispatch(plan, mask_t, gate_t, h2, rtot)
    ys = _experts(plan, xs.reshape(rtot, D), layer, w1, w3, w2).reshape(rtot // ALIGN, ALIGN, D)
    return _combine(meta["bidx"], plan, info, ys, x1, mod2, lng, lnb, out_rows)


def _tile_meta(groups, rows):
    bidx, pblk, first, last, cf, cb = [], [], [], [], [], []
    row = 0
    tile = 0
    for (b, s) in groups:
        assert s % rows == 0
        per = s // rows
        for bi in range(b):
            for j in range(per):
                bidx.append(row + bi)
                pblk.append(j)
                first.append(1 if j == 0 else 0)
                last.append(1 if j == per - 1 else 0)
                cf.append(tile + j)
                cb.append(tile + per - 1 - j)
            tile += per
        row += b
    as_i32 = lambda v: jnp.asarray(np.asarray(v, np.int32))
    return dict(bidx=as_i32(bidx), pblk=as_i32(pblk), first=as_i32(first), last=as_i32(last),
                cf=as_i32(cf), cb=as_i32(cb))


def _rope_tables(s_max):
    inv = jnp.power(ROPE_THETA, -jnp.arange(0, HD, 2, dtype=F32) / HD)
    ang = jnp.arange(s_max, dtype=F32)[:, None] * inv[None, :]
    cos, sin = jnp.cos(ang), jnp.sin(ang)
    cos128 = jnp.tile(cos, (1, 4))
    sin128 = jnp.tile(jnp.concatenate([-sin, sin], axis=1), (1, 2))
    return cos128, sin128


def _block_diag(w):
    d, nb, c, f = w.shape
    eye = jnp.eye(nb, dtype=w.dtype)
    return jnp.einsum('dncf,nm->dncmf', w, eye).reshape(d, nb * c, nb * f)


def _forward(xs, cs, w_mod, b_mod, w_in, attn_sink, conv_w, conv_b, lru_w_r, lru_b_r, lru_w_i, lru_b_i,
             lru_lambda, sg_norm_g, sg_w, sg_b, mix_norm_g, w_out, ln_g, ln_b, router_w, router_bias,
             exp_w1, exp_w3, exp_w2):
    groups = [(x.shape[0], x.shape[1]) for x in xs]
    assert len(groups) == 2
    meta_p, meta_a, meta_r = (_tile_meta(groups, rows) for rows in (TP, TA, TR))
    meta_m, meta_d = _tile_meta(groups, TM), _tile_meta(groups, TD)
    x_parts = [xx.reshape(-1, D) for xx in xs]
    group_rows = [p.shape[0] for p in x_parts]
    c_all = jnp.concatenate(cs, axis=0)
    bt = c_all.shape[0]
    mods = _modulation(c_all, w_mod, b_mod).reshape(2 * DEPTH, bt, 1, 3 * D)
    cos_t, sin_t = _rope_tables(max(s for _, s in groups))
    rwt = router_w.T[_slot_order()].astype(BF16)
    rb = router_bias[_slot_order()].reshape(NE, 1)
    for l in range(DEPTH):
        q, kk, vv, gy, xr, u, vn = _pre_mixer(meta_p, x_parts, mods[2 * l], w_in[l].astype(BF16), cos_t, sin_t,
                                              sg_norm_g[l].reshape(1, SGW))
        oa = _attention(meta_a, attn_sink[l], q, kk, vv)
        w_gates = jnp.concatenate([_block_diag(0.5 * lru_w_r[l]), _block_diag(0.5 * lru_w_i[l])], axis=2)
        b_gates = jnp.concatenate([0.5 * lru_b_r[l], 0.5 * lru_b_i[l]], axis=1)
        hf, hb = _recurrent(meta_r, xr, conv_w[l], conv_b[l].reshape(1, RW), w_gates.astype(BF16), b_gates,
                            lru_lambda[l])
        sgb = jnp.repeat(sg_b[l].T, HD, axis=1)
        x1, h2, gate_t, mask_t, cnt = _merge(meta_m, x_parts, oa, gy, hf, hb, u, vn, mods[2 * l], mods[2 * l + 1],
                                             mix_norm_g[l].reshape(1, D), sg_w[l].astype(BF16), sgb,
                                             w_out[l].astype(BF16), ln_g[l, 0].reshape(1, D),
                                             ln_b[l, 0].reshape(1, D), rwt, rb)
        out_rows = group_rows if l == DEPTH - 1 else [sum(group_rows)]
        x_parts = _moe(meta_d, h2, mask_t, gate_t, cnt, l, exp_w1, exp_w3, exp_w2, x1, mods[2 * l + 1],
                       ln_g[l, 1].reshape(1, D), ln_b[l, 1].reshape(1, D), out_rows)
    return tuple(p.reshape(b, s, D) for p, (b, s) in zip(x_parts, groups))


def kernel(x_prompt, x_sample, c_prompt, c_sample, w_mod, b_mod, w_in, attn_sink, conv_w, conv_b, lru_w_r, lru_b_r,
           lru_w_i, lru_b_i, lru_lambda, sg_norm_g, sg_w, sg_b, mix_norm_g, w_out, ln_g, ln_b, router_w,
           router_bias, exp_w1, exp_w3, exp_w2):
    return _forward([x_prompt, x_sample], [c_prompt, c_sample], w_mod, b_mod, w_in, attn_sink, conv_w, conv_b,
                    lru_w_r, lru_b_r, lru_w_i, lru_b_i, lru_lambda, sg_norm_g, sg_w, sg_b, mix_norm_g, w_out,
                    ln_g, ln_b, router_w, router_bias, exp_w1, exp_w3, exp_w2)
```

```python
import functools

import numpy as np
import jax
import jax.numpy as jnp
from jax import lax
from jax.experimental import pallas as pl
from jax.experimental.pallas import tpu as pltpu

F32 = jnp.float32
BF16 = jnp.bfloat16

D = 1024
DEPTH = 2
HD = 64
AW = 384
KVW = 128
RW = 384
SGW = 256
WINDOW = 128
CONV_WIDTH = 4
LRU_C = 8.0
CHUNK = 128
NE = 16
NG = 4
EPG = NE // NG
FF = 512
ALPHA = (2 * DEPTH) ** 0.25
LN_EPS = 1e-5
RMS_EPS = 1e-6
ROPE_THETA = 10000.0

TM = 1024
TP = 1024
TA = 2048
TR = 2048
QB = 128
SEG = TR // 8
NLG = RW // 128
TD = 512
ALIGN = 16
RL = 2 * TD + NE * ALIGN
TMX = 1024
ZROWS = 256
NEG = -1e30
VMEM_LIMIT = 48 * 1024 * 1024

C_Q, C_K, C_V, C_Y, C_R, C_U, C_SV, C_END = 0, 384, 512, 640, 1024, 1408, 1664, 1920


def _ln(x):
    mu = jnp.mean(x, axis=-1, keepdims=True)
    xc = x - mu
    var = jnp.mean(xc * xc, axis=-1, keepdims=True)
    return xc * lax.rsqrt(var + LN_EPS)


def _rms(x):
    return x * lax.rsqrt(jnp.mean(x * x, axis=-1, keepdims=True) + RMS_EPS)


def _sigmoid(z):
    return 0.5 * (jnp.tanh(0.5 * z) + 1.0)


def _gelu(x):
    return 0.5 * x * (1.0 + jnp.tanh(0.7978845608028654 * (x + 0.044715 * (x * x * x))))


def _split_bf16(a):
    hi = a.astype(BF16)
    lo = (a - hi.astype(F32)).astype(BF16)
    return hi, lo


def _mod_kernel(c_ref, w_ref, b_ref, o_ref):
    c = c_ref[...]
    ca = c * _sigmoid(c)
    ch, cl = _split_bf16(ca)
    wh, wl = _split_bf16(w_ref[...])
    acc = jnp.dot(ch, wh, preferred_element_type=F32)
    acc += jnp.dot(ch, wl, preferred_element_type=F32)
    acc += jnp.dot(cl, wh, preferred_element_type=F32)
    o_ref[...] = acc + b_ref[...]


def _modulation(c_all, w_mod, b_mod):
    bt = c_all.shape[0]
    cb = 768
    w = w_mod.reshape(2 * DEPTH, D, 3 * D)
    b = b_mod.reshape(2 * DEPTH, 1, 3 * D)
    return pl.pallas_call(
        _mod_kernel,
        grid=(2 * DEPTH, 3 * D // cb),
        in_specs=[pl.BlockSpec((bt, D), lambda s, j: (0, 0)),
                  pl.BlockSpec((None, D, cb), lambda s, j: (s, 0, j)),
                  pl.BlockSpec((None, 1, cb), lambda s, j: (s, 0, j))],
        out_specs=pl.BlockSpec((None, bt, cb), lambda s, j: (s, 0, j)),
        out_shape=jax.ShapeDtypeStruct((2 * DEPTH, bt, 3 * D), F32),
        name="modulation",
    )(c_all, w, b)


def _load_tokens(i, x_refs, split):
    if len(x_refs) == 1:
        return x_refs[0][...]
    return jnp.where(i < split, x_refs[0][...], x_refs[1][...])


def _token_specs(x_parts, rows):
    if len(x_parts) == 1:
        return [pl.BlockSpec((rows, D), lambda i, *_: (i, 0))], 0
    split = x_parts[0].shape[0] // rows
    return [pl.BlockSpec((rows, D), lambda i, *_: (jnp.minimum(i, split - 1), 0)),
            pl.BlockSpec((rows, D), lambda i, *_: (jnp.maximum(i - split, 0), 0))], split


def _pre_kernel(split, bidx_ref, pblk_ref, *refs):
    del bidx_ref, pblk_ref
    nx = len(refs) - 12
    x_refs = refs[:nx]
    mod_ref, w_ref, cos_ref, sin_ref, sgg_ref, q_ref, kk_ref, vv_ref, gy_ref, xr_ref, u_ref, vn_ref = refs[nx:]
    mod = mod_ref[...]
    shift, scale = mod[:, :D], mod[:, D:2 * D]
    h = (_ln(_load_tokens(pl.program_id(0), x_refs, split)) * (1.0 + scale) + shift).astype(BF16)
    cos = cos_ref[...]
    sin = sin_ref[...]
    lane = lax.broadcasted_iota(jnp.int32, (1, 128), 1)
    first_half = (lane % HD) < (HD // 2)

    def rope128(xg):
        rot = jnp.where(first_half, pltpu.roll(xg, 128 - HD // 2, 1), pltpu.roll(xg, HD // 2, 1))
        return xg * cos + rot * sin

    qk = jnp.dot(h, w_ref[:, C_Q:C_V], preferred_element_type=F32)
    for g in range(AW // 128):
        q_ref[:, g * 128:(g + 1) * 128] = (rope128(qk[:, g * 128:(g + 1) * 128]) * (HD ** -0.5)).astype(BF16)
    vg = jnp.dot(h, w_ref[:, C_V:C_R], preferred_element_type=F32)
    lo_half = lane < HD
    for src, dst_ref in ((rope128(qk[:, C_K:C_V]), kk_ref), (vg[:, :KVW], vv_ref)):
        swapped = pltpu.roll(src, HD, 1)
        dst_ref[:, 0:128] = jnp.where(lo_half, src, swapped).astype(BF16)
        dst_ref[:, 128:256] = jnp.where(lo_half, swapped, src).astype(BF16)
    gy_ref[...] = _gelu(vg[:, KVW:]).astype(gy_ref.dtype)
    rest = jnp.dot(h, w_ref[:, C_R:C_END], preferred_element_type=F32)
    for g in range(NLG):
        xr_ref[g] = rest[:, g * 128:(g + 1) * 128]
    u_ref[...] = _gelu(rest[:, C_U - C_R:C_SV - C_R]).astype(u_ref.dtype)
    sv = _gelu(rest[:, C_SV - C_R:])
    vn_ref[...] = (_ln(sv) * sgg_ref[...]).astype(BF16)


def _pre_mixer(meta, x_parts, mod_l, w_ext, cos_t, sin_t, sgg):
    n = sum(p.shape[0] for p in x_parts)
    nt = n // TP
    tok = lambda w: pl.BlockSpec((TP, w), lambda i, b, p: (i, 0))
    full = lambda a: pl.BlockSpec(a.shape, lambda i, b, p: (0,) * a.ndim)
    x_specs, split = _token_specs(x_parts, TP)
    grid_spec = pltpu.PrefetchScalarGridSpec(
        num_scalar_prefetch=2, grid=(nt,),
        in_specs=x_specs + [
            pl.BlockSpec((None, 1, 3 * D), lambda i, b, p: (b[i], 0, 0)),
            full(w_ext),
            pl.BlockSpec((TP, 128), lambda i, b, p: (p[i], 0)),
            pl.BlockSpec((TP, 128), lambda i, b, p: (p[i], 0)),
            full(sgg)],
        out_specs=[tok(AW), tok(256), tok(256), tok(RW),
                   pl.BlockSpec((NLG, TP, 128), lambda i, b, p: (0, i, 0)), tok(SGW), tok(SGW)])
    sds = lambda w, dt: jax.ShapeDtypeStruct((n, w), dt)
    return pl.pallas_call(
        functools.partial(_pre_kernel, split), grid_spec=grid_spec,
        out_shape=[sds(AW, BF16), sds(256, BF16), sds(256, BF16), sds(RW, BF16),
                   jax.ShapeDtypeStruct((NLG, n, 128), F32), sds(SGW, BF16), sds(SGW, BF16)],
        compiler_params=pltpu.CompilerParams(dimension_semantics=("parallel",), vmem_limit_bytes=VMEM_LIMIT),
        name="pre_mixer",
    )(meta["bidx"], meta["pblk"], *x_parts, mod_l, w_ext, cos_t, sin_t, sgg)


def _attn_kernel(first_ref, last_ref, sink_ref, q_ref, k_ref, kp_ref, kn_ref, v_ref, vp_ref, vn_ref,
                 o_ref, kw_s, vw_s):
    i = pl.program_id(0)
    is_first = first_ref[i] == 1
    is_last = last_ref[i] == 1
    kw_s[0:QB] = kp_ref[...]
    kw_s[QB:QB + TA] = k_ref[...]
    kw_s[QB + TA:] = kn_ref[...]
    vw_s[0:QB] = vp_ref[...]
    vw_s[QB:QB + TA] = v_ref[...]
    vw_s[QB + TA:] = vn_ref[...]
    iq = lax.broadcasted_iota(jnp.int32, (QB, 3 * QB), 0)
    ik = lax.broadcasted_iota(jnp.int32, (QB, 3 * QB), 1)
    rel = ik - iq
    band = (rel >= 0) & (rel <= 2 * WINDOW)
    lane = lax.broadcasted_iota(jnp.int32, (1, 128), 1)
    lo_half = lane < HD
    zero = jnp.zeros((), BF16)
    gsz = AW // HD // 2
    row_head = lax.broadcasted_iota(jnp.int32, (gsz * QB, 1), 0) // QB
    for j in range(TA // QB):
        ok = band
        if j == 0:
            ok = ok & ((ik >= QB) | jnp.logical_not(is_first))
        if j == TA // QB - 1:
            ok = ok & ((ik < 2 * QB) | jnp.logical_not(is_last))
        bias = jnp.where(ok, 0.0, NEG)
        bias = jnp.concatenate([bias] * gsz, axis=0)
        rows = slice(j * QB, (j + 1) * QB)
        keys = slice(j * QB, j * QB + 3 * QB)
        outs = [None] * (AW // HD)
        for g in range(2):
            heads = range(g * gsz, (g + 1) * gsz)
            qs = []
            for hh in heads:
                qg = q_ref[rows, (hh // 2) * 128:(hh // 2 + 1) * 128]
                qs.append(jnp.where(lo_half if hh % 2 == 0 else jnp.logical_not(lo_half), qg, zero))
            s = lax.dot_general(jnp.concatenate(qs, axis=0), kw_s[keys, g * 128:(g + 1) * 128],
                                (((1,), (1,)), ((), ())), preferred_element_type=F32) + bias
            sink = jnp.full((gsz * QB, 1), sink_ref[heads[-1]], F32)
            for t in range(gsz - 1):
                sink = jnp.where(row_head == t, sink_ref[heads[t]], sink)
            m = jnp.maximum(jnp.max(s, axis=-1, keepdims=True), sink)
            p = jnp.exp(s - m)
            vf = vw_s[keys, g * 128:(g + 1) * 128]
            pv = jnp.dot(p.astype(BF16), jnp.concatenate([vf, jnp.ones_like(vf)], axis=1),
                         preferred_element_type=F32)
            o = pv[:, :128] * (1.0 / (pv[:, 128:] + jnp.exp(sink - m)))
            for t, hh in enumerate(heads):
                outs[hh] = o[t * QB:(t + 1) * QB]
        for jg in range(AW // 128):
            o_ref[rows, jg * 128:(jg + 1) * 128] = jnp.where(lo_half, outs[2 * jg], outs[2 * jg + 1]).astype(o_ref.dtype)


def _attention(meta, sink, q, kk, vv):
    n = q.shape[0]
    nt = n // TA
    nqb = n // QB
    r = TA // QB
    main = lambda w: pl.BlockSpec((TA, w), lambda i, f, l, s: (i, 0))
    prev = pl.BlockSpec((QB, 256), lambda i, f, l, s: (jnp.maximum(i * r - 1, 0), 0))
    nxt = pl.BlockSpec((QB, 256), lambda i, f, l, s: (jnp.minimum(i * r + r, nqb - 1), 0))
    grid_spec = pltpu.PrefetchScalarGridSpec(
        num_scalar_prefetch=3, grid=(nt,),
        in_specs=[main(AW), main(256), prev, nxt, main(256), prev, nxt],
        out_specs=main(AW),
        scratch_shapes=[pltpu.VMEM((TA + 2 * QB, 256), BF16), pltpu.VMEM((TA + 2 * QB, 256), BF16)])
    return pl.pallas_call(
        _attn_kernel, grid_spec=grid_spec,
        out_shape=jax.ShapeDtypeStruct((n, AW), BF16),
        compiler_params=pltpu.CompilerParams(dimension_semantics=("parallel",), vmem_limit_bytes=VMEM_LIMIT),
        name="attention",
    )(meta["first"], meta["last"], sink, q, kk, kk, kk, vv, vv, vv)


def _rec_kernel(cf_ref, cb_ref, first_ref, last_ref,
                x_hbm, xfp_ref, xfn_ref, xbp_ref, xbn_ref,
                cw_ref, cbias_ref, wg_ref, bg_ref, lam_ref,
                hf_hbm, hb_hbm,
                ext_s, a_s, b_s, hs_s, ps_s, out_s, carry_s, gsem, ssem):
    i = pl.program_id(0)
    nsteps = pl.num_programs(0)
    slot = i % 2
    cw = cw_ref[...]
    cbias = cbias_ref[...]
    sub = lax.broadcasted_iota(jnp.int32, (8, 128), 0)
    unroll = 8
    chunk_refs = (cf_ref, cb_ref)
    out_hbm = (hf_hbm, hb_hbm)

    def gather(step, buf, d, start):
        c = chunk_refs[d][step]
        for g in range(NLG):
            for s in range(8):
                cp = pltpu.make_async_copy(x_hbm.at[g, pl.ds(pl.multiple_of(c * TR + s * SEG, SEG), SEG), :],
                                           ext_s.at[buf, d, g, pl.ds(2, SEG), s, :], gsem.at[buf, d])
                if start:
                    cp.start(priority=s % 2)
                else:
                    cp.wait()

    def scatter(d, c, start):
        for g in range(NLG):
            for s in range(8):
                cp = pltpu.make_async_copy(out_s.at[d, g, :, s, :],
                                           out_hbm[d].at[g, pl.ds(pl.multiple_of(c * TR + s * SEG, SEG), SEG), :],
                                           ssem.at[d])
                if start:
                    cp.start(priority=s % 2)
                else:
                    cp.wait()

    @pl.when(i == 0)
    def _():
        gather(0, 0, 0, True)
        gather(0, 0, 1, True)

    @pl.when(i + 1 < nsteps)
    def _():
        gather(i + 1, 1 - slot, 0, True)
        gather(i + 1, 1 - slot, 1, True)

    def run(d, xp_ref, xn_ref):
        c = chunk_refs[d][i]
        first = first_ref[c] == 1
        last = last_ref[c] == 1
        gather(i, slot, d, False)
        ext = ext_s.at[slot, d]
        planes = []
        for g in range(NLG):
            prev = jnp.where(first, 0.0, xp_ref[g])
            nxt = jnp.where(last, 0.0, xn_ref[g])
            ext[g, 0] = jnp.where(sub == 0, prev[6:7], pltpu.roll(ext[g, SEG], 1, 0))
            ext[g, 1] = jnp.where(sub == 0, prev[7:8], pltpu.roll(ext[g, SEG + 1], 1, 0))
            ext[g, SEG + 2] = jnp.where(sub == 7, nxt[0:1], pltpu.roll(ext[g, 2], 7, 0))
            lanes = slice(g * 128, (g + 1) * 128)
            acc = ext[g, 0:SEG] * cw[0:1, lanes]
            for t in range(1, CONV_WIDTH):
                acc = acc + ext[g, t:t + SEG] * cw[t:t + 1, lanes]
            planes.append(acc.reshape(TR, 128))
        xc = jnp.concatenate(planes, axis=1) + cbias
        xb16 = xc.astype(BF16)
        tg = jnp.tanh(jnp.dot(xb16, wg_ref[d], preferred_element_type=F32) + bg_ref[d:d + 1])
        tr, ti = tg[:, :RW], tg[:, RW:]
        nlam = -lam_ref[d:d + 1]
        softplus = jnp.maximum(nlam, 0.0) + jnp.log1p(jnp.exp(-jnp.abs(nlam)))
        half_c = (-0.5 * LRU_C) * softplus
        log_a = half_c * tr + half_c
        a = jnp.exp(log_a)
        v = (1.0 + a * a) * jnp.tanh(-log_a)
        b = jnp.where(v > 0.0, v * lax.rsqrt(v), 0.0) * (ti + 1.0) * (0.5 * xc)
        for g in range(NLG):
            a_s[g] = a[:, g * 128:(g + 1) * 128].reshape(SEG, 8, 128)
            b_s[g] = b[:, g * 128:(g + 1) * 128].reshape(SEG, 8, 128)

        reverse = d == 1
        reset = last if reverse else first

        @pl.when(reset)
        def _():
            carry_s[d] = jnp.zeros((NLG, 128), F32)

        def step(jo, hp):
            hp = list(hp)
            for ji in range(unroll):
                jj = jo * unroll + ji
                j = SEG - 1 - jj if reverse else jj
                for g in range(NLG):
                    h, p = hp[g]
                    ag = a_s[g, j]
                    h = ag * h + b_s[g, j]
                    p = p * ag
                    hs_s[g, j] = h
                    ps_s[g, j] = p
                    hp[g] = (h, p)
            return tuple(hp)

        init = tuple((jnp.zeros((8, 128), F32), jnp.ones((8, 128), F32)) for _ in range(NLG))
        ends = lax.fori_loop(0, SEG // unroll, step, init)
        cmats = []
        for g in range(NLG):
            e, pe = ends[g]
            c_in = carry_s[d, g:g + 1]
            rows = [None] * 8
            for s in (range(7, -1, -1) if reverse else range(8)):
                rows[s] = c_in
                c_in = e[s:s + 1] + pe[s:s + 1] * c_in
            carry_s[d, g:g + 1] = c_in
            cmats.append(jnp.concatenate(rows, axis=0))

        @pl.when(i > 0)
        def _():
            scatter(d, 0, False)

        def fix(jo, carry):
            for ji in range(unroll):
                j = jo * unroll + ji
                for g in range(NLG):
                    out_s[d, g, j] = hs_s[g, j] + ps_s[g, j] * cmats[g]
            return carry

        lax.fori_loop(0, SEG // unroll, fix, 0)
        scatter(d, c, True)

    run(0, xfp_ref, xfn_ref)
    run(1, xbp_ref, xbn_ref)

    @pl.when(i == nsteps - 1)
    def _():
        scatter(0, 0, False)
        scatter(1, 0, False)


def _recurrent(meta, xr, cw, cbias, wg, bg, lam):
    n = xr.shape[1]
    nt = n // TR
    n8 = n // 8
    r8 = TR // 8

    def halos(which):
        sel = (lambda cf, cb: cf) if which == 0 else (lambda cf, cb: cb)
        prev = pl.BlockSpec((NLG, 8, 128),
                            lambda i, cf, cb, f, l: (0, jnp.maximum(sel(cf, cb)[i] * r8 - 1, 0), 0))
        nxt = pl.BlockSpec((NLG, 8, 128),
                           lambda i, cf, cb, f, l: (0, jnp.minimum(sel(cf, cb)[i] * r8 + r8, n8 - 1), 0))
        return prev, nxt

    full = lambda a: pl.BlockSpec(a.shape, lambda i, cf, cb, f, l: (0,) * a.ndim)
    anywhere = pl.BlockSpec(memory_space=pl.ANY)
    plane = lambda lead: pltpu.VMEM(lead + (NLG, SEG, 8, 128), F32)
    grid_spec = pltpu.PrefetchScalarGridSpec(
        num_scalar_prefetch=4, grid=(nt,),
        in_specs=[anywhere, *halos(0), *halos(1),
                  full(cw), full(cbias), full(wg), full(bg), full(lam)],
        out_specs=[anywhere, anywhere],
        scratch_shapes=[pltpu.VMEM((2, 2, NLG, SEG + 3, 8, 128), F32),
                        plane(()), plane(()), plane(()), plane(()), plane((2,)),
                        pltpu.VMEM((2, NLG, 128), F32),
                        pltpu.SemaphoreType.DMA((2, 2)), pltpu.SemaphoreType.DMA((2,))])
    return pl.pallas_call(
        _rec_kernel, grid_spec=grid_spec,
        out_shape=[jax.ShapeDtypeStruct((NLG, n, 128), F32), jax.ShapeDtypeStruct((NLG, n, 128), F32)],
        compiler_params=pltpu.CompilerParams(dimension_semantics=("arbitrary",), vmem_limit_bytes=VMEM_LIMIT),
        name="recurrent",
    )(meta["cf"], meta["cb"], meta["first"], meta["last"], xr, xr, xr, xr, xr, cw, cbias, wg, bg, lam)


def _route(sel, score):
    def before(vk, vj, k, j):
        return (vk > vj) | ((vk == vj) & (k < j)) if k < j else (vk > vj)

    def count(flags):
        return sum(jnp.where(f, 1.0, 0.0) for f in flags)

    slot = lambda a, j: a[j * NG:(j + 1) * NG]
    cand = [slot(sel, j) for j in range(EPG)]
    in_top = [jnp.where(count(before(cand[k], cand[j], k, j) for k in range(EPG) if k != j) < 2.0, 1.0, 0.0)
              for j in range(EPG)]
    gscore = sum(in_top[j] * cand[j] for j in range(EPG))
    grow = [gscore[g:g + 1] for g in range(NG)]
    best = jnp.concatenate(
        [jnp.where(count(before(grow[k], grow[g], k, g) for k in range(NG) if k != g) < 1.0, 1.0, 0.0)
         for g in range(NG)], axis=0)
    picked = [in_top[j] * best * slot(score, j) for j in range(EPG)]
    inv = 1.0 / sum(jnp.sum(p, axis=0, keepdims=True) for p in picked)
    mask = jnp.concatenate([in_top[j] * best for j in range(EPG)], axis=0)
    gate = jnp.concatenate([p * inv for p in picked], axis=0)
    return mask, gate


def _slot_order():
    return np.asarray([g * EPG + j for j in range(EPG) for g in range(NG)], np.int32)


def _merge_kernel(split, bidx_ref, *refs):
    del bidx_ref
    nx = len(refs) - 22
    x_refs = refs[:nx]
    (oa_ref, gy_ref, hf_ref, hb_ref, u_ref, vn_ref, mod1_ref, mod2_ref, gmix_ref, sgw_ref, sgb_ref, wout_ref,
     lng_ref, lnb_ref, rwt_ref, rb_ref, x1_ref, h2_ref, gate_ref, mask_ref, cnt_ref, mrg_s) = refs[nx:]
    gmix = gmix_ref[...]
    mrg_s[:, 0:AW] = (_rms(oa_ref[...].astype(F32)) * gmix[:, 0:AW]).astype(BF16)
    hsum = jnp.concatenate([hf_ref[g] + hb_ref[g] for g in range(NLG)], axis=1)
    o_rec = gy_ref[...].astype(F32) * hsum
    mrg_s[:, AW:AW + RW] = (_rms(o_rec) * gmix[:, AW:AW + RW]).astype(BF16)
    lane = lax.broadcasted_iota(jnp.int32, (1, 128), 1)
    lo_half = lane < HD
    zero = jnp.zeros((), BF16)
    pieces = []
    for c in range(TM // CHUNK):
        rows = slice(c * CHUNK, (c + 1) * CHUNK)
        grp = []
        for g in range(SGW // 128):
            vg = vn_ref[rows, g * 128:(g + 1) * 128]
            mixed = jnp.dot(sgw_ref[2 * g], jnp.where(lo_half, vg, zero), preferred_element_type=F32)
            mixed += jnp.dot(sgw_ref[2 * g + 1], jnp.where(lo_half, zero, vg), preferred_element_type=F32)
            grp.append(mixed)
        mixed = jnp.concatenate(grp, axis=1) + sgb_ref[...]
        pieces.append(u_ref[rows, :].astype(F32) * mixed)
    o_sg = jnp.concatenate(pieces, axis=0)
    mrg_s[:, AW + RW:] = (_rms(o_sg) * gmix[:, AW + RW:]).astype(BF16)
    o = jnp.dot(mrg_s[...], wout_ref[...], preferred_element_type=F32)
    gate1 = mod1_ref[...][:, 2 * D:]
    x_in = _load_tokens(pl.program_id(0), x_refs, split)
    x1 = _ln(ALPHA * x_in + gate1 * o) * lng_ref[...] + lnb_ref[...]
    x1_ref[...] = x1
    mod2 = mod2_ref[...]
    h2 = (_ln(x1) * (1.0 + mod2[:, D:2 * D]) + mod2[:, :D]).astype(BF16)
    h2_ref[...] = h2
    logits = lax.dot_general(rwt_ref[...], h2, (((1,), (1,)), ((), ())), preferred_element_type=F32)
    score = _sigmoid(logits)
    sel = score + rb_ref[...]
    mask_t, gate_t = _route(sel, score)
    gate_ref[...] = gate_t
    mask_ref[...] = mask_t
    t_i = lax.broadcasted_iota(jnp.int32, (TM, 128), 0)
    j_i = lax.broadcasted_iota(jnp.int32, (TM, 128), 1)
    in_tile = jnp.where((t_i >= j_i * TD) & (t_i < (j_i + 1) * TD), 1.0, 0.0).astype(BF16)
    cnt_ref[...] = jnp.dot(mask_t.astype(BF16), in_tile, preferred_element_type=F32)


def _merge(meta, x_parts, oa, gy, hf, hb, u, vn, mod1, mod2, gmix, sgw, sgb, wout, lng, lnb, rwt, rb):
    n = oa.shape[0]
    nt = n // TM
    x_specs, split = _token_specs(x_parts, TM)
    tok = lambda w: pl.BlockSpec((TM, w), lambda i, b: (i, 0))
    full = lambda a: pl.BlockSpec(a.shape, lambda i, b: (0,) * a.ndim)
    modspec = pl.BlockSpec((None, 1, 3 * D), lambda i, b: (b[i], 0, 0))
    tspec = pl.BlockSpec((NE, TM), lambda i, b: (0, i))
    rec = pl.BlockSpec((NLG, TM, 128), lambda i, b: (0, i, 0))
    grid_spec = pltpu.PrefetchScalarGridSpec(
        num_scalar_prefetch=1, grid=(nt,),
        in_specs=x_specs + [
            tok(AW), tok(RW), rec, rec, tok(SGW), tok(SGW), modspec, modspec,
            full(gmix), full(sgw), full(sgb), full(wout), full(lng), full(lnb), full(rwt), full(rb)],
        out_specs=[tok(D), tok(D), tspec, tspec, pl.BlockSpec((None, NE, 128), lambda i, b: (i, 0, 0))],
        scratch_shapes=[pltpu.VMEM((TM, D), BF16)])
    return pl.pallas_call(
        functools.partial(_merge_kernel, split), grid_spec=grid_spec,
        out_shape=[jax.ShapeDtypeStruct((n, D), F32), jax.ShapeDtypeStruct((n, D), BF16),
                   jax.ShapeDtypeStruct((NE, n), F32), jax.ShapeDtypeStruct((NE, n), F32),
                   jax.ShapeDtypeStruct((nt, NE, 128), F32)],
        compiler_params=pltpu.CompilerParams(dimension_semantics=("parallel",), vmem_limit_bytes=VMEM_LIMIT),
        name="merge_route",
    )(meta["bidx"], *x_parts, oa, gy, hf, hb, u, vn, mod1, mod2, gmix, sgw, sgb, wout, lng, lnb, rwt, rb)


def _plan(cnt, n_mt):
    pc = (cnt + (ALIGN - 1)) // ALIGN * ALIGN
    lo = jnp.cumsum(pc, axis=1) - pc
    tot = jnp.sum(pc, axis=0)
    seg = (tot + (TMX - 1)) // TMX * TMX
    gend = jnp.cumsum(seg)
    gstart = gend - seg
    dst = gstart[None, :] + jnp.cumsum(pc, axis=0) - pc
    tile_row = jnp.arange(n_mt, dtype=jnp.int32) * TMX
    texp = jnp.minimum(jnp.sum((gend[None, :] <= tile_row[:, None]).astype(jnp.int32), axis=1), NE - 1)
    texp = jnp.asarray(_slot_order())[texp]
    nact = (gend[-1] // TMX).reshape(1)
    grp = lambda a: (a // ALIGN).astype(jnp.int32)
    return dict(pc=grp(pc).reshape(-1), lo=grp(lo).reshape(-1), dst=grp(dst).reshape(-1),
                tsum=grp(jnp.sum(pc, axis=1)),
                zdst=grp(jnp.concatenate([gstart + tot, gend[-1:]])), zlen=grp(seg - tot),
                texp=texp.astype(jnp.int32), nact=nact.astype(jnp.int32))


def _run_copies(src, dst, src_off, dst_off, groups, sem, max_bits):
    def arm(b):
        @pl.when(((groups >> b) & 1) == 1)
        def _():
            off = (groups >> (b + 1)) << (b + 1)
            pltpu.make_async_copy(src.at[pl.ds(src_off + off, 1 << b)], dst.at[pl.ds(dst_off + off, 1 << b)],
                                  sem).start()

    for b in range(min(COMMON_BITS, max_bits)):
        arm(b)
    if max_bits > COMMON_BITS:
        @pl.when(groups >= (1 << COMMON_BITS))
        def _():
            for b in range(COMMON_BITS, max_bits):
                arm(b)


def _wait_groups(buf, groups, sem, max_bits):
    for b in range(max_bits):
        @pl.when(((groups >> b) & 1) == 1)
        def _():
            pltpu.make_async_copy(buf.at[pl.ds(0, 1 << b)], buf.at[pl.ds(0, 1 << b)], sem).wait()


COMMON_BITS = 3
RUN_BITS = (TD // ALIGN).bit_length()
TILE_BITS = (RL // ALIGN).bit_length()


def _dispatch_kernel(pc_ref, lo_ref, dst_ref, ts_ref, zd_ref, zl_ref, mask_ref, gate_ref, h_ref,
                     xs_ref, info_ref, loc_s, zero_s, earlier_s, sems):
    i = pl.program_id(0)
    last_step = pl.num_programs(0) - 1
    slot = i % 2
    mask = mask_ref[...]
    mb = mask.astype(BF16)
    @pl.when(i == 0)
    def _():
        s_i = lax.broadcasted_iota(jnp.int32, (TD, TD), 0)
        t_i = lax.broadcasted_iota(jnp.int32, (TD, TD), 1)
        earlier_s[...] = jnp.where(s_i < t_i, 1.0, 0.0).astype(BF16)

    rank = jnp.dot(mb, earlier_s[...], preferred_element_type=F32)
    e_i = lax.broadcasted_iota(jnp.int32, (NE, NE), 0)
    f_i = lax.broadcasted_iota(jnp.int32, (NE, NE), 1)
    below = jnp.where(f_i < e_i, 1.0, 0.0).astype(BF16)
    lower = jnp.dot(below, mb, preferred_element_type=F32)
    row_e = lax.broadcasted_iota(jnp.int32, (NE, 1), 0)
    lo_vec = jnp.zeros((NE, 1), F32)
    for e in range(NE):
        lo_vec = jnp.where(row_e == e, (lo_ref[i * NE + e] * ALIGN).astype(F32), lo_vec)
    row = lo_vec + rank
    is0 = mask * jnp.where(lower == 0.0, 1.0, 0.0)
    is1 = mask - is0
    d0 = jnp.sum(is0 * row, axis=0, keepdims=True)
    d1 = jnp.sum(is1 * row, axis=0, keepdims=True)
    gate = gate_ref[...]
    w0 = jnp.sum(is0 * gate, axis=0, keepdims=True)
    w1 = jnp.sum(is1 * gate, axis=0, keepdims=True)
    info_ref[...] = jnp.concatenate([d0, d1, w0, w1, jnp.zeros((4, TD), F32)], axis=0)
    r_i = lax.broadcasted_iota(jnp.int32, (RL, TD), 0)
    perm = jnp.where((r_i == d0.astype(jnp.int32)) | (r_i == d1.astype(jnp.int32)), 1.0, 0.0).astype(BF16)
    loc = loc_s.at[slot]
    loc[...] = jnp.dot(perm, h_ref[...], preferred_element_type=F32).astype(BF16).reshape(RL // ALIGN, ALIGN, D)
    for e in range(NE):
        _run_copies(loc, xs_ref, lo_ref[i * NE + e], dst_ref[i * NE + e], pc_ref[i * NE + e], sems.at[slot],
                    RUN_BITS)

    @pl.when(i > 0)
    def _():
        _wait_groups(loc_s.at[1 - slot], ts_ref[jnp.maximum(i - 1, 0)], sems.at[1 - slot], TILE_BITS)

    @pl.when(i == last_step)
    def _():
        _wait_groups(loc, ts_ref[i], sems.at[slot], TILE_BITS)
        sem = sems.at[0]
        zero_s[...] = jnp.zeros_like(zero_s)
        zg = ZROWS // ALIGN
        for wait in (False, True):
            for e in range(NE):
                for part in range(TMX // ZROWS):
                    groups = jnp.clip(zl_ref[e] - part * zg, 0, zg)
                    if wait:
                        _wait_groups(zero_s, groups, sem, zg.bit_length())
                    else:
                        _run_copies(zero_s, xs_ref, 0, zd_ref[e] + part * zg, groups, sem, zg.bit_length())
        tail = zd_ref[NE]
        chunks = (xs_ref.shape[0] - tail) // zg

        def fill(c, carry):
            pltpu.make_async_copy(zero_s, xs_ref.at[pl.ds(tail + c * zg, zg)], sem).start()
            return carry

        def drain(c, carry):
            pltpu.make_async_copy(zero_s, zero_s, sem).wait()
            return carry

        lax.fori_loop(0, chunks, fill, 0)
        lax.fori_loop(0, chunks, drain, 0)


def _dispatch(plan, mask_t, gate_t, h2, rtot):
    n = h2.shape[0]
    ntd = n // TD
    tspec = pl.BlockSpec((NE, TD), lambda i, *_: (0, i))
    grid_spec = pltpu.PrefetchScalarGridSpec(
        num_scalar_prefetch=6, grid=(ntd,),
        in_specs=[tspec, tspec, pl.BlockSpec((TD, D), lambda i, *_: (i, 0))],
        out_specs=[pl.BlockSpec(memory_space=pl.ANY), pl.BlockSpec((8, TD), lambda i, *_: (0, i))],
        scratch_shapes=[pltpu.VMEM((2, RL // ALIGN, ALIGN, D), BF16), pltpu.VMEM((ZROWS // ALIGN, ALIGN, D), BF16),
                        pltpu.VMEM((TD, TD), BF16), pltpu.SemaphoreType.DMA((2,))])
    return pl.pallas_call(
        _dispatch_kernel, grid_spec=grid_spec,
        out_shape=[jax.ShapeDtypeStruct((rtot // ALIGN, ALIGN, D), BF16), jax.ShapeDtypeStruct((8, n), F32)],
        compiler_params=pltpu.CompilerParams(dimension_semantics=("arbitrary",), vmem_limit_bytes=VMEM_LIMIT),
        name="dispatch",
    )(plan["pc"], plan["lo"], plan["dst"], plan["tsum"], plan["zdst"], plan["zlen"], mask_t, gate_t, h2)


def _expert_kernel(texp_ref, nact_ref, x_ref, w1_ref, w3_ref, w2_ref, y_ref, w1_s, w3_s, w2_s):
    m = pl.program_id(0)
    new_expert = (m == 0) | (texp_ref[m] != texp_ref[jnp.maximum(m - 1, 0)])

    @pl.when(new_expert)
    def _():
        w1_s[...] = w1_ref[...].astype(BF16)
        w3_s[...] = w3_ref[...].astype(BF16)
        w2_s[...] = w2_ref[...].astype(BF16)

    @pl.when(m < nact_ref[0])
    def _():
        x = x_ref[...]
        a = jnp.dot(x, w1_s[...], preferred_element_type=F32)
        a = a * _sigmoid(a) * jnp.dot(x, w3_s[...], preferred_element_type=F32)
        y_ref[...] = jnp.dot(a.astype(BF16), w2_s[...], preferred_element_type=F32).astype(BF16)

    @pl.when(m >= nact_ref[0])
    def _():
        y_ref[...] = jnp.zeros_like(y_ref)


def _experts(plan, xs, layer, w1, w3, w2):
    rtot = xs.shape[0]
    n_mt = rtot // TMX
    grid_spec = pltpu.PrefetchScalarGridSpec(
        num_scalar_prefetch=2, grid=(n_mt,),
        in_specs=[pl.BlockSpec((TMX, D), lambda m, te, na: (jnp.minimum(m, na[0] - 1), 0)),
                  pl.BlockSpec((None, None, D, FF), lambda m, te, na: (layer, te[m], 0, 0)),
                  pl.BlockSpec((None, None, D, FF), lambda m, te, na: (layer, te[m], 0, 0)),
                  pl.BlockSpec((None, None, FF, D), lambda m, te, na: (layer, te[m], 0, 0))],
        out_specs=pl.BlockSpec((TMX, D), lambda m, te, na: (m, 0)),
        scratch_shapes=[pltpu.VMEM((D, FF), BF16), pltpu.VMEM((D, FF), BF16), pltpu.VMEM((FF, D), BF16)])
    return pl.pallas_call(
        _expert_kernel, grid_spec=grid_spec,
        out_shape=jax.ShapeDtypeStruct((rtot, D), BF16),
        compiler_params=pltpu.CompilerParams(dimension_semantics=("arbitrary",), vmem_limit_bytes=VMEM_LIMIT),
        name="experts",
    )(plan["texp"], plan["nact"], xs, w1, w3, w2)


def _combine_kernel(split, bidx_ref, pc_ref, lo_ref, dst_ref, ts_ref, info_ref, ys_ref, x1_ref, mod_ref,
                    lng_ref, lnb_ref, *rest):
    del bidx_ref
    o_refs, (loc_s, sems) = rest[:-2], rest[-2:]
    i = pl.program_id(0)
    slot = i % 2

    def fetch(tile, buf):
        for e in range(NE):
            _run_copies(ys_ref, loc_s.at[buf], dst_ref[tile * NE + e], lo_ref[tile * NE + e],
                        pc_ref[tile * NE + e], sems.at[buf], RUN_BITS)

    @pl.when(i == 0)
    def _():
        loc_s[...] = jnp.zeros_like(loc_s)
        fetch(0, 0)

    @pl.when(i + 1 < pl.num_programs(0))
    def _():
        fetch(i + 1, 1 - slot)

    _wait_groups(loc_s.at[slot], ts_ref[i], sems.at[slot], TILE_BITS)
    info = info_ref[...]
    d0 = info[0:1].astype(jnp.int32)
    d1 = info[1:2].astype(jnp.int32)
    r_i = lax.broadcasted_iota(jnp.int32, (RL, TD), 0)
    wperm = (jnp.where(r_i == d0, info[2:3], 0.0) + jnp.where(r_i == d1, info[3:4], 0.0)).astype(BF16)
    moe = lax.dot_general(wperm, loc_s[slot].reshape(RL, D), (((0,), (0,)), ((), ())),
                          preferred_element_type=F32)
    gate = mod_ref[...][:, 2 * D:]
    out = _ln(ALPHA * x1_ref[...] + gate * moe) * lng_ref[...] + lnb_ref[...]
    if len(o_refs) == 1:
        o_refs[0][...] = out
    else:
        @pl.when(i < split)
        def _():
            o_refs[0][...] = out

        @pl.when(i >= split)
        def _():
            o_refs[1][...] = out


def _combine(bidx_d, plan, info, ys, x1, mod2, lng, lnb, out_rows):
    n = x1.shape[0]
    ntd = n // TD
    tok = pl.BlockSpec((TD, D), lambda i, *_: (i, 0))
    full = lambda a: pl.BlockSpec(a.shape, lambda i, *_: (0,) * a.ndim)
    if len(out_rows) == 1:
        split, out_specs = 0, [tok]
    else:
        split = out_rows[0] // TD
        out_specs = [pl.BlockSpec((TD, D), lambda i, *_: (jnp.minimum(i, split - 1), 0)),
                     pl.BlockSpec((TD, D), lambda i, *_: (jnp.maximum(i - split, 0), 0))]
    grid_spec = pltpu.PrefetchScalarGridSpec(
        num_scalar_prefetch=5, grid=(ntd,),
        in_specs=[pl.BlockSpec((8, TD), lambda i, *_: (0, i)),
                  pl.BlockSpec(memory_space=pl.ANY),
                  tok,
                  pl.BlockSpec((None, 1, 3 * D), lambda i, b, *_: (b[i], 0, 0)),
                  full(lng), full(lnb)],
        out_specs=out_specs,
        scratch_shapes=[pltpu.VMEM((2, RL // ALIGN, ALIGN, D), BF16), pltpu.SemaphoreType.DMA((2,))])
    return pl.pallas_call(
        functools.partial(_combine_kernel, split), grid_spec=grid_spec,
        out_shape=[jax.ShapeDtypeStruct((r, D), F32) for r in out_rows],
        compiler_params=pltpu.CompilerParams(dimension_semantics=("arbitrary",), vmem_limit_bytes=VMEM_LIMIT),
        name="combine",
    )(bidx_d, plan["pc"], plan["lo"], plan["dst"], plan["tsum"], info, ys, x1, mod2, lng, lnb)


def _moe(meta, h2, mask_t, gate_t, cnt, layer, w1, w3, w2, x1, mod2, lng, lnb, out_rows):
    n = h2.shape[0]
    ntd = n // TD
    rtot = -(-(ntd * RL) // TMX) * TMX + NE * TMX
    cnt = cnt[:, :, :TM // TD].transpose(0, 2, 1).reshape(ntd, NE).astype(jnp.int32)
    plan = _plan(cnt, rtot // TMX)
    xs, info = _dispatch(plan, mask_t, gate_t, h2, rtot)
    ys = _experts(plan, xs.reshape(rtot, D), layer, w1, w3, w2).reshape(rtot // ALIGN, ALIGN, D)
    return _combine(meta["bidx"], plan, info, ys, x1, mod2, lng, lnb, out_rows)


def _tile_meta(groups, rows):
    bidx, pblk, first, last, cf, cb = [], [], [], [], [], []
    row = 0
    tile = 0
    for (b, s) in groups:
        assert s % rows == 0
        per = s // rows
        for bi in range(b):
            for j in range(per):
                bidx.append(row + bi)
                pblk.append(j)
                first.append(1 if j == 0 else 0)
                last.append(1 if j == per - 1 else 0)
                cf.append(tile + j)
                cb.append(tile + per - 1 - j)
            tile += per
        row += b
    as_i32 = lambda v: jnp.asarray(np.asarray(v, np.int32))
    return dict(bidx=as_i32(bidx), pblk=as_i32(pblk), first=as_i32(first), last=as_i32(last),
                cf=as_i32(cf), cb=as_i32(cb))


def _rope_tables(s_max):
    inv = jnp.power(ROPE_THETA, -jnp.arange(0, HD, 2, dtype=F32) / HD)
    ang = jnp.arange(s_max, dtype=F32)[:, None] * inv[None, :]
    cos, sin = jnp.cos(ang), jnp.sin(ang)
    cos128 = jnp.tile(cos, (1, 4))
    sin128 = jnp.tile(jnp.concatenate([-sin, sin], axis=1), (1, 2))
    return cos128, sin128


def _block_diag(w):
    d, nb, c, f = w.shape
    eye = jnp.eye(nb, dtype=w.dtype)
    return jnp.einsum('dncf,nm->dncmf', w, eye).reshape(d, nb * c, nb * f)


def _forward(xs, cs, w_mod, b_mod, w_in, attn_sink, conv_w, conv_b, lru_w_r, lru_b_r, lru_w_i, lru_b_i,
             lru_lambda, sg_norm_g, sg_w, sg_b, mix_norm_g, w_out, ln_g, ln_b, router_w, router_bias,
             exp_w1, exp_w3, exp_w2):
    groups = [(x.shape[0], x.shape[1]) for x in xs]
    assert len(groups) == 2
    meta_p, meta_a, meta_r = (_tile_meta(groups, rows) for rows in (TP, TA, TR))
    meta_m, meta_d = _tile_meta(groups, TM), _tile_meta(groups, TD)
    x_parts = [xx.reshape(-1, D) for xx in xs]
    group_rows = [p.shape[0] for p in x_parts]
    c_all = jnp.concatenate(cs, axis=0)
    bt = c_all.shape[0]
    mods = _modulation(c_all, w_mod, b_mod).reshape(2 * DEPTH, bt, 1, 3 * D)
    cos_t, sin_t = _rope_tables(max(s for _, s in groups))
    rwt = router_w.T[_slot_order()].astype(BF16)
    rb = router_bias[_slot_order()].reshape(NE, 1)
    for l in range(DEPTH):
        q, kk, vv, gy, xr, u, vn = _pre_mixer(meta_p, x_parts, mods[2 * l], w_in[l].astype(BF16), cos_t, sin_t,
                                              sg_norm_g[l].reshape(1, SGW))
        oa = _attention(meta_a, attn_sink[l], q, kk, vv)
        w_gates = jnp.concatenate([_block_diag(0.5 * lru_w_r[l]), _block_diag(0.5 * lru_w_i[l])], axis=2)
        b_gates = jnp.concatenate([0.5 * lru_b_r[l], 0.5 * lru_b_i[l]], axis=1)
        hf, hb = _recurrent(meta_r, xr, conv_w[l], conv_b[l].reshape(1, RW), w_gates.astype(BF16), b_gates,
                            lru_lambda[l])
        sgb = jnp.repeat(sg_b[l].T, HD, axis=1)
        x1, h2, gate_t, mask_t, cnt = _merge(meta_m, x_parts, oa, gy, hf, hb, u, vn, mods[2 * l], mods[2 * l + 1],
                                             mix_norm_g[l].reshape(1, D), sg_w[l].astype(BF16), sgb,
                                             w_out[l].astype(BF16), ln_g[l, 0].reshape(1, D),
                                             ln_b[l, 0].reshape(1, D), rwt, rb)
        out_rows = group_rows if l == DEPTH - 1 else [sum(group_rows)]
        x_parts = _moe(meta_d, h2, mask_t, gate_t, cnt, l, exp_w1, exp_w3, exp_w2, x1, mods[2 * l + 1],
                       ln_g[l, 1].reshape(1, D), ln_b[l, 1].reshape(1, D), out_rows)
    return tuple(p.reshape(b, s, D) for p, (b, s) in zip(x_parts, groups))


def kernel(x_prompt, x_sample, c_prompt, c_sample, w_mod, b_mod, w_in, attn_sink, conv_w, conv_b, lru_w_r, lru_b_r,
           lru_w_i, lru_b_i, lru_lambda, sg_norm_g, sg_w, sg_b, mix_norm_g, w_out, ln_g, ln_b, router_w,
           router_bias, exp_w1, exp_w3, exp_w2):
    return _forward([x_prompt, x_sample], [c_prompt, c_sample], w_mod, b_mod, w_in, attn_sink, conv_w, conv_b,
                    lru_w_r, lru_b_r, lru_w_i, lru_b_i, lru_lambda, sg_norm_g, sg_w, sg_b, mix_norm_g, w_out,
                    ln_g, ln_b, router_w, router_bias, exp_w1, exp_w3, exp_w2)
```
